```python
import math
import jax
import jax.numpy as jnp
from jax import lax
import numpy as np

D_MODEL = 2048
BATCH = 4
SEQ = 2048
DEPTH = 4
DEC_BATCH = 8
DEC_SEQ = 8
PAST_LEN = 16384
PAGE_SIZE = 128

H_A = 4
DK_A = 64
DV_A = 2 * DK_A
H_B = 4
DK_B = 128
DV_B = 256
RET_CHUNK = 128
RET_THETA = 10000.0
H_C = 4
DK_C = 128
CMP_BLOCK = 32
SEL_BLOCK = 64
TOPK = 16
WINDOW = 512
FORCED_SCORE = H_C + 1.0
ROPE_THETA = 500000.0
QBLOCK = 128
D_FF = 5504
CONV_W = 3
PLE_DIM = 256
ALPHA = (2 * DEPTH) ** 0.25
BETA = (8 * DEPTH) ** -0.25
EPS = 1e-5
IN_WIDTHS = (H_A * 2 * DK_A, H_A * 2 * DK_A, H_A * DV_A,
             H_B * DK_B, H_B * DK_B, H_B * DV_B, H_B * DV_B,
             H_C * DK_C, 6 * DK_C, 3 * H_C)
N_IN = sum(IN_WIDTHS)

kernel_name = 'hybrid_diffattn_retnet_nsa_decoder_step'


def layer_norm(x, g, b):
    xf = x.astype(jnp.float32)
    mu = jnp.mean(xf, -1, keepdims=True)
    var = jnp.mean(jnp.square(xf - mu), -1, keepdims=True)
    y = (xf - mu) * lax.rsqrt(var + EPS) * g.astype(jnp.float32) + b.astype(jnp.float32)
    return y.astype(x.dtype)


def head_rms_norm(x, g):
    xf = x.astype(jnp.float32)
    y = xf * lax.rsqrt(jnp.mean(jnp.square(xf), -1, keepdims=True) + EPS)
    return (y * g.astype(jnp.float32)).astype(x.dtype)


def head_group_norm(x, g):
    xf = x.astype(jnp.float32)
    mu = jnp.mean(xf, -1, keepdims=True)
    var = jnp.mean(jnp.square(xf - mu), -1, keepdims=True)
    return ((xf - mu) * lax.rsqrt(var + EPS) * g.astype(jnp.float32)).astype(x.dtype)


def rope(x, pos, rot_dim, theta):
    half = rot_dim // 2
    inv_freq = jnp.exp(-math.log(theta) * jnp.arange(half, dtype=jnp.float32) / half)
    ang = pos.astype(jnp.float32)[:, None] * inv_freq[None, :]
    bshape = (pos.shape[0],) + (1,) * (x.ndim - 3) + (half,)
    cos = jnp.cos(ang).reshape(bshape).astype(x.dtype)
    sin = jnp.sin(ang).reshape(bshape).astype(x.dtype)
    x1 = x[..., :half]
    x2 = x[..., half:rot_dim]
    return jnp.concatenate([x1 * cos - x2 * sin, x2 * cos + x1 * sin, x[..., rot_dim:]], axis=-1)


def masked_softmax(s, mask):
    neg = jnp.finfo(jnp.float32).min
    s = jnp.where(mask, s, neg)
    e = jnp.where(mask, jnp.exp(s - jnp.max(s, -1, keepdims=True)), 0.0)
    return e / jnp.maximum(jnp.sum(e, -1, keepdims=True), jnp.finfo(jnp.float32).tiny)


def map_query_blocks(fn, args, qpos):
    t = qpos.shape[0]
    if t <= QBLOCK or t % QBLOCK:
        return fn(*args, qpos)
    nb = t // QBLOCK

    def split(a):
        return jnp.moveaxis(a.reshape((a.shape[0], nb, QBLOCK) + a.shape[2:]), 1, 0)

    out = lax.map(lambda xs: fn(*xs[:-1], xs[-1]),
                  tuple(split(a) for a in args) + (qpos.reshape(nb, QBLOCK),))
    return jnp.moveaxis(out, 0, 1).reshape((out.shape[1], t) + out.shape[3:])


def diff_attention(q, qpos, kv, kpos, lam):
    k1 = kv[:, :, 0, :, :DK_A]
    k2 = kv[:, :, 0, :, DK_A:]
    v = kv[:, :, 1]
    scale = DK_A ** -0.5

    def block(q1, q2, qp):
        mask = kpos[None, :] <= qp[:, None]
        p1 = masked_softmax(jnp.einsum('bqhd,bkhd->bhqk', q1, k1).astype(jnp.float32) * scale, mask)
        p2 = masked_softmax(jnp.einsum('bqhd,bkhd->bhqk', q2, k2).astype(jnp.float32) * scale, mask)
        return jnp.einsum('bhqk,bkhv->bqhv', (p1 - lam * p2).astype(v.dtype), v)

    return map_query_blocks(block, (q[:, :, :, 0], q[:, :, :, 1]), qpos)


def retention(q, k, v, s0, chunk):
    b, t, h, _ = q.shape
    dv = v.shape[-1]
    nc = t // chunk
    log_g = jnp.log(1.0 - jnp.exp2(-5.0 - jnp.arange(h, dtype=jnp.float32)))
    idx = jnp.arange(chunk, dtype=jnp.float32)
    rel = idx[:, None] - idx[None, :]
    dmask = jnp.where(rel >= 0, jnp.exp(log_g[:, None, None] * jnp.maximum(rel, 0.0)), 0.0).astype(q.dtype)
    q_dec = jnp.exp(log_g[None, :] * (idx[:, None] + 1.0)).astype(q.dtype)
    k_dec = jnp.exp(log_g[None, :] * (chunk - 1.0 - idx[:, None])).astype(q.dtype)
    c_dec = jnp.exp(log_g * chunk).astype(q.dtype)

    def to_chunks(a):
        return jnp.moveaxis(a.reshape(b, nc, chunk, h, a.shape[-1]), 1, 0)

    def step(s, inp):
        qc, kc, vc = inp
        att = jnp.einsum('bihd,bjhd->bhij', qc, kc) * dmask
        o = jnp.einsum('bhij,bjhv->bihv', att, vc) + jnp.einsum('bihd,bhdv->bihv', qc, s) * q_dec[None, :, :, None]
        s_new = s * c_dec[None, :, None, None] + jnp.einsum('bjhd,bjhv->bhdv', kc * k_dec[None, :, :, None], vc)
        return s_new.astype(s.dtype), o.astype(vc.dtype)

    s_fin, o = lax.scan(step, s0, (to_chunks(q), to_chunks(k), to_chunks(v)))
    return jnp.moveaxis(o, 0, 1).reshape(b, t, h, dv), s_fin


def nsa_compressed_selected(q, qpos, k_cmp, v_cmp, k_sel, v_sel, cmp_pos, cmp_w):
    b, nq, h, dk = q.shape
    t = k_cmp.shape[1]
    t_pad = -(-t // SEL_BLOCK) * SEL_BLOCK
    n_cmp = t_pad // CMP_BLOCK
    n_sel = t_pad // SEL_BLOCK
    scale = dk ** -0.5

    def pad(a):
        return jnp.pad(a, ((0, 0), (0, t_pad - t), (0, 0)))

    def compress(a, pe, w):
        blocks = pad(a).reshape(b, n_cmp, CMP_BLOCK, dk) + pe
        return blocks.reshape(b, n_cmp, CMP_BLOCK * dk) @ w

    kc = compress(k_cmp, cmp_pos[0], cmp_w[0])
    vc = compress(v_cmp, cmp_pos[1], cmp_w[1])
    cmp_end = (jnp.arange(n_cmp) + 1) * CMP_BLOCK - 1
    p_cmp = masked_softmax(jnp.einsum('bqhd,bnd->bhqn', q, kc).astype(jnp.float32) * scale,
                           cmp_end[None, :] <= qpos[:, None])
    o_cmp = jnp.einsum('bhqn,bnd->bqhd', p_cmp.astype(vc.dtype), vc)
    imp = p_cmp.sum(axis=1).reshape(b, nq, n_sel, SEL_BLOCK // CMP_BLOCK).sum(-1)
    blk = jnp.arange(n_sel)[None, :]
    cur = (qpos // SEL_BLOCK)[:, None]
    forced = (blk == 0) | (blk == cur) | (blk == cur - 1)
    valid = blk * SEL_BLOCK <= qpos[:, None]
    score = jnp.where(valid, jnp.where(forced, FORCED_SCORE, imp), -1.0)
    top_s, top_i = lax.top_k(score, min(TOPK, n_sel))
    ok = top_s >= 0.0
    ks_blocks = pad(k_sel).reshape(b, n_sel, SEL_BLOCK, dk)
    vs_blocks = pad(v_sel).reshape(b, n_sel, SEL_BLOCK, dk)
    take = jax.vmap(lambda a, i: a[i])

    def sel_block(qb, ib, okb, qpb):
        kg = take(ks_blocks, ib)
        vg = take(vs_blocks, ib)
        kpos = ib[..., None] * SEL_BLOCK + jnp.arange(SEL_BLOCK)
        mask = (kpos <= qpb[None, :, None, None]) & okb[..., None]
        qn, kn = ib.shape[1], ib.shape[2]
        s = jnp.einsum('bqhd,bqksd->bhqks', qb, kg).astype(jnp.float32) * scale
        p = masked_softmax(s.reshape(b, h, qn, kn * SEL_BLOCK), mask.reshape(b, 1, qn, kn * SEL_BLOCK))
        return jnp.einsum('bhqks,bqksd->bqhd', p.reshape(b, h, qn, kn, SEL_BLOCK).astype(vg.dtype), vg)

    o_sel = map_query_blocks(sel_block, (q, top_i, ok), qpos)
    return o_cmp, o_sel


def window_attn(q, qpos, k, v, kpos):
    d = qpos[:, None] - kpos[None, :]
    mask = (d >= 0) & (d <= WINDOW) & (kpos[None, :] >= 0)
    s = jnp.einsum('bqhd,bkd->bhqk', q, k).astype(jnp.float32) * (q.shape[-1] ** -0.5)
    return jnp.einsum('bhqk,bkd->bqhd', masked_softmax(s, mask).astype(v.dtype), v)


def window_prompt(q, kvw):
    b, t, h, dk = q.shape
    nb = t // QBLOCK
    kvp = jnp.pad(kvw, ((0, 0), (WINDOW, 0), (0, 0), (0, 0)))

    def block(ib):
        start = ib * QBLOCK
        qb = lax.dynamic_slice_in_dim(q, start, QBLOCK, axis=1)
        kvb = lax.dynamic_slice_in_dim(kvp, start, QBLOCK + WINDOW, axis=1)
        qpos = start + jnp.arange(QBLOCK)
        kpos = start - WINDOW + jnp.arange(QBLOCK + WINDOW)
        return window_attn(qb, qpos, kvb[:, :, 0], kvb[:, :, 1], kpos)

    out = lax.map(block, jnp.arange(nb))
    return jnp.moveaxis(out, 0, 1).reshape(b, t, h, dk)


def token_mixer(x, pos, past, lw, layer):
    b, t, _ = x.shape
    offs = np.cumsum(IN_WIDTHS)[:-1].tolist()
    a_q, a_k, a_v, b_q, b_k, b_v, b_g, c_q, c_kv, c_g = jnp.split(x @ lw['w_in'], offs, axis=-1)

    rot_a = DK_A // 4
    qa = rope(a_q.reshape(b, t, H_A, 2, DK_A), pos, rot_a, ROPE_THETA)
    ka = rope(a_k.reshape(b, t, H_A, 2, DK_A), pos, rot_a, ROPE_THETA)
    new_diff = jnp.stack([ka.reshape(b, t, H_A, 2 * DK_A), a_v.reshape(b, t, H_A, DV_A)], axis=2)
    kv_a = new_diff if past is None else jnp.concatenate([past['diff_kv'], new_diff], axis=1)
    lam_init = 0.8 - 0.6 * math.exp(-0.3 * layer)
    lp = lw['diff_lambda'].astype(jnp.float32)
    lam = jnp.exp(jnp.sum(lp[0] * lp[1])) - jnp.exp(jnp.sum(lp[2] * lp[3])) + lam_init
    oa = diff_attention(qa, pos, kv_a, jnp.arange(kv_a.shape[1]), lam)
    oa = head_rms_norm(oa, lw['diff_norm_gain']) * (1.0 - lam_init)
    ya = oa.reshape(b, t, H_A * DV_A) @ lw['w_branch_a']

    qb = rope(b_q.reshape(b, t, H_B, DK_B), pos, DK_B, RET_THETA)
    kb = rope(b_k.reshape(b, t, H_B, DK_B), pos, DK_B, RET_THETA) * (DK_B ** -0.5)
    vb = b_v.reshape(b, t, H_B, DV_B)
    s0 = jnp.zeros((b, H_B, DK_B, DV_B), x.dtype) if past is None else past['ret']
    chunk = RET_CHUNK if t % RET_CHUNK == 0 else t
    ob, s_new = retention(qb, kb, vb, s0, chunk)
    ob = head_group_norm(ob, lw['ret_norm_gain'].reshape(H_B, DV_B)).reshape(b, t, H_B * DV_B) * jax.nn.silu(b_g)
    yb = ob @ lw['w_branch_b']

    rot_c = DK_C // 4
    qc = rope(c_q.reshape(b, t, H_C, DK_C), pos, rot_c, ROPE_THETA)
    c_kv = c_kv.reshape(b, t, 6, DK_C)
    keys = rope(c_kv[:, :, 0::2], pos, rot_c, ROPE_THETA)
    vals = c_kv[:, :, 1::2]
    new_nsa = jnp.stack([keys[:, :, 0], vals[:, :, 0], keys[:, :, 1], vals[:, :, 1]], axis=2)
    nsa_full = new_nsa if past is None else jnp.concatenate([past['nsa_kv'], new_nsa], axis=1)
    o_cmp, o_sel = nsa_compressed_selected(qc, pos, nsa_full[:, :, 0], nsa_full[:, :, 1],
                                           nsa_full[:, :, 2], nsa_full[:, :, 3],
                                           lw['nsa_cmp_pos'], lw['nsa_cmp_w'])
    new_win = jnp.stack([keys[:, :, 2], vals[:, :, 2]], axis=2)
    if past is None:
        o_win = window_prompt(qc, new_win)
        win_state = new_win[:, -min(WINDOW, t):]
    else:
        nbuf = past['win'].shape[1]
        buf = jnp.concatenate([past['win'], new_win], axis=1)
        kpos = pos[0] - nbuf + jnp.arange(nbuf + t)
        o_win = window_attn(qc, pos, buf[:, :, 0], buf[:, :, 1], kpos)
        win_state = buf[:, -nbuf:]
    gc = jax.nn.sigmoid(c_g.reshape(b, t, 3, H_C))[..., None]
    oc = gc[:, :, 0] * o_cmp + gc[:, :, 1] * o_sel + gc[:, :, 2] * o_win
    yc = oc.reshape(b, t, H_C * DK_C) @ lw['w_branch_c']

    g = jax.nn.sigmoid(x @ lw['w_merge_gate']).reshape(b, t, 3, D_MODEL)
    mix = (g[:, :, 0] * ya + g[:, :, 1] * yb + g[:, :, 2] * yc) @ lw['w_out']
    return mix, new_diff, new_nsa, win_state, s_new


def conv_ffn(x, prev, lw):
    a = x @ lw['w_ffn_gate']
    u = x @ lw['w_ffn_up']
    b, t, f = a.shape
    if prev is None:
        prev = jnp.zeros((b, CONV_W - 1, f), a.dtype)
    ap = jnp.concatenate([prev, a], axis=1)
    cw = lw['ffn_conv_w']
    ac = lw['ffn_conv_b'] + sum(ap[:, j:j + t] * cw[j] for j in range(CONV_W))
    h = jax.nn.silu(ac) * u
    return h @ lw['w_ffn_down'], ap[:, t:]


def decoder_layer(x, p, pos, past, lw, layer):
    mix, diff_kv, nsa_kv, win, ret = token_mixer(x, pos, past, lw, layer)
    x = layer_norm(ALPHA * x + mix, lw['ln_gain'][0], lw['ln_bias'][0])
    f, conv = conv_ffn(x, None if past is None else past['conv'], lw)
    x = layer_norm(ALPHA * x + f, lw['ln_gain'][1], lw['ln_bias'][1])
    pe = jax.nn.sigmoid(x @ lw['w_ple_gate']) * (p @ lw['w_ple_proj'])
    x = layer_norm(ALPHA * x + pe, lw['ln_gain'][2], lw['ln_bias'][2])
    return x, (diff_kv, nsa_kv, win, ret, conv)


def setup_inputs(seed: int = 0) -> dict:
    key = jax.random.key(seed)
    ks = jax.random.split(key, 32)
    n_pages = PAST_LEN // PAGE_SIZE
    n_used = DEC_BATCH * n_pages
    n_phys = n_used + (n_used + 3) // 4
    wbuf = min(WINDOW, PAST_LEN)

    def nrm(k, shape, scale=1.0):
        return jax.random.normal(k, shape, jnp.float32) * scale

    page_table = jax.random.permutation(ks[7], n_phys)[:n_used].reshape(DEC_BATCH, n_pages).astype(jnp.int32)
    return {
        'x_prompt': nrm(ks[0], (BATCH, SEQ, D_MODEL)),
        'x_sample': nrm(ks[1], (DEC_BATCH, DEC_SEQ, D_MODEL)),
        'cache_diff_kv': nrm(ks[2], (DEPTH, n_phys, PAGE_SIZE, 2, H_A, DV_A)),
        'cache_nsa_kv': nrm(ks[3], (DEPTH, n_phys, PAGE_SIZE, 4, DK_C)),
        'state_nsa_win': nrm(ks[4], (DEPTH, DEC_BATCH, wbuf, 2, DK_C)),
        'state_ret': nrm(ks[5], (DEPTH, DEC_BATCH, H_B, DK_B, DV_B), 0.5),
        'state_conv': nrm(ks[6], (DEPTH, DEC_BATCH, CONV_W - 1, D_FF)),
        'page_table': page_table,
        'p_prompt': nrm(ks[8], (DEPTH, BATCH, SEQ, PLE_DIM)),
        'p_sample': nrm(ks[9], (DEPTH, DEC_BATCH, DEC_SEQ, PLE_DIM)),
        'ln_gain': 1.0 + nrm(ks[10], (DEPTH, 3, D_MODEL), 0.02),
        'ln_bias': nrm(ks[11], (DEPTH, 3, D_MODEL), 0.02),
        'w_in': nrm(ks[12], (DEPTH, D_MODEL, N_IN), D_MODEL ** -0.5),
        'diff_lambda': nrm(ks[13], (DEPTH, 4, DK_A), 0.1),
        'diff_norm_gain': 1.0 + nrm(ks[14], (DEPTH, DV_A), 0.02),
        'ret_norm_gain': 1.0 + nrm(ks[15], (DEPTH, H_B * DV_B), 0.02),
        'nsa_cmp_pos': nrm(ks[16], (DEPTH, 2, CMP_BLOCK, DK_C), 0.02),
        'nsa_cmp_w': nrm(ks[17], (DEPTH, 2, CMP_BLOCK * DK_C, DK_C), (CMP_BLOCK * DK_C) ** -0.5),
        'w_branch_a': nrm(ks[18], (DEPTH, H_A * DV_A, D_MODEL), (H_A * DV_A) ** -0.5),
        'w_branch_b': nrm(ks[19], (DEPTH, H_B * DV_B, D_MODEL), (H_B * DV_B) ** -0.5),
        'w_branch_c': nrm(ks[20], (DEPTH, H_C * DK_C, D_MODEL), (H_C * DK_C) ** -0.5),
        'w_merge_gate': nrm(ks[21], (DEPTH, D_MODEL, 3 * D_MODEL), D_MODEL ** -0.5),
        'w_out': nrm(ks[22], (DEPTH, D_MODEL, D_MODEL), BETA * D_MODEL ** -0.5),
        'w_ffn_gate': nrm(ks[23], (DEPTH, D_MODEL, D_FF), D_MODEL ** -0.5),
        'w_ffn_up': nrm(ks[24], (DEPTH, D_MODEL, D_FF), D_MODEL ** -0.5),
        'ffn_conv_w': nrm(ks[25], (DEPTH, CONV_W, D_FF), CONV_W ** -0.5),
        'ffn_conv_b': nrm(ks[26], (DEPTH, D_FF), 0.02),
        'w_ffn_down': nrm(ks[27], (DEPTH, D_FF, D_MODEL), BETA * D_FF ** -0.5),
        'w_ple_gate': nrm(ks[28], (DEPTH, D_MODEL, D_MODEL), D_MODEL ** -0.5),
        'w_ple_proj': nrm(ks[29], (DEPTH, PLE_DIM, D_MODEL), BETA * PLE_DIM ** -0.5),
    }


def reference(x_prompt, x_sample, cache_diff_kv, cache_nsa_kv, state_nsa_win, state_ret, state_conv,
              page_table, p_prompt, p_sample, ln_gain, ln_bias, w_in, diff_lambda, diff_norm_gain,
              ret_norm_gain, nsa_cmp_pos, nsa_cmp_w, w_branch_a, w_branch_b, w_branch_c, w_merge_gate,
              w_out, w_ffn_gate, w_ffn_up, ffn_conv_w, ffn_conv_b, w_ffn_down, w_ple_gate, w_ple_proj):
    db, n_pages = page_table.shape
    past_len = n_pages * PAGE_SIZE
    pos_p = jnp.arange(x_prompt.shape[1], dtype=jnp.int32)
    pos_s = past_len + jnp.arange(x_sample.shape[1], dtype=jnp.int32)
    xp, xs = x_prompt, x_sample
    states_p, states_s = [], []
    for i in range(DEPTH):
        lw = {
            'ln_gain': ln_gain[i], 'ln_bias': ln_bias[i], 'w_in': w_in[i],
            'diff_lambda': diff_lambda[i], 'diff_norm_gain': diff_norm_gain[i],
            'ret_norm_gain': ret_norm_gain[i], 'nsa_cmp_pos': nsa_cmp_pos[i], 'nsa_cmp_w': nsa_cmp_w[i],
            'w_branch_a': w_branch_a[i], 'w_branch_b': w_branch_b[i], 'w_branch_c': w_branch_c[i],
            'w_merge_gate': w_merge_gate[i], 'w_out': w_out[i],
            'w_ffn_gate': w_ffn_gate[i], 'w_ffn_up': w_ffn_up[i], 'ffn_conv_w': ffn_conv_w[i],
            'ffn_conv_b': ffn_conv_b[i], 'w_ffn_down': w_ffn_down[i],
            'w_ple_gate': w_ple_gate[i], 'w_ple_proj': w_ple_proj[i],
        }
        past = {
            'diff_kv': cache_diff_kv[i, page_table].reshape(db, past_len, 2, H_A, DV_A),
            'nsa_kv': cache_nsa_kv[i, page_table].reshape(db, past_len, 4, DK_C),
            'win': state_nsa_win[i],
            'ret': state_ret[i],
            'conv': state_conv[i],
        }
        xp, st_p = decoder_layer(xp, p_prompt[i], pos_p, None, lw, i)
        xs, st_s = decoder_layer(xs, p_sample[i], pos_s, past, lw, i)
        states_p.append(st_p)
        states_s.append(st_s)
    diff_kv_p, nsa_kv_p, win_p, ret_p, conv_p = [jnp.stack(s) for s in zip(*states_p)]
    diff_kv_s, nsa_kv_s, win_s, ret_s, conv_s = [jnp.stack(s) for s in zip(*states_s)]
    return (xp, xs, diff_kv_p, diff_kv_s, nsa_kv_p, nsa_kv_s, win_p, win_s, ret_p, ret_s, conv_p, conv_s)
```

```python
import functools
import math

import jax
import jax.numpy as jnp
from jax import lax
from jax.experimental import pallas as pl
from jax.experimental.pallas import tpu as pltpu

F32 = jnp.float32
BF16 = jnp.bfloat16

D_MODEL = 2048
H_A, DK_A, DV_A = 4, 64, 128
H_B, DK_B, DV_B = 4, 128, 256
RET_CHUNK = 128
RET_THETA = 10000.0
H_C, DK_C = 4, 128
CMP_BLOCK, SEL_BLOCK, TOPK, WINDOW = 32, 64, 16, 512
FORCED_SCORE = H_C + 1.0
ROPE_THETA = 500000.0
D_FF = 5504
CONV_W = 3
PLE_DIM = 256
N_LAYERS = 4
ALPHA = (2 * N_LAYERS) ** 0.25
EPS = 1e-5
PAGE = 128

N_IN = 5900
N_IN_PAD = 6144
COL_AQ, COL_AK, COL_AV = 0, 512, 1024
COL_BQ, COL_BK, COL_BV, COL_BG = 1536, 2048, 2560, 3584
COL_CQ, COL_CKV, COL_CWIN, COL_CG = 4608, 5120, 5632, 5888
D_FF_PAD = 5632

VMEM_LIMIT = 56 * 1024 * 1024

NEG = float(jnp.finfo(jnp.float32).min)
TINY = float(jnp.finfo(jnp.float32).tiny)


def _params(*sem):
    return pltpu.CompilerParams(dimension_semantics=sem, vmem_limit_bytes=VMEM_LIMIT)


def _dot(a, b):
    return jnp.dot(a, b, preferred_element_type=F32)


def _dot_nt(a, b):
    return lax.dot_general(a, b, (((1,), (1,)), ((), ())), preferred_element_type=F32)


def _dot_tn(a, b):
    return lax.dot_general(a, b, (((0,), (0,)), ((), ())), preferred_element_type=F32)


def _masked_softmax(s, mask):
    s = jnp.where(mask, s, NEG)
    e = jnp.where(mask, jnp.exp(s - jnp.max(s, axis=-1, keepdims=True)), 0.0)
    return e / jnp.maximum(jnp.sum(e, axis=-1, keepdims=True), TINY)


def _layer_norm(y, g, b):
    mu = jnp.mean(y, axis=-1, keepdims=True)
    d = y - mu
    var = jnp.mean(d * d, axis=-1, keepdims=True)
    return d * lax.rsqrt(var + EPS) * g + b


def _sigmoid(x):
    return 1.0 / (1.0 + jnp.exp(-x))


def _mm_kernel(x_ref, w_ref, o_ref):
    o_ref[...] = _dot(x_ref[...], w_ref[...]).astype(o_ref.dtype)


def _matmul(x, w, out_dtype, tm, tn, name):
    m, k = x.shape
    n = w.shape[1]
    return pl.pallas_call(
        _mm_kernel,
        grid=(n // tn, m // tm),
        in_specs=[pl.BlockSpec((tm, k), lambda j, i: (i, 0)),
                  pl.BlockSpec((k, tn), lambda j, i: (0, j))],
        out_specs=pl.BlockSpec((tm, tn), lambda j, i: (i, j)),
        out_shape=jax.ShapeDtypeStruct((m, n), out_dtype),
        compiler_params=_params("parallel", "parallel"),
        name=name,
    )(x, w)


def _merge_kernel(x_ref, w0_ref, w1_ref, w2_ref, oa_ref, ob_ref, oc_ref, wa_ref, wb_ref, wc_ref, o_ref):
    x = x_ref[...]
    m = _sigmoid(_dot(x, w0_ref[...])) * _dot(oa_ref[...], wa_ref[...])
    m = m + _sigmoid(_dot(x, w1_ref[...])) * _dot(ob_ref[...], wb_ref[...])
    m = m + _sigmoid(_dot(x, w2_ref[...])) * _dot(oc_ref[...], wc_ref[...])
    o_ref[...] = m.astype(o_ref.dtype)


def _merge(xb, w_mg, oa, ob, oc, wa, wb, wc, tm, tn):
    m = xb.shape[0]
    nj = D_MODEL // tn
    row = lambda width: pl.BlockSpec((tm, width), lambda j, i: (i, 0))
    col = lambda k, off: pl.BlockSpec((k, tn), lambda j, i: (0, j + off))
    return pl.pallas_call(
        _merge_kernel,
        grid=(nj, m // tm),
        in_specs=[row(D_MODEL), col(D_MODEL, 0), col(D_MODEL, nj), col(D_MODEL, 2 * nj),
                  row(H_A * DV_A), row(H_B * DV_B), row(H_C * DK_C),
                  col(H_A * DV_A, 0), col(H_B * DV_B, 0), col(H_C * DK_C, 0)],
        out_specs=pl.BlockSpec((tm, tn), lambda j, i: (i, j)),
        out_shape=jax.ShapeDtypeStruct((m, D_MODEL), BF16),
        compiler_params=_params("parallel", "parallel"),
        name="merge_gate",
    )(xb, w_mg, w_mg, w_mg, oa, ob, oc, wa, wb, wc)


def _proj_ln_kernel(m_ref, w_ref, x_ref, g_ref, b_ref, o_ref, ob_ref):
    y = ALPHA * x_ref[...] + _dot(m_ref[...], w_ref[...])
    out = _layer_norm(y, g_ref[...], b_ref[...])
    o_ref[...] = out
    ob_ref[...] = out.astype(BF16)


def _proj_ln(mb, w, x, g, b, tm):
    m = x.shape[0]
    k = mb.shape[1]
    row = lambda width: pl.BlockSpec((tm, width), lambda i: (i, 0))
    full = lambda r, c: pl.BlockSpec((r, c), lambda i: (0, 0))
    return pl.pallas_call(
        _proj_ln_kernel,
        grid=(m // tm,),
        in_specs=[row(k), full(k, D_MODEL), row(D_MODEL), full(1, D_MODEL), full(1, D_MODEL)],
        out_specs=[row(D_MODEL), row(D_MODEL)],
        out_shape=[jax.ShapeDtypeStruct((m, D_MODEL), F32), jax.ShapeDtypeStruct((m, D_MODEL), BF16)],
        compiler_params=_params("parallel"),
        name="out_proj_ln",
    )(mb, w, x, g, b)


def _ple_ln_kernel(xb_ref, wg_ref, p_ref, wp_ref, x_ref, g_ref, b_ref, o_ref, ob_ref):
    pe = _sigmoid(_dot(xb_ref[...], wg_ref[...])) * _dot(p_ref[...].astype(BF16), wp_ref[...])
    out = _layer_norm(ALPHA * x_ref[...] + pe, g_ref[...], b_ref[...])
    o_ref[...] = out
    ob_ref[...] = out.astype(BF16)


def _ple_ln(xb, wg, p, wp, x, g, b, tm):
    m = x.shape[0]
    row = lambda width: pl.BlockSpec((tm, width), lambda i: (i, 0))
    full = lambda r, c: pl.BlockSpec((r, c), lambda i: (0, 0))
    return pl.pallas_call(
        _ple_ln_kernel,
        grid=(m // tm,),
        in_specs=[row(D_MODEL), full(D_MODEL, D_MODEL), row(PLE_DIM), full(PLE_DIM, D_MODEL),
                  row(D_MODEL), full(1, D_MODEL), full(1, D_MODEL)],
        out_specs=[row(D_MODEL), row(D_MODEL)],
        out_shape=[jax.ShapeDtypeStruct((m, D_MODEL), F32), jax.ShapeDtypeStruct((m, D_MODEL), BF16)],
        compiler_params=_params("parallel"),
        name="ple_ln",
    )(xb, wg, p, wp, x, g, b)


def _conv_silu(a, a1, a2, u, cw_ref, cb_ref):
    ac = cb_ref[...] + ((a2 * cw_ref[0:1, :] + a1 * cw_ref[1:2, :]) + a * cw_ref[2:3, :])
    return (ac * _sigmoid(ac)) * u


def _ffn_up_seq_kernel(x_ref, wg_ref, wu_ref, cw_ref, cb_ref, h_ref, st_ref, carry_ref, *, tiles_per_seq):
    i = pl.program_id(1)
    x = x_ref[...]
    a = _dot(x, wg_ref[...])
    u = _dot(x, wu_ref[...])
    tm = a.shape[0]

    @pl.when(i % tiles_per_seq == 0)
    def _():
        carry_ref[...] = jnp.zeros_like(carry_ref)

    prev = carry_ref[...]
    row = lax.broadcasted_iota(jnp.int32, a.shape, 0)
    a1 = jnp.where(row == 0, prev[7:8, :], pltpu.roll(a, 1, 0))
    a2 = jnp.where(row == 0, prev[6:7, :], jnp.where(row == 1, prev[7:8, :], pltpu.roll(a, 2, 0)))
    h_ref[...] = _conv_silu(a, a1, a2, u, cw_ref, cb_ref).astype(h_ref.dtype)
    tail = a[tm - 8:tm, :]
    carry_ref[...] = tail
    st_ref[...] = tail


def _ffn_up_seq(xb, wg, wu, cw, cb, seq_len, tm, tn):
    m = xb.shape[0]
    tps = seq_len // tm
    return pl.pallas_call(
        functools.partial(_ffn_up_seq_kernel, tiles_per_seq=tps),
        grid=(D_FF_PAD // tn, m // tm),
        in_specs=[pl.BlockSpec((tm, D_MODEL), lambda j, i: (i, 0)),
                  pl.BlockSpec((D_MODEL, tn), lambda j, i: (0, j)),
                  pl.BlockSpec((D_MODEL, tn), lambda j, i: (0, j)),
                  pl.BlockSpec((CONV_W, tn), lambda j, i: (0, j)),
                  pl.BlockSpec((1, tn), lambda j, i: (0, j))],
        out_specs=[pl.BlockSpec((tm, tn), lambda j, i: (i, j)),
                   pl.BlockSpec((None, 8, tn), lambda j, i: (i // tps, 0, j))],
        out_shape=[jax.ShapeDtypeStruct((m, D_FF_PAD), BF16),
                   jax.ShapeDtypeStruct((m // seq_len, 8, D_FF_PAD), F32)],
        scratch_shapes=[pltpu.VMEM((8, tn), F32)],
        compiler_params=_params("arbitrary", "arbitrary"),
        name="ffn_up_conv_prompt",
    )(xb, wg, wu, cw, cb)


def _ffn_up_short_kernel(x_ref, wg_ref, wu_ref, cw_ref, cb_ref, p1_ref, p2_ref, h_ref, a_ref, *, seq_len):
    x = x_ref[...]
    a = _dot(x, wg_ref[...])
    u = _dot(x, wu_ref[...])
    t = lax.broadcasted_iota(jnp.int32, a.shape, 0) % seq_len
    a1 = jnp.where(t == 0, p1_ref[...], pltpu.roll(a, 1, 0))
    a2 = jnp.where(t < 2, p2_ref[...], pltpu.roll(a, 2, 0))
    h_ref[...] = _conv_silu(a, a1, a2, u, cw_ref, cb_ref).astype(h_ref.dtype)
    a_ref[...] = a


def _ffn_up_short(xb, wg, wu, cw, cb, p1, p2, seq_len, tn):
    m = xb.shape[0]
    colb = lambda r: pl.BlockSpec((r, tn), lambda j: (0, j))
    return pl.pallas_call(
        functools.partial(_ffn_up_short_kernel, seq_len=seq_len),
        grid=(D_FF_PAD // tn,),
        in_specs=[pl.BlockSpec((m, D_MODEL), lambda j: (0, 0)), colb(D_MODEL), colb(D_MODEL),
                  colb(CONV_W), colb(1), colb(m), colb(m)],
        out_specs=[colb(m), colb(m)],
        out_shape=[jax.ShapeDtypeStruct((m, D_FF_PAD), BF16), jax.ShapeDtypeStruct((m, D_FF_PAD), F32)],
        compiler_params=_params("parallel"),
        name="ffn_up_conv_sample",
    )(xb, wg, wu, cw, cb, p1, p2)


def _ffn_down_ln_kernel(h_ref, w_ref, x_ref, g_ref, b_ref, o_ref, ob_ref, acc_ref):
    k = pl.program_id(1)

    @pl.when(k == 0)
    def _():
        acc_ref[...] = jnp.zeros_like(acc_ref)

    acc_ref[...] += _dot(h_ref[...], w_ref[...])

    @pl.when(k == pl.num_programs(1) - 1)
    def _():
        out = _layer_norm(ALPHA * x_ref[...] + acc_ref[...], g_ref[...], b_ref[...])
        o_ref[...] = out
        ob_ref[...] = out.astype(BF16)


def _ffn_down_ln(h, w, x, g, b, tm, tk):
    m = x.shape[0]
    row = pl.BlockSpec((tm, D_MODEL), lambda i, k: (i, 0))
    vec = pl.BlockSpec((1, D_MODEL), lambda i, k: (0, 0))
    return pl.pallas_call(
        _ffn_down_ln_kernel,
        grid=(m // tm, D_FF_PAD // tk),
        in_specs=[pl.BlockSpec((tm, tk), lambda i, k: (i, k)),
                  pl.BlockSpec((tk, D_MODEL), lambda i, k: (k, 0)), row, vec, vec],
        out_specs=[row, row],
        out_shape=[jax.ShapeDtypeStruct((m, D_MODEL), F32), jax.ShapeDtypeStruct((m, D_MODEL), BF16)],
        scratch_shapes=[pltpu.VMEM((tm, D_MODEL), F32)],
        compiler_params=_params("parallel", "arbitrary"),
        name="ffn_down_ln",
    )(h, w, x, g, b)


def _rope128(x, c, s_up, s_dn, shift):
    y = x * c + pltpu.roll(x, 128 - shift, 1) * s_up
    if s_dn is not None:
        y = y + pltpu.roll(x, shift, 1) * s_dn
    return y


def _rope_split_kernel(aq_ref, ak_ref, av_ref, bq_ref, bk_ref, cq_ref, ckv_ref, cw_ref,
                       ca_ref, sau_ref, sad_ref, cb_ref, sb_ref, cc_ref, scu_ref, scd_ref,
                       qa_ref, nd_ref, kva_ref, qb_ref, kb_ref, qc_ref, nn_ref, sw_ref, nw_ref, kcm_ref, vcm_ref):
    ca, sau, sad = ca_ref[...], sau_ref[...], sad_ref[...]
    cb, sb = cb_ref[...], sb_ref[...]
    cc, scu, scd = cc_ref[...], scu_ref[...], scd_ref[...]
    rot_a = DK_A // 8
    rot_c = DK_C // 8
    for h in range(4):
        sl = slice(h * 128, (h + 1) * 128)
        qa_ref[:, sl] = _rope128(aq_ref[:, sl], ca, sau, sad, rot_a).astype(BF16)
        ka = _rope128(ak_ref[:, sl], ca, sau, sad, rot_a)
        nd_ref[:, sl] = ka
        kva_ref[:, sl] = ka.astype(BF16)
        qb_ref[:, sl] = _rope128(bq_ref[:, sl], cb, sb, None, DK_B // 2).astype(BF16)
        kb_ref[:, sl] = _rope128(bk_ref[:, sl], cb, sb, None, DK_B // 2) * (DK_B ** -0.5)
        qc_ref[:, sl] = _rope128(cq_ref[:, sl], cc, scu, scd, rot_c).astype(BF16)
    av = av_ref[...]
    nd_ref[:, 512:1024] = av
    kva_ref[:, 512:1024] = av.astype(BF16)
    k_cmp = _rope128(ckv_ref[:, 0:128], cc, scu, scd, rot_c)
    v_cmp = ckv_ref[:, 128:256]
    k_sel = _rope128(ckv_ref[:, 256:384], cc, scu, scd, rot_c)
    v_sel = ckv_ref[:, 384:512]
    k_win = _rope128(cw_ref[:, 0:128], cc, scu, scd, rot_c)
    v_win = cw_ref[:, 128:256]
    nn_ref[:, 0:128] = k_cmp
    nn_ref[:, 128:256] = v_cmp
    nn_ref[:, 256:384] = k_sel
    nn_ref[:, 384:512] = v_sel
    sw_ref[:, 0:128] = k_sel.astype(BF16)
    sw_ref[:, 128:256] = v_sel.astype(BF16)
    sw_ref[:, 256:384] = k_win.astype(BF16)
    sw_ref[:, 384:512] = v_win.astype(BF16)
    nw_ref[:, 0:128] = k_win
    nw_ref[:, 128:256] = v_win
    kcm_ref[...] = k_cmp
    vcm_ref[...] = v_cmp


def _rope_split(u, tables, tm):
    m = u.shape[0]
    n_tab = tables[0].shape[0] // tm
    ub = lambda width, blk: pl.BlockSpec((tm, width), lambda i: (i, blk))
    tab = pl.BlockSpec((tm, 128), lambda i: (i % n_tab, 0))
    out = lambda width: pl.BlockSpec((tm, width), lambda i: (i, 0))
    shp = lambda width, dt: jax.ShapeDtypeStruct((m, width), dt)
    return pl.pallas_call(
        _rope_split_kernel,
        grid=(m // tm,),
        in_specs=[ub(512, COL_AQ // 512), ub(512, COL_AK // 512), ub(512, COL_AV // 512),
                  ub(512, COL_BQ // 512), ub(512, COL_BK // 512), ub(512, COL_CQ // 512),
                  ub(512, COL_CKV // 512), ub(256, COL_CWIN // 256)] + [tab] * 8,
        out_specs=[out(512), out(1024), out(1024), out(512), out(512), out(512), out(512), out(512),
                   out(256), out(128), out(128)],
        out_shape=[shp(512, BF16), shp(1024, F32), shp(1024, BF16), shp(512, BF16), shp(512, F32),
                   shp(512, BF16), shp(512, F32), shp(512, BF16), shp(256, F32), shp(128, F32), shp(128, F32)],
        compiler_params=_params("parallel"),
        name="rope_split",
    )(u, u, u, u, u, u, u, u, *tables)


def _rope_tables(pos):
    posf = pos.astype(F32)[:, None]
    lane = jnp.arange(128)

    def cs(half, theta):
        inv_freq = jnp.exp(-math.log(theta) * jnp.arange(half, dtype=F32) / half)
        ang = posf * inv_freq[None, :]
        return jnp.cos(ang), jnp.sin(ang)

    def partial_tables(head_dim, half, theta):
        cos, sin = cs(half, theta)
        d = lane % head_dim
        lo = d < half
        hi = (d >= half) & (d < 2 * half)
        idx = jnp.where(lo, d, jnp.where(hi, d - half, 0))
        c = jnp.where((lo | hi)[None, :], cos[:, idx], 1.0)
        s_up = jnp.where(lo[None, :], -sin[:, idx], 0.0)
        s_dn = jnp.where(hi[None, :], sin[:, idx], 0.0)
        return c, s_up, s_dn

    ca, sau, sad = partial_tables(DK_A, DK_A // 8, ROPE_THETA)
    cc, scu, scd = partial_tables(DK_C, DK_C // 8, ROPE_THETA)
    cosb, sinb = cs(DK_B // 2, RET_THETA)
    cb = jnp.concatenate([cosb, cosb], axis=1)
    sb = jnp.concatenate([-sinb, sinb], axis=1)
    return (ca, sau, sad, cb, sb, cc, scu, scd)


def _rms_head(o, gain, post_scale):
    return o * lax.rsqrt(jnp.mean(o * o, axis=-1, keepdims=True) + EPS) * gain * post_scale


def _diff_attn_prompt_kernel(lam_ref, q_ref, k_ref, v_ref, gain_ref, o_ref, *, tq, post_scale):
    qi = pl.program_id(2)
    q = q_ref[...]
    k = k_ref[...]
    t = k.shape[0]
    lane = lax.broadcasted_iota(jnp.int32, q.shape, 1)
    zero = jnp.zeros_like(q)
    scale = DK_A ** -0.5
    s1 = _dot_nt(jnp.where(lane < DK_A, q, zero), k) * scale
    s2 = _dot_nt(jnp.where(lane >= DK_A, q, zero), k) * scale
    qpos = qi * tq + lax.broadcasted_iota(jnp.int32, (tq, 1), 0)
    kpos = lax.broadcasted_iota(jnp.int32, (1, t), 1)
    mask = kpos <= qpos
    p = _masked_softmax(s1, mask) - lam_ref[0] * _masked_softmax(s2, mask)
    o = _dot(p.astype(BF16), v_ref[...])
    o_ref[...] = _rms_head(o, gain_ref[...], post_scale).astype(o_ref.dtype)


def _diff_attn_prompt(lam, qa, kva, gain, b, t, tq, post_scale):
    m = qa.shape[0]
    nq = t // tq
    return pl.pallas_call(
        functools.partial(_diff_attn_prompt_kernel, tq=tq, post_scale=post_scale),
        grid=(b, H_A, nq),
        in_specs=[pl.BlockSpec(memory_space=pltpu.SMEM),
                  pl.BlockSpec((tq, 128), lambda bi, h, qi: (bi * nq + qi, h)),
                  pl.BlockSpec((t, 128), lambda bi, h, qi: (bi, h)),
                  pl.BlockSpec((t, 128), lambda bi, h, qi: (bi, H_A + h)),
                  pl.BlockSpec((1, 128), lambda bi, h, qi: (0, 0))],
        out_specs=pl.BlockSpec((tq, 128), lambda bi, h, qi: (bi * nq + qi, h)),
        out_shape=jax.ShapeDtypeStruct((m, H_A * DV_A), BF16),
        compiler_params=_params("parallel", "parallel", "parallel"),
        name="diff_attn_prompt",
    )(lam, qa, kva, kva, gain)


def _diff_attn_paged_kernel(pt_ref, lam_ref, q_ref, *rest, n_group, n_new, post_scale):
    page_refs = rest[:n_group]
    kvn_ref, gain_ref, o_ref, m_ref, l_ref, acc_ref = rest[n_group:]
    j = pl.program_id(1)
    scale = DK_A ** -0.5
    q = q_ref[...]
    hk = H_A * 2 * DK_A

    @pl.when(j == 0)
    def _():
        m_ref[...] = jnp.full_like(m_ref, NEG)
        l_ref[...] = jnp.zeros_like(l_ref)
        acc_ref[...] = jnp.zeros_like(acc_ref)

    def update(s, v):
        m_old = m_ref[...]
        m_new = jnp.maximum(m_old, jnp.max(s, axis=-1, keepdims=True))
        a = jnp.exp(m_old - m_new)
        p = jnp.exp(s - m_new[:, 0:1])
        l_ref[...] = a * l_ref[...] + jnp.sum(p, axis=-1, keepdims=True)
        acc_ref[...] = a[:, 0:1] * acc_ref[...] + _dot(p.astype(BF16), v)
        m_ref[...] = m_new

    k = jnp.concatenate([r[:, 0:hk].astype(BF16) for r in page_refs], axis=0)
    v = jnp.concatenate([r[:, hk:2 * hk].astype(BF16) for r in page_refs], axis=0)
    update(_dot_nt(q, k) * scale, v)

    @pl.when(j == pl.num_programs(1) - 1)
    def _():
        pad = jnp.zeros((PAGE - n_new, 2 * hk), F32)
        kvn = jnp.concatenate([kvn_ref[...], pad], axis=0)
        s = _dot_nt(q, kvn[:, 0:hk].astype(BF16)) * scale
        tq = lax.broadcasted_iota(jnp.int32, s.shape, 0) % n_new
        tk = lax.broadcasted_iota(jnp.int32, s.shape, 1)
        update(jnp.where(tk <= tq, s, NEG), kvn[:, hk:2 * hk].astype(BF16))
        o = acc_ref[...] / l_ref[:, 0:1]
        lam = lam_ref[0]
        for h in range(H_A):
            r0 = h * 2 * n_new
            sl = slice(h * DV_A, (h + 1) * DV_A)
            oh = o[r0:r0 + n_new, sl] - lam * o[r0 + n_new:r0 + 2 * n_new, sl]
            o_ref[:, sl] = _rms_head(oh, gain_ref[...], post_scale)


def _diff_attn_paged(page_table, lam, q_exp, cache, layer, kv_new, gain, n_group, post_scale):
    b, n_pages = page_table.shape
    n_new = kv_new.shape[1]
    rows = q_exp.shape[1]
    width = cache.shape[-1]
    pt = page_table.reshape(-1)

    def page_spec(g):
        return pl.BlockSpec((None, None, PAGE, width),
                            lambda bi, j, pt_ref: (layer, pt_ref[bi * n_pages + j * n_group + g], 0, 0))

    grid_spec = pltpu.PrefetchScalarGridSpec(
        num_scalar_prefetch=1,
        grid=(b, n_pages // n_group),
        in_specs=[pl.BlockSpec(memory_space=pltpu.SMEM),
                  pl.BlockSpec((None, rows, q_exp.shape[2]), lambda bi, j, pt_ref: (bi, 0, 0))]
                 + [page_spec(g) for g in range(n_group)]
                 + [pl.BlockSpec((None, n_new, width), lambda bi, j, pt_ref: (bi, 0, 0)),
                    pl.BlockSpec((1, DV_A), lambda bi, j, pt_ref: (0, 0))],
        out_specs=pl.BlockSpec((None, n_new, H_A * DV_A), lambda bi, j, pt_ref: (bi, 0, 0)),
        scratch_shapes=[pltpu.VMEM((rows, 128), F32), pltpu.VMEM((rows, 128), F32),
                        pltpu.VMEM((rows, H_A * DV_A), F32)],
    )
    return pl.pallas_call(
        functools.partial(_diff_attn_paged_kernel, n_group=n_group, n_new=n_new, post_scale=post_scale),
        grid_spec=grid_spec,
        out_shape=jax.ShapeDtypeStruct((b, n_new, H_A * DV_A), F32),
        compiler_params=_params("parallel", "arbitrary"),
        name="diff_attn_paged",
    )(pt, lam, q_exp, *([cache] * n_group), kv_new, gain)


def _retention_kernel(q_ref, k_ref, v_ref, g_ref, s0_ref, dm_ref, qd_ref, kd_ref, cd_ref, gain_ref,
                      o_ref, sf_ref, s_ref):
    c = pl.program_id(2)

    @pl.when(c == 0)
    def _():
        s_ref[...] = s0_ref[...]

    q = q_ref[...]
    k = k_ref[...]
    v = v_ref[...].astype(BF16)
    s = s_ref[...]
    att = _dot_nt(q, k.astype(BF16)) * dm_ref[...]
    o = _dot(att.astype(BF16), v) + _dot(q, s.astype(BF16)) * qd_ref[...]
    s_new = s * cd_ref[...] + _dot_tn((k * kd_ref[...]).astype(BF16), v)
    s_ref[...] = s_new
    sf_ref[...] = s_new
    mu = jnp.mean(o, axis=-1, keepdims=True)
    d = o - mu
    var = jnp.mean(d * d, axis=-1, keepdims=True)
    y = d * lax.rsqrt(var + EPS) * gain_ref[...]
    g = g_ref[...]
    o_ref[...] = (y * (g * _sigmoid(g))).astype(o_ref.dtype)


def _retention(q, k, vg, v_blk, g_blk, s0, tables, gain, b, nc, chunk):
    m = q.shape[0]
    dmask, qdec, kdec, cdec = tables
    row = lambda bi, h, c: bi * nc + c
    return pl.pallas_call(
        _retention_kernel,
        grid=(b, H_B, nc),
        in_specs=[pl.BlockSpec((chunk, DK_B), lambda bi, h, c: (row(bi, h, c), h)),
                  pl.BlockSpec((chunk, DK_B), lambda bi, h, c: (row(bi, h, c), h)),
                  pl.BlockSpec((chunk, DV_B), lambda bi, h, c: (row(bi, h, c), v_blk + h)),
                  pl.BlockSpec((chunk, DV_B), lambda bi, h, c: (row(bi, h, c), g_blk + h)),
                  pl.BlockSpec((None, None, DK_B, DV_B), lambda bi, h, c: (bi, h, 0, 0)),
                  pl.BlockSpec((None, chunk, chunk), lambda bi, h, c: (h, 0, 0)),
                  pl.BlockSpec((None, chunk, DV_B), lambda bi, h, c: (h, 0, 0)),
                  pl.BlockSpec((None, chunk, DK_B), lambda bi, h, c: (h, 0, 0)),
                  pl.BlockSpec((None, 1, DV_B), lambda bi, h, c: (h, 0, 0)),
                  pl.BlockSpec((1, DV_B), lambda bi, h, c: (0, h))],
        out_specs=[pl.BlockSpec((chunk, DV_B), lambda bi, h, c: (row(bi, h, c), h)),
                   pl.BlockSpec((None, None, DK_B, DV_B), lambda bi, h, c: (bi, h, 0, 0))],
        out_shape=[jax.ShapeDtypeStruct((m, H_B * DV_B), BF16),
                   jax.ShapeDtypeStruct((b, H_B, DK_B, DV_B), F32)],
        scratch_shapes=[pltpu.VMEM((DK_B, DV_B), F32)],
        compiler_params=_params("parallel", "parallel", "arbitrary"),
        name="retention",
    )(q, k, vg, vg, s0, dmask, qdec, kdec, cdec, gain)


def _retention_tables(chunk, rows):
    log_g = jnp.log(1.0 - jnp.exp2(-5.0 - jnp.arange(H_B, dtype=F32)))
    idx = jnp.arange(chunk, dtype=F32)
    rel = idx[:, None] - idx[None, :]
    dmask = jnp.where(rel >= 0, jnp.exp(log_g[:, None, None] * jnp.maximum(rel, 0.0)), 0.0)
    q_dec = jnp.exp(log_g[:, None] * (idx[None, :] + 1.0))
    k_dec = jnp.exp(log_g[:, None] * (chunk - 1.0 - idx[None, :]))
    c_dec = jnp.exp(log_g * chunk)
    padn = rows - chunk
    dmask = jnp.pad(dmask, ((0, 0), (0, padn), (0, padn)))
    q_dec = jnp.pad(q_dec, ((0, 0), (0, padn)))
    k_dec = jnp.pad(k_dec, ((0, 0), (0, padn)))
    qd = jnp.broadcast_to(q_dec[:, :, None], (H_B, rows, DV_B))
    kd = jnp.broadcast_to(k_dec[:, :, None], (H_B, rows, DK_B))
    cd = jnp.broadcast_to(c_dec[:, None, None], (H_B, 1, DV_B))
    return dmask, qd, kd, cd


def _compress_kernel(x_ref, pe_ref, w_ref, o_ref):
    o_ref[...] = _dot((x_ref[...] + pe_ref[...]).astype(BF16), w_ref[...])


def _compress(x, pe, w):
    nb = x.shape[1]
    kdim = CMP_BLOCK * DK_C
    return pl.pallas_call(
        _compress_kernel,
        grid=(2,),
        in_specs=[pl.BlockSpec((None, nb, kdim), lambda i: (i, 0, 0)),
                  pl.BlockSpec((None, 1, kdim), lambda i: (i, 0, 0)),
                  pl.BlockSpec((None, kdim, DK_C), lambda i: (i, 0, 0))],
        out_specs=pl.BlockSpec((None, nb, DK_C), lambda i: (i, 0, 0)),
        out_shape=jax.ShapeDtypeStruct((2, nb, DK_C), F32),
        compiler_params=_params("parallel"),
        name="nsa_compress_prompt",
    )(x, pe, w)


def _cmp_attention(qs, qpos, kce, kco, vce, vco):
    scale = DK_C ** -0.5
    nh = kce.shape[0]
    n = lax.broadcasted_iota(jnp.int32, (1, nh), 1)
    me = (2 * CMP_BLOCK * n + (CMP_BLOCK - 1)) <= qpos
    mo = (2 * CMP_BLOCK * n + (2 * CMP_BLOCK - 1)) <= qpos
    se = jnp.where(me, _dot_nt(qs, kce) * scale, NEG)
    so = jnp.where(mo, _dot_nt(qs, kco) * scale, NEG)
    mx = jnp.maximum(jnp.max(se, axis=-1, keepdims=True), jnp.max(so, axis=-1, keepdims=True))
    ee = jnp.where(me, jnp.exp(se - mx), 0.0)
    eo = jnp.where(mo, jnp.exp(so - mx), 0.0)
    den = jnp.maximum(jnp.sum(ee, axis=-1, keepdims=True) + jnp.sum(eo, axis=-1, keepdims=True), TINY)
    pe = ee / den
    po = eo / den
    o = _dot(pe.astype(BF16), vce) + _dot(po.astype(BF16), vco)
    return o, pe, po


def _top_blocks(score, n_pick):
    blk = lax.broadcasted_iota(jnp.int32, score.shape, 1).astype(F32)
    big = 1e9
    work = score
    sel = jnp.zeros(score.shape, F32)
    idxs, vals = [], []
    for _ in range(n_pick):
        mval = jnp.max(work, axis=-1, keepdims=True)
        idx = jnp.min(jnp.where(work == mval, blk, big), axis=-1, keepdims=True)
        pick = blk == idx
        sel = jnp.where(pick, jnp.where(mval >= 0.0, 1.0, 0.0), sel)
        work = jnp.where(pick, -2.0, work)
        idxs.append(idx.astype(jnp.int32))
        vals.append(mval)
    return sel, idxs, vals


def _block_scores(imp, qpos):
    blk = lax.broadcasted_iota(jnp.int32, imp.shape, 1)
    cur = lax.shift_right_arithmetic(qpos, int(math.log2(SEL_BLOCK)))
    forced = jnp.where(blk == 0, 1, 0) + jnp.where(blk == cur, 1, 0) + jnp.where(blk == cur - 1, 1, 0)
    valid = blk * SEL_BLOCK <= qpos
    return jnp.where(valid, jnp.where(forced > 0, FORCED_SCORE, imp), -1.0)


def _nsa_prompt_kernel(q_ref, kce_ref, kco_ref, vce_ref, vco_ref, sw_ref, g_ref, o_ref, *, tq):
    qi = pl.program_id(1)
    scale = DK_C ** -0.5
    q = q_ref[...]
    qs = jnp.concatenate([q[:, h * DK_C:(h + 1) * DK_C] for h in range(H_C)], axis=0)
    qpos1 = qi * tq + lax.broadcasted_iota(jnp.int32, (tq, 1), 0)
    qpos = jnp.concatenate([qpos1] * H_C, axis=0)
    t = sw_ref.shape[0]
    nh = kce_ref.shape[0]

    o_cmp, pe, po = _cmp_attention(qs, qpos, kce_ref[...], kco_ref[...], vce_ref[...], vco_ref[...])
    spe = pe[0:tq] + pe[tq:2 * tq] + pe[2 * tq:3 * tq] + pe[3 * tq:4 * tq]
    spo = po[0:tq] + po[tq:2 * tq] + po[2 * tq:3 * tq] + po[3 * tq:4 * tq]
    score = _block_scores(spe + spo, qpos1)
    sel, _, _ = _top_blocks(score, min(TOPK, nh))
    key_blk = lax.shift_right_arithmetic(lax.broadcasted_iota(jnp.int32, (nh, t), 1), int(math.log2(SEL_BLOCK)))
    expand = jnp.where(key_blk == lax.broadcasted_iota(jnp.int32, (nh, t), 0), 1.0, 0.0).astype(BF16)
    kpos = lax.broadcasted_iota(jnp.int32, (1, t), 1)
    selk1 = jnp.where(kpos <= qpos1, _dot(sel.astype(BF16), expand), 0.0)
    selk = jnp.concatenate([selk1] * H_C, axis=0) > 0.5

    s = _dot_nt(qs, sw_ref[:, 0:128]) * scale
    o_sel = _dot(_masked_softmax(s, selk).astype(BF16), sw_ref[:, 128:256])

    d = qpos - kpos
    wmask = jnp.abs(2 * d - WINDOW) <= WINDOW
    s = _dot_nt(qs, sw_ref[:, 256:384]) * scale
    o_win = _dot(_masked_softmax(s, wmask).astype(BF16), sw_ref[:, 384:512])

    g = _sigmoid(g_ref[...])
    for h in range(H_C):
        r = slice(h * tq, (h + 1) * tq)
        oc = (g[:, h:h + 1] * o_cmp[r] + g[:, H_C + h:H_C + h + 1] * o_sel[r]
              + g[:, 2 * H_C + h:2 * H_C + h + 1] * o_win[r])
        o_ref[:, h * DK_C:(h + 1) * DK_C] = oc.astype(o_ref.dtype)


def _nsa_prompt(qc, kce, kco, vce, vco, sw, u, b, t, tq):
    m = qc.shape[0]
    nq = t // tq
    nh = kce.shape[1]
    cm = pl.BlockSpec((None, nh, DK_C), lambda bi, qi: (bi, 0, 0))
    return pl.pallas_call(
        functools.partial(_nsa_prompt_kernel, tq=tq),
        grid=(b, nq),
        in_specs=[pl.BlockSpec((tq, H_C * DK_C), lambda bi, qi: (bi * nq + qi, 0)), cm, cm, cm, cm,
                  pl.BlockSpec((t, 512), lambda bi, qi: (bi, 0)),
                  pl.BlockSpec((tq, 128), lambda bi, qi: (bi * nq + qi, COL_CG // 128))],
        out_specs=pl.BlockSpec((tq, H_C * DK_C), lambda bi, qi: (bi * nq + qi, 0)),
        out_shape=jax.ShapeDtypeStruct((m, H_C * DK_C), BF16),
        compiler_params=_params("parallel", "parallel"),
        name="nsa_prompt",
    )(qc, kce, kco, vce, vco, sw, u)


def _compress_paged_kernel(pt_ref, *rest, n_group):
    page_refs = rest[:n_group]
    pe_ref, w_ref, o_ref, xk_ref, xv_ref = rest[n_group:]
    j = pl.program_id(1)
    for g, r in enumerate(page_refs):
        rows = pl.ds(pl.multiple_of((j * n_group + g) * PAGE, PAGE), PAGE)
        xk_ref[rows, :] = r[:, 0:DK_C]
        xv_ref[rows, :] = r[:, DK_C:2 * DK_C]

    @pl.when(j == pl.num_programs(1) - 1)
    def _():
        nb = xk_ref.shape[0] // CMP_BLOCK
        acc_k = jnp.zeros((nb, DK_C), F32)
        acc_v = jnp.zeros((nb, DK_C), F32)
        for i in range(CMP_BLOCK):
            xk = xk_ref[pl.ds(i, nb, stride=CMP_BLOCK), :]
            xv = xv_ref[pl.ds(i, nb, stride=CMP_BLOCK), :]
            wsl = slice(i * DK_C, (i + 1) * DK_C)
            acc_k = acc_k + _dot((xk + pe_ref[0, i:i + 1, :]).astype(BF16), w_ref[0, wsl, :])
            acc_v = acc_v + _dot((xv + pe_ref[1, i:i + 1, :]).astype(BF16), w_ref[1, wsl, :])
        o_ref[0] = acc_k
        o_ref[1] = acc_v


def _compress_paged(page_table, cache, layer, pe, w, n_group):
    b, n_pages = page_table.shape
    pt = page_table.reshape(-1)
    nb = n_pages * PAGE // CMP_BLOCK

    def page_spec(g):
        return pl.BlockSpec((None, None, PAGE, 2 * DK_C),
                            lambda bi, j, pt_ref: (layer, pt_ref[bi * n_pages + j * n_group + g], 0, 0))

    grid_spec = pltpu.PrefetchScalarGridSpec(
        num_scalar_prefetch=1,
        grid=(b, n_pages // n_group),
        in_specs=[page_spec(g) for g in range(n_group)]
                 + [pl.BlockSpec((2, CMP_BLOCK, DK_C), lambda bi, j, pt_ref: (0, 0, 0)),
                    pl.BlockSpec((2, CMP_BLOCK * DK_C, DK_C), lambda bi, j, pt_ref: (0, 0, 0))],
        out_specs=pl.BlockSpec((None, 2, nb, DK_C), lambda bi, j, pt_ref: (bi, 0, 0, 0)),
        scratch_shapes=[pltpu.VMEM((n_pages * PAGE, DK_C), F32), pltpu.VMEM((n_pages * PAGE, DK_C), F32)],
    )
    return pl.pallas_call(
        functools.partial(_compress_paged_kernel, n_group=n_group),
        grid_spec=grid_spec,
        out_shape=jax.ShapeDtypeStruct((b, 2, nb, DK_C), F32),
        compiler_params=_params("parallel", "arbitrary"),
        name="nsa_compress_paged",
    )(pt, *([cache] * n_group), pe, w)


def _nsa_sample_select_kernel(q_ref, kce_ref, kco_ref, vce_ref, vco_ref, o_ref, idx_ref, *, n_new, past_len, n_pick):
    qs = q_ref[...]
    rows = qs.shape[0]
    qpos = past_len + lax.broadcasted_iota(jnp.int32, (rows, 1), 0) % n_new
    o_cmp, pe, po = _cmp_attention(qs, qpos, kce_ref[...], kco_ref[...], vce_ref[...], vco_ref[...])
    o_ref[...] = o_cmp
    spe = pe[0:n_new]
    spo = po[0:n_new]
    for h in range(1, H_C):
        spe = spe + pe[h * n_new:(h + 1) * n_new]
        spo = spo + po[h * n_new:(h + 1) * n_new]
    score = _block_scores(spe + spo, qpos[0:n_new])
    _, idxs, _ = _top_blocks(score, n_pick)
    lane = lax.broadcasted_iota(jnp.int32, (n_new, 128), 1)
    out = jnp.zeros((n_new, 128), jnp.int32)
    for r, idx in enumerate(idxs):
        out = jnp.where(lane == r, idx, out)
    idx_ref[...] = out


def _nsa_sample_select(qs, kce, kco, vce, vco, n_new, past_len, n_pick):
    b, rows, _ = qs.shape
    nh = kce.shape[1]
    cm = pl.BlockSpec((None, nh, DK_C), lambda bi: (bi, 0, 0))
    return pl.pallas_call(
        functools.partial(_nsa_sample_select_kernel, n_new=n_new, past_len=past_len, n_pick=n_pick),
        grid=(b,),
        in_specs=[pl.BlockSpec((None, rows, DK_C), lambda bi: (bi, 0, 0)), cm, cm, cm, cm],
        out_specs=[pl.BlockSpec((None, rows, DK_C), lambda bi: (bi, 0, 0)),
                   pl.BlockSpec((None, n_new, 128), lambda bi: (bi, 0, 0))],
        out_shape=[jax.ShapeDtypeStruct((b, rows, DK_C), F32), jax.ShapeDtypeStruct((b, n_new, 128), jnp.int32)],
        compiler_params=_params("parallel"),
        name="nsa_sample_select",
    )(qs, kce, kco, vce, vco)


def _nsa_sample_attend_kernel(pt_ref, ix_ref, q_ref, *rest, n_pick, n_new, past_len):
    blk_refs = rest[:n_pick]
    new_ref, wst_ref, wnew_ref, ocmp_ref, g_ref, o_ref = rest[n_pick:]
    bi = pl.program_id(0)
    qi = pl.program_id(1)
    scale = DK_C ** -0.5
    q = q_ref[...].astype(BF16)
    qpos = past_len + qi
    lane_blk = lax.broadcasted_iota(jnp.int32, (1, SEL_BLOCK), 1)

    ks = [r[:, 0:DK_C].astype(BF16) for r in blk_refs]
    vs = [r[:, DK_C:2 * DK_C].astype(BF16) for r in blk_refs]
    kpos = [ix_ref[(bi * n_new + qi) * TOPK + r] * SEL_BLOCK + lane_blk for r in range(n_pick)]
    new = jnp.concatenate([new_ref[...], jnp.zeros((SEL_BLOCK - n_new, 2 * DK_C), F32)], axis=0)
    ks.append(new[:, 0:DK_C].astype(BF16))
    vs.append(new[:, DK_C:2 * DK_C].astype(BF16))
    kpos.append(past_len + lane_blk)
    s = _dot_nt(q, jnp.concatenate(ks, axis=0)) * scale
    mask = jnp.concatenate(kpos, axis=1) <= qpos
    o_sel = _dot(_masked_softmax(s, mask).astype(BF16), jnp.concatenate(vs, axis=0))

    nbuf = wst_ref.shape[0]
    wnew = jnp.concatenate([wnew_ref[...], jnp.zeros((PAGE - n_new, 2 * DK_C), F32)], axis=0)
    kw = jnp.concatenate([wst_ref[:, 0:DK_C].astype(BF16), wnew[:, 0:DK_C].astype(BF16)], axis=0)
    vw = jnp.concatenate([wst_ref[:, DK_C:2 * DK_C].astype(BF16), wnew[:, DK_C:2 * DK_C].astype(BF16)], axis=0)
    wpos = past_len - nbuf + lax.broadcasted_iota(jnp.int32, (1, nbuf + PAGE), 1)
    d = qpos - wpos
    wmask = jnp.where(wpos >= 0, jnp.abs(2 * d - WINDOW), 4 * WINDOW) <= WINDOW
    s = _dot_nt(q, kw) * scale
    o_win = _dot(_masked_softmax(s, wmask).astype(BF16), vw)

    g = _sigmoid(g_ref[...])
    o_ref[...] = g[0] * ocmp_ref[...] + g[1] * o_sel + g[2] * o_win


def _nsa_sample_attend(page_table, idx, q8, cache, layer, nsa_new, win_state, win_new, ocmp8, gexp,
                       n_pick, past_len):
    b, n_pages = page_table.shape
    n_new = q8.shape[1]
    pt = page_table.reshape(-1)
    nbuf = win_state.shape[1]

    def blk_spec(r):
        def imap(bi, qi, pt_ref, ix_ref):
            blk = ix_ref[(bi * n_new + qi) * TOPK + r]
            return (layer, pt_ref[bi * n_pages + blk // 2], blk % 2, 0, 1)
        return pl.BlockSpec((None, None, None, SEL_BLOCK, 2 * DK_C), imap)

    per_q = lambda: pl.BlockSpec((None, None, 8, DK_C), lambda bi, qi, p, x: (bi, qi, 0, 0))
    grid_spec = pltpu.PrefetchScalarGridSpec(
        num_scalar_prefetch=2,
        grid=(b, n_new),
        in_specs=[per_q()] + [blk_spec(r) for r in range(n_pick)]
                 + [pl.BlockSpec((None, n_new, 2 * DK_C), lambda bi, qi, p, x: (bi, 0, 1)),
                    pl.BlockSpec((None, nbuf, 2 * DK_C), lambda bi, qi, p, x: (bi, 0, 0)),
                    pl.BlockSpec((None, n_new, 2 * DK_C), lambda bi, qi, p, x: (bi, 0, 0)),
                    per_q(),
                    pl.BlockSpec((None, None, 3, 8, DK_C), lambda bi, qi, p, x: (bi, qi, 0, 0, 0))],
        out_specs=per_q(),
    )
    return pl.pallas_call(
        functools.partial(_nsa_sample_attend_kernel, n_pick=n_pick, n_new=n_new, past_len=past_len),
        grid_spec=grid_spec,
        out_shape=jax.ShapeDtypeStruct((b, n_new, 8, DK_C), F32),
        compiler_params=_params("parallel", "parallel"),
        name="nsa_sample_attend",
    )(pt, idx, q8, *([cache] * n_pick), nsa_new, win_state, win_new, ocmp8, gexp)


def _tile(m, pref):
    if m <= pref:
        return m
    t = pref - pref % 128
    while m % t:
        t -= 128
    assert t > 0, (m, pref)
    return t


def _lambda_scalar(lp, layer):
    lam_init = 0.8 - 0.6 * math.exp(-0.3 * layer)
    lam = jnp.exp(jnp.sum(lp[0] * lp[1])) - jnp.exp(jnp.sum(lp[2] * lp[3])) + lam_init
    return lam.reshape(1).astype(F32), 1.0 - lam_init


def _dense_tail(x, xb, oa, ob, oc, p, w, seq_len, conv_prev):
    m = x.shape[0]
    short = seq_len < 128
    mix = _merge(xb, w['w_mg'], oa, ob, oc, w['wa'], w['wb'], w['wc'], _tile(m, 512), 512)
    x1, x1b = _proj_ln(mix, w['w_out'], x, w['g0'], w['b0'], _tile(m, 256))
    if short:
        p1, p2 = conv_prev
        h, a = _ffn_up_short(x1b, w['w_fg'], w['w_fu'], w['cw'], w['cb'], p1, p2, seq_len, 512)
        conv = a.reshape(m // seq_len, seq_len, D_FF_PAD)[:, seq_len - (CONV_W - 1):, :D_FF]
    else:
        h, st = _ffn_up_seq(x1b, w['w_fg'], w['w_fu'], w['cw'], w['cb'], seq_len, _tile(seq_len, 1024), 512)
        conv = st[:, 8 - (CONV_W - 1):, :D_FF]
    x2, x2b = _ffn_down_ln(h, w['w_fd'], x1, w['g1'], w['b1'], _tile(m, 512), 512)
    x3, x3b = _ple_ln(x2b, w['w_pg'], p, w['w_pp'], x2, w['g2'], w['b2'], _tile(m, 256))
    return x3, x3b, conv


def _prompt_layer(x, xb, p, w, layer, b, t, rope_tabs, ret_tabs):
    m = b * t
    u = _matmul(xb, w['w_in'], F32, _tile(m, 1024), 1024, "in_proj")
    (qa, new_diff, kva, qb, kb, qc, new_nsa, sw, new_win, kcm, vcm) = _rope_split(u, rope_tabs, _tile(t, 256))

    lam, post = _lambda_scalar(w['diff_lambda'], layer)
    oa = _diff_attn_prompt(lam, qa, kva, w['diff_gain'], b, t, _tile(t, 256), post)

    chunk = RET_CHUNK
    s0 = jnp.zeros((b, H_B, DK_B, DV_B), F32)
    ob, ret = _retention(qb, kb, u, COL_BV // DV_B, COL_BG // DV_B, s0, ret_tabs, w['ret_gain'], b, t // chunk, chunk)

    nb = m // CMP_BLOCK
    xc = jnp.stack([kcm.reshape(nb, CMP_BLOCK * DK_C), vcm.reshape(nb, CMP_BLOCK * DK_C)])
    kvc = _compress(xc, w['cmp_pe'].reshape(2, 1, CMP_BLOCK * DK_C), w['cmp_w'])
    kvc = kvc.reshape(2, b, t // CMP_BLOCK, DK_C).astype(BF16)
    oc = _nsa_prompt(qc, kvc[0, :, 0::2], kvc[0, :, 1::2], kvc[1, :, 0::2], kvc[1, :, 1::2], sw, u,
                     b, t, _tile(t, 128))

    x3, x3b, conv = _dense_tail(x, xb, oa, ob, oc, p, w, t, None)
    nwin = min(WINDOW, t)
    states = (new_diff.reshape(b, t, 2, H_A, DV_A), new_nsa.reshape(b, t, 4, DK_C),
              new_win.reshape(b, t, 2, DK_C)[:, t - nwin:], ret, conv)
    return x3, x3b, states


def _sample_layer(x, xb, p, w, layer, b, t, past_len, rope_tabs, ret_tabs, page_table,
                  cache_diff, cache_nsa, win_state, ret_state, conv_state):
    m = b * t
    u = _matmul(xb, w['w_in'], F32, m, 1024, "in_proj")
    (qa, new_diff, kva, qb, kb, qc, new_nsa, sw, new_win, kcm, vcm) = _rope_split(u, rope_tabs, m)

    lam, post = _lambda_scalar(w['diff_lambda'], layer)
    q5 = qa.reshape(b, t, H_A, 2, DK_A).transpose(0, 2, 3, 1, 4)
    eye = jnp.eye(H_A * 2, dtype=BF16).reshape(H_A, 2, 1, H_A, 2, 1)
    q_exp = (q5[:, :, :, :, None, None, :] * eye[None]).reshape(b, H_A * 2 * t, H_A * 2 * DK_A)
    n_phys = cache_diff.shape[1]
    oa = _diff_attn_paged(page_table, lam, q_exp, cache_diff.reshape(-1, n_phys, PAGE, 2 * H_A * DV_A),
                          layer, new_diff.reshape(b, t, 2 * H_A * DV_A), w['diff_gain'], 4, post)
    oa = oa.reshape(m, H_A * DV_A).astype(BF16)

    rows = RET_CHUNK
    padr = lambda a: jnp.pad(a.reshape(b, t, a.shape[-1]), ((0, 0), (0, rows - t), (0, 0))).reshape(b * rows, -1)
    ob, ret = _retention(padr(qb), padr(kb), padr(u[:, COL_BV:COL_CQ]), 0, H_B, ret_state, ret_tabs,
                         w['ret_gain'], b, 1, rows)
    ob = ob.reshape(b, rows, H_B * DV_B)[:, :t].reshape(m, H_B * DV_B)

    n_pages = page_table.shape[1]
    kvc = _compress_paged(page_table, cache_nsa.reshape(-1, n_phys, PAGE, 4 * DK_C), layer,
                          w['cmp_pe'], w['cmp_w'], 4).astype(BF16)
    qs = qc.reshape(b, t, H_C, DK_C).transpose(0, 2, 1, 3).reshape(b, H_C * t, DK_C)
    n_pick = TOPK - 1
    assert past_len % SEL_BLOCK == 0 and t <= SEL_BLOCK and past_len // SEL_BLOCK >= n_pick
    ocmp, idx = _nsa_sample_select(qs, kvc[:, 0, 0::2], kvc[:, 0, 1::2], kvc[:, 1, 0::2], kvc[:, 1, 1::2],
                                   t, past_len, n_pick)
    pad8 = lambda a: jnp.pad(a, ((0, 0),) * (a.ndim - 2) + ((0, 8 - H_C), (0, 0)))
    q8 = pad8(qc.astype(F32).reshape(b, t, H_C, DK_C))
    ocmp8 = pad8(ocmp.reshape(b, H_C, t, DK_C).transpose(0, 2, 1, 3))
    cg = u[:, COL_CG:COL_CG + 3 * H_C].reshape(b, t, 3, H_C)
    gexp = jnp.broadcast_to(pad8(cg[..., None]), (b, t, 3, 8, DK_C))
    oc = _nsa_sample_attend(page_table, idx[:, :, :TOPK].reshape(-1), q8,
                            cache_nsa.reshape(-1, n_phys, 2, SEL_BLOCK, 4 * DK_C), layer,
                            new_nsa.reshape(b, t, 4 * DK_C), win_state.reshape(b, -1, 2 * DK_C),
                            new_win.reshape(b, t, 2 * DK_C), ocmp8, gexp, n_pick, past_len)
    oc = oc[:, :, :H_C].reshape(m, H_C * DK_C).astype(BF16)

    cs = jnp.pad(conv_state, ((0, 0), (0, 0), (0, D_FF_PAD - D_FF)))
    zeros = jnp.zeros((b, t - 1, D_FF_PAD), F32)
    p1 = jnp.concatenate([cs[:, 1:2], zeros], axis=1).reshape(m, D_FF_PAD)
    p2 = jnp.concatenate([cs, zeros[:, 1:]], axis=1).reshape(m, D_FF_PAD)

    x3, x3b, conv = _dense_tail(x, xb, oa, ob, oc, p, w, t, (p1, p2))
    nbuf = win_state.shape[1]
    win = jnp.concatenate([win_state, new_win.reshape(b, t, 2, DK_C)], axis=1)[:, t:]
    assert win.shape[1] == nbuf
    states = (new_diff.reshape(b, t, 2, H_A, DV_A), new_nsa.reshape(b, t, 4, DK_C), win, ret, conv)
    return x3, x3b, states


def _layer_weights(i, ln_gain, ln_bias, w_in, diff_lambda, diff_norm_gain, ret_norm_gain, nsa_cmp_pos, nsa_cmp_w,
                   w_branch_a, w_branch_b, w_branch_c, w_merge_gate, w_out, w_ffn_gate, w_ffn_up, ffn_conv_w,
                   ffn_conv_b, w_ffn_down, w_ple_gate, w_ple_proj):
    bf = lambda a: a.astype(BF16)
    padc = lambda a, n: jnp.pad(a, ((0, 0), (0, n - a.shape[1])))
    return {
        'w_in': padc(bf(w_in[i]), N_IN_PAD),
        'diff_lambda': diff_lambda[i].astype(F32),
        'diff_gain': diff_norm_gain[i].reshape(1, DV_A),
        'ret_gain': ret_norm_gain[i].reshape(1, H_B * DV_B),
        'cmp_pe': nsa_cmp_pos[i],
        'cmp_w': bf(nsa_cmp_w[i]),
        'wa': bf(w_branch_a[i]), 'wb': bf(w_branch_b[i]), 'wc': bf(w_branch_c[i]),
        'w_mg': bf(w_merge_gate[i]), 'w_out': bf(w_out[i]),
        'w_fg': padc(bf(w_ffn_gate[i]), D_FF_PAD), 'w_fu': padc(bf(w_ffn_up[i]), D_FF_PAD),
        'cw': padc(ffn_conv_w[i], D_FF_PAD), 'cb': padc(ffn_conv_b[i].reshape(1, D_FF), D_FF_PAD),
        'w_fd': jnp.pad(bf(w_ffn_down[i]), ((0, D_FF_PAD - D_FF), (0, 0))),
        'w_pg': bf(w_ple_gate[i]), 'w_pp': bf(w_ple_proj[i]),
        'g0': ln_gain[i, 0:1], 'g1': ln_gain[i, 1:2], 'g2': ln_gain[i, 2:3],
        'b0': ln_bias[i, 0:1], 'b1': ln_bias[i, 1:2], 'b2': ln_bias[i, 2:3],
    }


def kernel(x_prompt, x_sample, cache_diff_kv, cache_nsa_kv, state_nsa_win, state_ret, state_conv, page_table,
           p_prompt, p_sample, ln_gain, ln_bias, w_in, diff_lambda, diff_norm_gain, ret_norm_gain, nsa_cmp_pos,
           nsa_cmp_w, w_branch_a, w_branch_b, w_branch_c, w_merge_gate, w_out, w_ffn_gate, w_ffn_up, ffn_conv_w,
           ffn_conv_b, w_ffn_down, w_ple_gate, w_ple_proj):
    bp, tp, _ = x_prompt.shape
    bs, ts, _ = x_sample.shape
    n_layers = w_in.shape[0]
    past_len = page_table.shape[1] * PAGE

    rope_p = _rope_tables(jnp.arange(tp, dtype=jnp.int32))
    rope_s = tuple(jnp.tile(tb, (bs, 1)) for tb in _rope_tables(past_len + jnp.arange(ts, dtype=jnp.int32)))
    ret_p = _retention_tables(RET_CHUNK, RET_CHUNK)
    ret_s = _retention_tables(ts, RET_CHUNK)

    xp = x_prompt.reshape(bp * tp, D_MODEL)
    xs = x_sample.reshape(bs * ts, D_MODEL)
    xpb, xsb = xp.astype(BF16), xs.astype(BF16)
    st_p, st_s = [], []
    for i in range(n_layers):
        w = _layer_weights(i, ln_gain, ln_bias, w_in, diff_lambda, diff_norm_gain, ret_norm_gain, nsa_cmp_pos,
                           nsa_cmp_w, w_branch_a, w_branch_b, w_branch_c, w_merge_gate, w_out, w_ffn_gate,
                           w_ffn_up, ffn_conv_w, ffn_conv_b, w_ffn_down, w_ple_gate, w_ple_proj)
        xp, xpb, sp = _prompt_layer(xp, xpb, p_prompt[i].reshape(bp * tp, PLE_DIM), w, i, bp, tp, rope_p, ret_p)
        xs, xsb, ss = _sample_layer(xs, xsb, p_sample[i].reshape(bs * ts, PLE_DIM), w, i, bs, ts, past_len,
                                    rope_s, ret_s, page_table, cache_diff_kv, cache_nsa_kv, state_nsa_win[i],
                                    state_ret[i], state_conv[i])
        st_p.append(sp)
        st_s.append(ss)
    diff_p, nsa_p, win_p, ret_p_out, conv_p = [jnp.stack(s) for s in zip(*st_p)]
    diff_s, nsa_s, win_s, ret_s_out, conv_s = [jnp.stack(s) for s in zip(*st_s)]
    return (xp.reshape(bp, tp, D_MODEL), xs.reshape(bs, ts, D_MODEL), diff_p, diff_s, nsa_p, nsa_s,
            win_p, win_s, ret_p_out, ret_s_out, conv_p, conv_s)
```

```python
import functools
import math

import jax
import jax.numpy as jnp
from jax import lax
from jax.experimental import pallas as pl
from jax.experimental.pallas import tpu as pltpu

F32 = jnp.float32
BF16 = jnp.bfloat16

D_MODEL = 2048
H_A, DK_A, DV_A = 4, 64, 128
H_B, DK_B, DV_B = 4, 128, 256
RET_CHUNK = 128
RET_THETA = 10000.0
H_C, DK_C = 4, 128
CMP_BLOCK, SEL_BLOCK, TOPK, WINDOW = 32, 64, 16, 512
FORCED_SCORE = H_C + 1.0
ROPE_THETA = 500000.0
D_FF = 5504
CONV_W = 3
PLE_DIM = 256
N_LAYERS = 4
ALPHA = (2 * N_LAYERS) ** 0.25
EPS = 1e-5
PAGE = 128

N_IN = 5900
N_IN_PAD = 6144
COL_AQ, COL_AK, COL_AV = 0, 512, 1024
COL_BQ, COL_BK, COL_BV, COL_BG = 1536, 2048, 2560, 3584
COL_CQ, COL_CKV, COL_CWIN, COL_CG = 4608, 5120, 5632, 5888
D_FF_PAD = 5632

VMEM_LIMIT = 56 * 1024 * 1024

NEG = float(jnp.finfo(jnp.float32).min)
TINY = float(jnp.finfo(jnp.float32).tiny)


def _params(*sem):
    return pltpu.CompilerParams(dimension_semantics=sem, vmem_limit_bytes=VMEM_LIMIT)


def _dot(a, b):
    return jnp.dot(a, b, preferred_element_type=F32)


def _dot_nt(a, b):
    return lax.dot_general(a, b, (((1,), (1,)), ((), ())), preferred_element_type=F32)


def _dot_tn(a, b):
    return lax.dot_general(a, b, (((0,), (0,)), ((), ())), preferred_element_type=F32)


def _masked_softmax(s, mask):
    s = jnp.where(mask, s, NEG)
    e = jnp.where(mask, jnp.exp(s - jnp.max(s, axis=-1, keepdims=True)), 0.0)
    return e / jnp.maximum(jnp.sum(e, axis=-1, keepdims=True), TINY)


def _layer_norm(y, g, b):
    mu = jnp.mean(y, axis=-1, keepdims=True)
    d = y - mu
    var = jnp.mean(d * d, axis=-1, keepdims=True)
    return d * lax.rsqrt(var + EPS) * g + b


def _sigmoid(x):
    return 1.0 / (1.0 + jnp.exp(-x))


def _mm_kernel(x_ref, w_ref, o_ref):
    o_ref[...] = _dot(x_ref[...], w_ref[...]).astype(o_ref.dtype)


def _matmul(x, w, layer, out_dtype, tm, tn, name):
    m, k = x.shape
    n = w.shape[2]
    return pl.pallas_call(
        _mm_kernel,
        grid=(n // tn, m // tm),
        in_specs=[pl.BlockSpec((tm, k), lambda j, i: (i, 0)),
                  pl.BlockSpec((None, k, tn), lambda j, i: (layer, 0, j))],
        out_specs=pl.BlockSpec((tm, tn), lambda j, i: (i, j)),
        out_shape=jax.ShapeDtypeStruct((m, n), out_dtype),
        compiler_params=_params("parallel", "parallel"),
        name=name,
    )(x, w)


def _merge_kernel(x_ref, w0_ref, w1_ref, w2_ref, oa_ref, ob_ref, oc_ref, wa_ref, wb_ref, wc_ref, o_ref):
    x = x_ref[...]
    m = _sigmoid(_dot(x, w0_ref[...])) * _dot(oa_ref[...], wa_ref[...])
    m = m + _sigmoid(_dot(x, w1_ref[...])) * _dot(ob_ref[...], wb_ref[...])
    m = m + _sigmoid(_dot(x, w2_ref[...])) * _dot(oc_ref[...], wc_ref[...])
    o_ref[...] = m.astype(o_ref.dtype)


def _merge(xb, w_mg, oa, ob, oc, wa, wb, wc, layer, tm, tn):
    m = xb.shape[0]
    nj = D_MODEL // tn
    row = lambda width: pl.BlockSpec((tm, width), lambda j, i: (i, 0))
    col = lambda k, off: pl.BlockSpec((None, k, tn), lambda j, i: (layer, 0, j + off))
    return pl.pallas_call(
        _merge_kernel,
        grid=(nj, m // tm),
        in_specs=[row(D_MODEL), col(D_MODEL, 0), col(D_MODEL, nj), col(D_MODEL, 2 * nj),
                  row(H_A * DV_A), row(H_B * DV_B), row(H_C * DK_C),
                  col(H_A * DV_A, 0), col(H_B * DV_B, 0), col(H_C * DK_C, 0)],
        out_specs=pl.BlockSpec((tm, tn), lambda j, i: (i, j)),
        out_shape=jax.ShapeDtypeStruct((m, D_MODEL), BF16),
        compiler_params=_params("parallel", "parallel"),
        name="merge_gate",
    )(xb, w_mg, w_mg, w_mg, oa, ob, oc, wa, wb, wc)


def _proj_ln_kernel(m_ref, w_ref, x_ref, g_ref, b_ref, o_ref, ob_ref):
    y = ALPHA * x_ref[...] + _dot(m_ref[...], w_ref[...])
    out = _layer_norm(y, g_ref[...], b_ref[...])
    o_ref[...] = out
    ob_ref[...] = out.astype(BF16)


def _resident_weight(k, n, layer):
    return pl.BlockSpec((None, k, n), lambda i: (layer, 0, 0), pipeline_mode=pl.Buffered(1))


def _proj_ln(mb, w, layer, x, g, b, tm):
    m = x.shape[0]
    k = mb.shape[1]
    row = lambda width: pl.BlockSpec((tm, width), lambda i: (i, 0))
    full = lambda r, c: pl.BlockSpec((r, c), lambda i: (0, 0))
    return pl.pallas_call(
        _proj_ln_kernel,
        grid=(m // tm,),
        in_specs=[row(k), _resident_weight(k, D_MODEL, layer), row(D_MODEL), full(1, D_MODEL), full(1, D_MODEL)],
        out_specs=[row(D_MODEL), row(D_MODEL)],
        out_shape=[jax.ShapeDtypeStruct((m, D_MODEL), F32), jax.ShapeDtypeStruct((m, D_MODEL), BF16)],
        compiler_params=_params("parallel"),
        name="out_proj_ln",
    )(mb, w, x, g, b)


def _ple_ln_kernel(xb_ref, wg_ref, p_ref, wp_ref, x_ref, g_ref, b_ref, o_ref, ob_ref):
    pe = _sigmoid(_dot(xb_ref[...], wg_ref[...])) * _dot(p_ref[...].astype(BF16), wp_ref[...])
    out = _layer_norm(ALPHA * x_ref[...] + pe, g_ref[...], b_ref[...])
    o_ref[...] = out
    ob_ref[...] = out.astype(BF16)


def _ple_ln(xb, wg, p, wp, layer, x, g, b, tm):
    m = x.shape[0]
    row = lambda width: pl.BlockSpec((tm, width), lambda i: (i, 0))
    full = lambda r, c: pl.BlockSpec((r, c), lambda i: (0, 0))
    return pl.pallas_call(
        _ple_ln_kernel,
        grid=(m // tm,),
        in_specs=[row(D_MODEL), _resident_weight(D_MODEL, D_MODEL, layer), row(PLE_DIM),
                  _resident_weight(PLE_DIM, D_MODEL, layer), row(D_MODEL), full(1, D_MODEL), full(1, D_MODEL)],
        out_specs=[row(D_MODEL), row(D_MODEL)],
        out_shape=[jax.ShapeDtypeStruct((m, D_MODEL), F32), jax.ShapeDtypeStruct((m, D_MODEL), BF16)],
        compiler_params=_params("parallel"),
        name="ple_ln",
    )(xb, wg, p, wp, x, g, b)


def _conv_silu(a, a1, a2, u, cw_ref, cb_ref):
    ac = cb_ref[...] + ((a2 * cw_ref[0:1, :] + a1 * cw_ref[1:2, :]) + a * cw_ref[2:3, :])
    return (ac * _sigmoid(ac)) * u


def _ffn_up_seq_kernel(x_ref, wg_ref, wu_ref, cw_ref, cb_ref, h_ref, st_ref, carry_ref, *, tiles_per_seq):
    i = pl.program_id(1)
    x = x_ref[...]
    a = _dot(x, wg_ref[...])
    u = _dot(x, wu_ref[...])
    tm = a.shape[0]

    @pl.when(i % tiles_per_seq == 0)
    def _():
        carry_ref[...] = jnp.zeros_like(carry_ref)

    prev = carry_ref[...]
    row = lax.broadcasted_iota(jnp.int32, a.shape, 0)
    a1 = jnp.where(row == 0, prev[7:8, :], pltpu.roll(a, 1, 0))
    a2 = jnp.where(row == 0, prev[6:7, :], jnp.where(row == 1, prev[7:8, :], pltpu.roll(a, 2, 0)))
    h_ref[...] = _conv_silu(a, a1, a2, u, cw_ref, cb_ref).astype(h_ref.dtype)
    tail = a[tm - 8:tm, :]
    carry_ref[...] = tail
    st_ref[...] = tail


def _ffn_up_seq(xb, wg, wu, layer, cw, cb, seq_len, tm, tn):
    m = xb.shape[0]
    tps = seq_len // tm
    return pl.pallas_call(
        functools.partial(_ffn_up_seq_kernel, tiles_per_seq=tps),
        grid=(D_FF_PAD // tn, m // tm),
        in_specs=[pl.BlockSpec((tm, D_MODEL), lambda j, i: (i, 0)),
                  pl.BlockSpec((None, D_MODEL, tn), lambda j, i: (layer, 0, j)),
                  pl.BlockSpec((None, D_MODEL, tn), lambda j, i: (layer, 0, j)),
                  pl.BlockSpec((CONV_W, tn), lambda j, i: (0, j)),
                  pl.BlockSpec((1, tn), lambda j, i: (0, j))],
        out_specs=[pl.BlockSpec((tm, tn), lambda j, i: (i, j)),
                   pl.BlockSpec((None, 8, tn), lambda j, i: (i // tps, 0, j))],
        out_shape=[jax.ShapeDtypeStruct((m, D_FF_PAD), BF16),
                   jax.ShapeDtypeStruct((m // seq_len, 8, D_FF_PAD), F32)],
        scratch_shapes=[pltpu.VMEM((8, tn), F32)],
        compiler_params=_params("arbitrary", "arbitrary"),
        name="ffn_up_conv_prompt",
    )(xb, wg, wu, cw, cb)


def _ffn_up_short_kernel(x_ref, wg_ref, wu_ref, cw_ref, cb_ref, p1_ref, p2_ref, h_ref, a_ref, *, seq_len):
    x = x_ref[...]
    a = _dot(x, wg_ref[...])
    u = _dot(x, wu_ref[...])
    t = lax.broadcasted_iota(jnp.int32, a.shape, 0) % seq_len
    a1 = jnp.where(t == 0, p1_ref[...], pltpu.roll(a, 1, 0))
    a2 = jnp.where(t < 2, p2_ref[...], pltpu.roll(a, 2, 0))
    h_ref[...] = _conv_silu(a, a1, a2, u, cw_ref, cb_ref).astype(h_ref.dtype)
    a_ref[...] = a


def _ffn_up_short(xb, wg, wu, layer, cw, cb, p1, p2, seq_len, tn):
    m = xb.shape[0]
    colb = lambda r: pl.BlockSpec((r, tn), lambda j: (0, j))
    wcol = pl.BlockSpec((None, D_MODEL, tn), lambda j: (layer, 0, j))
    return pl.pallas_call(
        functools.partial(_ffn_up_short_kernel, seq_len=seq_len),
        grid=(D_FF_PAD // tn,),
        in_specs=[pl.BlockSpec((m, D_MODEL), lambda j: (0, 0)), wcol, wcol,
                  colb(CONV_W), colb(1), colb(m), colb(m)],
        out_specs=[colb(m), colb(m)],
        out_shape=[jax.ShapeDtypeStruct((m, D_FF_PAD), BF16), jax.ShapeDtypeStruct((m, D_FF_PAD), F32)],
        compiler_params=_params("parallel"),
        name="ffn_up_conv_sample",
    )(xb, wg, wu, cw, cb, p1, p2)


def _ffn_down_ln_kernel(h_ref, w_ref, x_ref, g_ref, b_ref, o_ref, ob_ref, acc_ref):
    k = pl.program_id(1)

    @pl.when(k == 0)
    def _():
        acc_ref[...] = jnp.zeros_like(acc_ref)

    acc_ref[...] += _dot(h_ref[...], w_ref[...])

    @pl.when(k == pl.num_programs(1) - 1)
    def _():
        out = _layer_norm(ALPHA * x_ref[...] + acc_ref[...], g_ref[...], b_ref[...])
        o_ref[...] = out
        ob_ref[...] = out.astype(BF16)


def _ffn_down_ln(h, w, layer, x, g, b, tm, tk):
    m = x.shape[0]
    row = pl.BlockSpec((tm, D_MODEL), lambda i, k: (i, 0))
    vec = pl.BlockSpec((1, D_MODEL), lambda i, k: (0, 0))
    return pl.pallas_call(
        _ffn_down_ln_kernel,
        grid=(m // tm, D_FF_PAD // tk),
        in_specs=[pl.BlockSpec((tm, tk), lambda i, k: (i, k)),
                  pl.BlockSpec((None, tk, D_MODEL), lambda i, k: (layer, k, 0)), row, vec, vec],
        out_specs=[row, row],
        out_shape=[jax.ShapeDtypeStruct((m, D_MODEL), F32), jax.ShapeDtypeStruct((m, D_MODEL), BF16)],
        scratch_shapes=[pltpu.VMEM((tm, D_MODEL), F32)],
        compiler_params=_params("parallel", "arbitrary"),
        name="ffn_down_ln",
    )(h, w, x, g, b)


def _rope128(x, c, s_up, s_dn, shift):
    y = x * c + pltpu.roll(x, 128 - shift, 1) * s_up
    if s_dn is not None:
        y = y + pltpu.roll(x, shift, 1) * s_dn
    return y


def _rope_split_kernel(aq_ref, ak_ref, av_ref, bq_ref, bk_ref, cq_ref, ckv_ref, cw_ref,
                       ca_ref, sau_ref, sad_ref, cb_ref, sb_ref, cc_ref, scu_ref, scd_ref,
                       qa_ref, nd_ref, kva_ref, qb_ref, kb_ref, qc_ref, nn_ref, sw_ref, nw_ref, kcm_ref, vcm_ref):
    ca, sau, sad = ca_ref[...], sau_ref[...], sad_ref[...]
    cb, sb = cb_ref[...], sb_ref[...]
    cc, scu, scd = cc_ref[...], scu_ref[...], scd_ref[...]
    rot_a = DK_A // 8
    rot_c = DK_C // 8
    for h in range(4):
        sl = slice(h * 128, (h + 1) * 128)
        qa_ref[:, sl] = _rope128(aq_ref[:, sl], ca, sau, sad, rot_a).astype(BF16)
        ka = _rope128(ak_ref[:, sl], ca, sau, sad, rot_a)
        nd_ref[:, sl] = ka
        kva_ref[:, sl] = ka.astype(BF16)
        qb_ref[:, sl] = _rope128(bq_ref[:, sl], cb, sb, None, DK_B // 2).astype(BF16)
        kb_ref[:, sl] = _rope128(bk_ref[:, sl], cb, sb, None, DK_B // 2) * (DK_B ** -0.5)
        qc_ref[:, sl] = _rope128(cq_ref[:, sl], cc, scu, scd, rot_c).astype(BF16)
    av = av_ref[...]
    nd_ref[:, 512:1024] = av
    kva_ref[:, 512:1024] = av.astype(BF16)
    k_cmp = _rope128(ckv_ref[:, 0:128], cc, scu, scd, rot_c)
    v_cmp = ckv_ref[:, 128:256]
    k_sel = _rope128(ckv_ref[:, 256:384], cc, scu, scd, rot_c)
    v_sel = ckv_ref[:, 384:512]
    k_win = _rope128(cw_ref[:, 0:128], cc, scu, scd, rot_c)
    v_win = cw_ref[:, 128:256]
    nn_ref[:, 0:128] = k_cmp
    nn_ref[:, 128:256] = v_cmp
    nn_ref[:, 256:384] = k_sel
    nn_ref[:, 384:512] = v_sel
    sw_ref[:, 0:128] = k_sel.astype(BF16)
    sw_ref[:, 128:256] = v_sel.astype(BF16)
    sw_ref[:, 256:384] = k_win.astype(BF16)
    sw_ref[:, 384:512] = v_win.astype(BF16)
    nw_ref[:, 0:128] = k_win
    nw_ref[:, 128:256] = v_win
    kcm_ref[...] = k_cmp
    vcm_ref[...] = v_cmp


def _rope_split(u, tables, tm):
    m = u.shape[0]
    n_tab = tables[0].shape[0] // tm
    ub = lambda width, blk: pl.BlockSpec((tm, width), lambda i: (i, blk))
    tab = pl.BlockSpec((tm, 128), lambda i: (i % n_tab, 0))
    out = lambda width: pl.BlockSpec((tm, width), lambda i: (i, 0))
    shp = lambda width, dt: jax.ShapeDtypeStruct((m, width), dt)
    return pl.pallas_call(
        _rope_split_kernel,
        grid=(m // tm,),
        in_specs=[ub(512, COL_AQ // 512), ub(512, COL_AK // 512), ub(512, COL_AV // 512),
                  ub(512, COL_BQ // 512), ub(512, COL_BK // 512), ub(512, COL_CQ // 512),
                  ub(512, COL_CKV // 512), ub(256, COL_CWIN // 256)] + [tab] * 8,
        out_specs=[out(512), out(1024), out(1024), out(512), out(512), out(512), out(512), out(512),
                   out(256), out(128), out(128)],
        out_shape=[shp(512, BF16), shp(1024, F32), shp(1024, BF16), shp(512, BF16), shp(512, F32),
                   shp(512, BF16), shp(512, F32), shp(512, BF16), shp(256, F32), shp(128, F32), shp(128, F32)],
        compiler_params=_params("parallel"),
        name="rope_split",
    )(u, u, u, u, u, u, u, u, *tables)


def _rope_tables(pos):
    posf = pos.astype(F32)[:, None]
    lane = jnp.arange(128)

    def cs(half, theta):
        inv_freq = jnp.exp(-math.log(theta) * jnp.arange(half, dtype=F32) / half)
        ang = posf * inv_freq[None, :]
        return jnp.cos(ang), jnp.sin(ang)

    def partial_tables(head_dim, half, theta):
        cos, sin = cs(half, theta)
        d = lane % head_dim
        lo = d < half
        hi = (d >= half) & (d < 2 * half)
        idx = jnp.where(lo, d, jnp.where(hi, d - half, 0))
        c = jnp.where((lo | hi)[None, :], cos[:, idx], 1.0)
        s_up = jnp.where(lo[None, :], -sin[:, idx], 0.0)
        s_dn = jnp.where(hi[None, :], sin[:, idx], 0.0)
        return c, s_up, s_dn

    ca, sau, sad = partial_tables(DK_A, DK_A // 8, ROPE_THETA)
    cc, scu, scd = partial_tables(DK_C, DK_C // 8, ROPE_THETA)
    cosb, sinb = cs(DK_B // 2, RET_THETA)
    cb = jnp.concatenate([cosb, cosb], axis=1)
    sb = jnp.concatenate([-sinb, sinb], axis=1)
    return (ca, sau, sad, cb, sb, cc, scu, scd)


def _rms_head(o, gain, post_scale):
    return o * lax.rsqrt(jnp.mean(o * o, axis=-1, keepdims=True) + EPS) * gain * post_scale


def _diff_attn_prompt_kernel(lam_ref, q_ref, k_ref, v_ref, gain_ref, o_ref, *, tq, post_scale, n_bucket):
    qi = pl.program_id(2)
    q = q_ref[...]
    t = k_ref.shape[0]
    lane = lax.broadcasted_iota(jnp.int32, q.shape, 1)
    zero = jnp.zeros_like(q)
    scale = DK_A ** -0.5
    q1 = jnp.where(lane < DK_A, q, zero)
    q2 = jnp.where(lane >= DK_A, q, zero)
    qpos = qi * tq + lax.broadcasted_iota(jnp.int32, (tq, 1), 0)
    kb = t // n_bucket
    bucket = lax.div((qi + 1) * tq - 1, kb)

    for nb in range(n_bucket):
        @pl.when(bucket == nb)
        def _(nkeys=(nb + 1) * kb):
            k = k_ref[0:nkeys, :]
            mask = lax.broadcasted_iota(jnp.int32, (1, nkeys), 1) <= qpos
            p = (_masked_softmax(_dot_nt(q1, k) * scale, mask)
                 - lam_ref[0] * _masked_softmax(_dot_nt(q2, k) * scale, mask))
            o = _dot(p.astype(BF16), v_ref[0:nkeys, :])
            o_ref[...] = _rms_head(o, gain_ref[...], post_scale).astype(o_ref.dtype)


def _key_buckets(t, tq):
    n_bucket = 4
    return n_bucket if t % (n_bucket * 128) == 0 and (t // n_bucket) % tq == 0 else 1


def _diff_attn_prompt(lam, qa, kva, gain, b, t, tq, post_scale):
    m = qa.shape[0]
    nq = t // tq
    return pl.pallas_call(
        functools.partial(_diff_attn_prompt_kernel, tq=tq, post_scale=post_scale, n_bucket=_key_buckets(t, tq)),
        grid=(b, H_A, nq),
        in_specs=[pl.BlockSpec(memory_space=pltpu.SMEM),
                  pl.BlockSpec((tq, 128), lambda bi, h, qi: (bi * nq + qi, h)),
                  pl.BlockSpec((t, 128), lambda bi, h, qi: (bi, h)),
                  pl.BlockSpec((t, 128), lambda bi, h, qi: (bi, H_A + h)),
                  pl.BlockSpec((1, 128), lambda bi, h, qi: (0, 0))],
        out_specs=pl.BlockSpec((tq, 128), lambda bi, h, qi: (bi * nq + qi, h)),
        out_shape=jax.ShapeDtypeStruct((m, H_A * DV_A), BF16),
        compiler_params=_params("parallel", "parallel", "parallel"),
        name="diff_attn_prompt",
    )(lam, qa, kva, kva, gain)


def _diff_attn_paged_kernel(pt_ref, lam_ref, q_ref, *rest, n_group, n_new, post_scale):
    page_refs = rest[:n_group]
    kvn_ref, gain_ref, o_ref, m_ref, l_ref, acc_ref = rest[n_group:]
    j = pl.program_id(1)
    scale = DK_A ** -0.5
    q = q_ref[...]
    hk = H_A * 2 * DK_A

    @pl.when(j == 0)
    def _():
        m_ref[...] = jnp.full_like(m_ref, NEG)
        l_ref[...] = jnp.zeros_like(l_ref)
        acc_ref[...] = jnp.zeros_like(acc_ref)

    def update(s, v):
        m_old = m_ref[...]
        m_new = jnp.maximum(m_old, jnp.max(s, axis=-1, keepdims=True))
        a = jnp.exp(m_old - m_new)
        p = jnp.exp(s - m_new[:, 0:1])
        l_ref[...] = a * l_ref[...] + jnp.sum(p, axis=-1, keepdims=True)
        acc_ref[...] = a[:, 0:1] * acc_ref[...] + _dot(p.astype(BF16), v)
        m_ref[...] = m_new

    def heads(r, first):
        return jnp.concatenate([r[pl.ds(first + h, PAGE, stride=2 * H_A), :].astype(BF16) for h in range(H_A)],
                               axis=1)

    k = jnp.concatenate([heads(r, 0) for r in page_refs], axis=0)
    v = jnp.concatenate([heads(r, H_A) for r in page_refs], axis=0)
    update(_dot_nt(q, k) * scale, v)

    @pl.when(j == pl.num_programs(1) - 1)
    def _():
        pad = jnp.zeros((PAGE - n_new, 2 * hk), F32)
        kvn = jnp.concatenate([kvn_ref[...], pad], axis=0)
        s = _dot_nt(q, kvn[:, 0:hk].astype(BF16)) * scale
        tq = lax.broadcasted_iota(jnp.int32, s.shape, 0) % n_new
        tk = lax.broadcasted_iota(jnp.int32, s.shape, 1)
        update(jnp.where(tk <= tq, s, NEG), kvn[:, hk:2 * hk].astype(BF16))
        o = acc_ref[...] / l_ref[:, 0:1]
        lam = lam_ref[0]
        for h in range(H_A):
            r0 = h * 2 * n_new
            sl = slice(h * DV_A, (h + 1) * DV_A)
            oh = o[r0:r0 + n_new, sl] - lam * o[r0 + n_new:r0 + 2 * n_new, sl]
            o_ref[:, sl] = _rms_head(oh, gain_ref[...], post_scale)


def _diff_attn_paged(page_table, lam, q_exp, cache, layer, kv_new, gain, n_group, post_scale):
    b, n_pages = page_table.shape
    n_new = kv_new.shape[1]
    rows = q_exp.shape[1]
    width = kv_new.shape[2]
    pt = page_table.reshape(-1)

    def page_spec(g):
        return pl.BlockSpec((None, None, PAGE * 2 * H_A, DV_A),
                            lambda bi, j, pt_ref: (layer, pt_ref[bi * n_pages + j * n_group + g], 0, 0))

    grid_spec = pltpu.PrefetchScalarGridSpec(
        num_scalar_prefetch=1,
        grid=(b, n_pages // n_group),
        in_specs=[pl.BlockSpec(memory_space=pltpu.SMEM),
                  pl.BlockSpec((None, rows, q_exp.shape[2]), lambda bi, j, pt_ref: (bi, 0, 0))]
                 + [page_spec(g) for g in range(n_group)]
                 + [pl.BlockSpec((None, n_new, width), lambda bi, j, pt_ref: (bi, 0, 0)),
                    pl.BlockSpec((1, DV_A), lambda bi, j, pt_ref: (0, 0))],
        out_specs=pl.BlockSpec((None, n_new, H_A * DV_A), lambda bi, j, pt_ref: (bi, 0, 0)),
        scratch_shapes=[pltpu.VMEM((rows, 128), F32), pltpu.VMEM((rows, 128), F32),
                        pltpu.VMEM((rows, H_A * DV_A), F32)],
    )
    return pl.pallas_call(
        functools.partial(_diff_attn_paged_kernel, n_group=n_group, n_new=n_new, post_scale=post_scale),
        grid_spec=grid_spec,
        out_shape=jax.ShapeDtypeStruct((b, n_new, H_A * DV_A), F32),
        compiler_params=_params("parallel", "arbitrary"),
        name="diff_attn_paged",
    )(pt, lam, q_exp, *([cache] * n_group), kv_new, gain)


def _retention_kernel(q_ref, k_ref, v0_ref, v1_ref, g0_ref, g1_ref, s0_ref, dm_ref, qd_ref, kd_ref, cd_ref,
                      gain_ref, o_ref, sf_ref, s_ref):
    c = pl.program_id(1)

    @pl.when(c == 0)
    def _():
        s_ref[...] = s0_ref[...]

    for h in range(H_B):
        ks = slice(h * DK_B, (h + 1) * DK_B)
        half = slice((h % 2) * DV_B, (h % 2 + 1) * DV_B)
        v_ref, g_ref = (v0_ref, g0_ref) if h < 2 else (v1_ref, g1_ref)
        q = q_ref[:, ks]
        k = k_ref[:, ks]
        v = v_ref[:, half].astype(BF16)
        s = s_ref[h]
        att = _dot_nt(q, k.astype(BF16)) * dm_ref[h]
        o = _dot(att.astype(BF16), v) + _dot(q, s.astype(BF16)) * qd_ref[h]
        s_new = s * cd_ref[h] + _dot_tn((k * kd_ref[h]).astype(BF16), v)
        s_ref[h] = s_new
        sf_ref[h] = s_new
        mu = jnp.mean(o, axis=-1, keepdims=True)
        d = o - mu
        var = jnp.mean(d * d, axis=-1, keepdims=True)
        y = d * lax.rsqrt(var + EPS) * gain_ref[:, h * DV_B:(h + 1) * DV_B]
        g = g_ref[:, half]
        o_ref[:, h * DV_B:(h + 1) * DV_B] = (y * (g * _sigmoid(g))).astype(o_ref.dtype)


def _retention(q, k, vg, v_blk, g_blk, s0, tables, gain, b, nc, chunk):
    m = q.shape[0]
    dmask, qdec, kdec, cdec = tables
    row = lambda width, blk: pl.BlockSpec((chunk, width), lambda bi, c: (bi * nc + c, blk))
    full = lambda a: pl.BlockSpec(a.shape, lambda bi, c: (0,) * a.ndim)
    state = pl.BlockSpec((None, H_B, DK_B, DV_B), lambda bi, c: (bi, 0, 0, 0))
    return pl.pallas_call(
        _retention_kernel,
        grid=(b, nc),
        in_specs=[row(H_B * DK_B, 0), row(H_B * DK_B, 0),
                  row(2 * DV_B, v_blk), row(2 * DV_B, v_blk + 1), row(2 * DV_B, g_blk), row(2 * DV_B, g_blk + 1),
                  state, full(dmask), full(qdec), full(kdec), full(cdec), full(gain)],
        out_specs=[row(H_B * DV_B, 0), state],
        out_shape=[jax.ShapeDtypeStruct((m, H_B * DV_B), BF16),
                   jax.ShapeDtypeStruct((b, H_B, DK_B, DV_B), F32)],
        scratch_shapes=[pltpu.VMEM((H_B, DK_B, DV_B), F32)],
        compiler_params=_params("parallel", "arbitrary"),
        name="retention",
    )(q, k, vg, vg, vg, vg, s0, dmask, qdec, kdec, cdec, gain)


def _retention_tables(chunk, rows):
    log_g = jnp.log(1.0 - jnp.exp2(-5.0 - jnp.arange(H_B, dtype=F32)))
    idx = jnp.arange(chunk, dtype=F32)
    rel = idx[:, None] - idx[None, :]
    dmask = jnp.where(rel >= 0, jnp.exp(log_g[:, None, None] * jnp.maximum(rel, 0.0)), 0.0)
    q_dec = jnp.exp(log_g[:, None] * (idx[None, :] + 1.0))
    k_dec = jnp.exp(log_g[:, None] * (chunk - 1.0 - idx[None, :]))
    c_dec = jnp.exp(log_g * chunk)
    padn = rows - chunk
    dmask = jnp.pad(dmask, ((0, 0), (0, padn), (0, padn)))
    q_dec = jnp.pad(q_dec, ((0, 0), (0, padn)))
    k_dec = jnp.pad(k_dec, ((0, 0), (0, padn)))
    qd = jnp.broadcast_to(q_dec[:, :, None], (H_B, rows, DV_B))
    kd = jnp.broadcast_to(k_dec[:, :, None], (H_B, rows, DK_B))
    cd = jnp.broadcast_to(c_dec[:, None, None], (H_B, 1, DV_B))
    return dmask, qd, kd, cd


def _compress_kernel(x_ref, pe_ref, w_ref, o_ref):
    o_ref[...] = _dot((x_ref[...] + pe_ref[...]).astype(BF16), w_ref[...])


def _compress(x, pe, w, layer):
    nb = x.shape[1]
    kdim = CMP_BLOCK * DK_C
    return pl.pallas_call(
        _compress_kernel,
        grid=(2,),
        in_specs=[pl.BlockSpec((None, nb, kdim), lambda i: (i, 0, 0)),
                  pl.BlockSpec((None, 1, kdim), lambda i: (i, 0, 0)),
                  pl.BlockSpec((None, None, kdim, DK_C), lambda i: (layer, i, 0, 0))],
        out_specs=pl.BlockSpec((None, nb, DK_C), lambda i: (i, 0, 0)),
        out_shape=jax.ShapeDtypeStruct((2, nb, DK_C), F32),
        compiler_params=_params("parallel"),
        name="nsa_compress_prompt",
    )(x, pe, w)


def _cmp_attention(qs, qpos, kce, kco, vce, vco):
    scale = DK_C ** -0.5
    nh = kce.shape[0]
    n = lax.broadcasted_iota(jnp.int32, (1, nh), 1)
    me = (2 * CMP_BLOCK * n + (CMP_BLOCK - 1)) <= qpos
    mo = (2 * CMP_BLOCK * n + (2 * CMP_BLOCK - 1)) <= qpos
    se = jnp.where(me, _dot_nt(qs, kce) * scale, NEG)
    so = jnp.where(mo, _dot_nt(qs, kco) * scale, NEG)
    mx = jnp.maximum(jnp.max(se, axis=-1, keepdims=True), jnp.max(so, axis=-1, keepdims=True))
    ee = jnp.where(me, jnp.exp(se - mx), 0.0)
    eo = jnp.where(mo, jnp.exp(so - mx), 0.0)
    den = jnp.maximum(jnp.sum(ee, axis=-1, keepdims=True) + jnp.sum(eo, axis=-1, keepdims=True), TINY)
    pe = ee / den
    po = eo / den
    o = _dot(pe.astype(BF16), vce) + _dot(po.astype(BF16), vco)
    return o, pe, po


def _top_blocks(score, n_pick):
    blk = lax.broadcasted_iota(jnp.int32, score.shape, 1).astype(F32)
    big = 1e9
    work = score
    sel = jnp.zeros(score.shape, F32)
    idxs, vals = [], []
    for _ in range(n_pick):
        mval = jnp.max(work, axis=-1, keepdims=True)
        idx = jnp.min(jnp.where(work == mval, blk, big), axis=-1, keepdims=True)
        pick = blk == idx
        sel = jnp.where(pick, jnp.where(mval >= 0.0, 1.0, 0.0), sel)
        work = jnp.where(pick, -2.0, work)
        idxs.append(idx.astype(jnp.int32))
        vals.append(mval)
    return sel, idxs, vals


def _block_scores(imp, qpos):
    blk = lax.broadcasted_iota(jnp.int32, imp.shape, 1)
    cur = lax.shift_right_arithmetic(qpos, int(math.log2(SEL_BLOCK)))
    forced = jnp.where(blk == 0, 1, 0) + jnp.where(blk == cur, 1, 0) + jnp.where(blk == cur - 1, 1, 0)
    valid = blk * SEL_BLOCK <= qpos
    return jnp.where(valid, jnp.where(forced > 0, FORCED_SCORE, imp), -1.0)


def _nsa_prompt_kernel(q_ref, kce_ref, kco_ref, vce_ref, vco_ref, sw_ref, g_ref, o_ref, osel_ref, *, tq, n_bucket):
    qi = pl.program_id(1)
    scale = DK_C ** -0.5
    q = q_ref[...]
    qs = jnp.concatenate([q[:, h * DK_C:(h + 1) * DK_C] for h in range(H_C)], axis=0)
    qpos1 = qi * tq + lax.broadcasted_iota(jnp.int32, (tq, 1), 0)
    qpos = jnp.concatenate([qpos1] * H_C, axis=0)
    t = sw_ref.shape[0]
    nh = kce_ref.shape[0]

    o_cmp, pe, po = _cmp_attention(qs, qpos, kce_ref[...], kco_ref[...], vce_ref[...], vco_ref[...])
    spe = pe[0:tq] + pe[tq:2 * tq] + pe[2 * tq:3 * tq] + pe[3 * tq:4 * tq]
    spo = po[0:tq] + po[tq:2 * tq] + po[2 * tq:3 * tq] + po[3 * tq:4 * tq]
    score = _block_scores(spe + spo, qpos1)
    sel, _, _ = _top_blocks(score, min(TOPK, nh))
    selb = sel.astype(BF16)

    kb = t // n_bucket
    bucket = lax.div((qi + 1) * tq - 1, kb)
    for nb in range(n_bucket):
        @pl.when(bucket == nb)
        def _(nkeys=(nb + 1) * kb):
            key_blk = lax.shift_right_arithmetic(lax.broadcasted_iota(jnp.int32, (nh, nkeys), 1),
                                                 int(math.log2(SEL_BLOCK)))
            expand = jnp.where(key_blk == lax.broadcasted_iota(jnp.int32, (nh, nkeys), 0), 1.0, 0.0).astype(BF16)
            kpos = lax.broadcasted_iota(jnp.int32, (1, nkeys), 1)
            selk1 = jnp.where(kpos <= qpos1, _dot(selb, expand), 0.0)
            selk = jnp.concatenate([selk1] * H_C, axis=0) > 0.5
            s = _dot_nt(qs, sw_ref[0:nkeys, 0:128]) * scale
            osel_ref[...] = _dot(_masked_softmax(s, selk).astype(BF16), sw_ref[0:nkeys, 128:256])
    o_sel = osel_ref[...]

    wlen = min(t, WINDOW + tq)
    start = pl.multiple_of(jnp.clip(qi * tq - WINDOW, 0, t - wlen), 128) if wlen < t else 0
    wpos = start + lax.broadcasted_iota(jnp.int32, (1, wlen), 1)
    d = qpos - wpos
    wmask = jnp.abs(2 * d - WINDOW) <= WINDOW
    s = _dot_nt(qs, sw_ref[pl.ds(start, wlen), 256:384]) * scale
    o_win = _dot(_masked_softmax(s, wmask).astype(BF16), sw_ref[pl.ds(start, wlen), 384:512])

    g = _sigmoid(g_ref[...])
    for h in range(H_C):
        r = slice(h * tq, (h + 1) * tq)
        oc = (g[:, h:h + 1] * o_cmp[r] + g[:, H_C + h:H_C + h + 1] * o_sel[r]
              + g[:, 2 * H_C + h:2 * H_C + h + 1] * o_win[r])
        o_ref[:, h * DK_C:(h + 1) * DK_C] = oc.astype(o_ref.dtype)


def _nsa_prompt(qc, kce, kco, vce, vco, sw, u, b, t, tq):
    m = qc.shape[0]
    nq = t // tq
    nh = kce.shape[1]
    cm = pl.BlockSpec((None, nh, DK_C), lambda bi, qi: (bi, 0, 0))
    return pl.pallas_call(
        functools.partial(_nsa_prompt_kernel, tq=tq, n_bucket=_key_buckets(t, tq)),
        grid=(b, nq),
        scratch_shapes=[pltpu.VMEM((H_C * tq, DK_C), F32)],
        in_specs=[pl.BlockSpec((tq, H_C * DK_C), lambda bi, qi: (bi * nq + qi, 0)), cm, cm, cm, cm,
                  pl.BlockSpec((t, 512), lambda bi, qi: (bi, 0)),
                  pl.BlockSpec((tq, 128), lambda bi, qi: (bi * nq + qi, COL_CG // 128))],
        out_specs=pl.BlockSpec((tq, H_C * DK_C), lambda bi, qi: (bi * nq + qi, 0)),
        out_shape=jax.ShapeDtypeStruct((m, H_C * DK_C), BF16),
        compiler_params=_params("parallel", "parallel"),
        name="nsa_prompt",
    )(qc, kce, kco, vce, vco, sw, u)


def _compress_paged_kernel(pt_ref, *rest, n_group):
    page_refs = rest[:n_group]
    pe_ref, w_ref, o_ref, xk_ref, xv_ref = rest[n_group:]
    j = pl.program_id(1)
    for g, r in enumerate(page_refs):
        rows = pl.ds(pl.multiple_of((j * n_group + g) * PAGE, PAGE), PAGE)
        xk_ref[rows, :] = r[pl.ds(0, PAGE, stride=4), :]
        xv_ref[rows, :] = r[pl.ds(1, PAGE, stride=4), :]

    @pl.when(j == pl.num_programs(1) - 1)
    def _():
        nb = xk_ref.shape[0] // CMP_BLOCK
        acc_k = jnp.zeros((nb, DK_C), F32)
        acc_v = jnp.zeros((nb, DK_C), F32)
        for i in range(CMP_BLOCK):
            xk = xk_ref[pl.ds(i, nb, stride=CMP_BLOCK), :]
            xv = xv_ref[pl.ds(i, nb, stride=CMP_BLOCK), :]
            wsl = slice(i * DK_C, (i + 1) * DK_C)
            acc_k = acc_k + _dot((xk + pe_ref[0, i:i + 1, :]).astype(BF16), w_ref[0, wsl, :])
            acc_v = acc_v + _dot((xv + pe_ref[1, i:i + 1, :]).astype(BF16), w_ref[1, wsl, :])
        o_ref[0] = acc_k
        o_ref[1] = acc_v


def _compress_paged(page_table, cache, layer, pe, w, n_group):
    b, n_pages = page_table.shape
    pt = page_table.reshape(-1)
    nb = n_pages * PAGE // CMP_BLOCK

    def page_spec(g):
        return pl.BlockSpec((None, None, PAGE * 4, DK_C),
                            lambda bi, j, pt_ref: (layer, pt_ref[bi * n_pages + j * n_group + g], 0, 0))

    grid_spec = pltpu.PrefetchScalarGridSpec(
        num_scalar_prefetch=1,
        grid=(b, n_pages // n_group),
        in_specs=[page_spec(g) for g in range(n_group)]
                 + [pl.BlockSpec((2, CMP_BLOCK, DK_C), lambda bi, j, pt_ref: (0, 0, 0)),
                    pl.BlockSpec((None, 2, CMP_BLOCK * DK_C, DK_C), lambda bi, j, pt_ref: (layer, 0, 0, 0))],
        out_specs=pl.BlockSpec((None, 2, nb, DK_C), lambda bi, j, pt_ref: (bi, 0, 0, 0)),
        scratch_shapes=[pltpu.VMEM((n_pages * PAGE, DK_C), F32), pltpu.VMEM((n_pages * PAGE, DK_C), F32)],
    )
    return pl.pallas_call(
        functools.partial(_compress_paged_kernel, n_group=n_group),
        grid_spec=grid_spec,
        out_shape=jax.ShapeDtypeStruct((b, 2, nb, DK_C), F32),
        compiler_params=_params("parallel", "arbitrary"),
        name="nsa_compress_paged",
    )(pt, *([cache] * n_group), pe, w)


def _nsa_sample_select_kernel(q_ref, kce_ref, kco_ref, vce_ref, vco_ref, o_ref, idx_ref, *, n_new, past_len, n_pick):
    qs = q_ref[...]
    rows = qs.shape[0]
    qpos = past_len + lax.broadcasted_iota(jnp.int32, (rows, 1), 0) % n_new
    o_cmp, pe, po = _cmp_attention(qs, qpos, kce_ref[...], kco_ref[...], vce_ref[...], vco_ref[...])
    o_ref[...] = o_cmp
    spe = pe[0:n_new]
    spo = po[0:n_new]
    for h in range(1, H_C):
        spe = spe + pe[h * n_new:(h + 1) * n_new]
        spo = spo + po[h * n_new:(h + 1) * n_new]
    score = _block_scores(spe + spo, qpos[0:n_new])
    _, idxs, _ = _top_blocks(score, n_pick)
    lane = lax.broadcasted_iota(jnp.int32, (n_new, 128), 1)
    out = jnp.zeros((n_new, 128), jnp.int32)
    for r, idx in enumerate(idxs):
        out = jnp.where(lane == r, idx, out)
    idx_ref[...] = out


def _nsa_sample_select(qs, kce, kco, vce, vco, n_new, past_len, n_pick):
    b, rows, _ = qs.shape
    nh = kce.shape[1]
    cm = pl.BlockSpec((None, nh, DK_C), lambda bi: (bi, 0, 0))
    return pl.pallas_call(
        functools.partial(_nsa_sample_select_kernel, n_new=n_new, past_len=past_len, n_pick=n_pick),
        grid=(b,),
        in_specs=[pl.BlockSpec((None, rows, DK_C), lambda bi: (bi, 0, 0)), cm, cm, cm, cm],
        out_specs=[pl.BlockSpec((None, rows, DK_C), lambda bi: (bi, 0, 0)),
                   pl.BlockSpec((None, n_new, 128), lambda bi: (bi, 0, 0))],
        out_shape=[jax.ShapeDtypeStruct((b, rows, DK_C), F32), jax.ShapeDtypeStruct((b, n_new, 128), jnp.int32)],
        compiler_params=_params("parallel"),
        name="nsa_sample_select",
    )(qs, kce, kco, vce, vco)


def _nsa_sample_attend_kernel(pt_ref, ix_ref, q_ref, *rest, n_pick, n_new, past_len):
    blk_refs = rest[:n_pick]
    new_ref, wst_ref, wnew_ref, ocmp_ref, g_ref, o_ref = rest[n_pick:]
    bi = pl.program_id(0)
    qi = pl.program_id(1)
    scale = DK_C ** -0.5
    q = q_ref[...].astype(BF16)
    qpos = past_len + qi
    lane_blk = lax.broadcasted_iota(jnp.int32, (1, SEL_BLOCK), 1)

    ks = [r[pl.ds(2, SEL_BLOCK, stride=4), :].astype(BF16) for r in blk_refs]
    vs = [r[pl.ds(3, SEL_BLOCK, stride=4), :].astype(BF16) for r in blk_refs]
    kpos = [ix_ref[(bi * n_new + qi) * TOPK + r] * SEL_BLOCK + lane_blk for r in range(n_pick)]
    new = jnp.concatenate([new_ref[...], jnp.zeros((SEL_BLOCK - n_new, 2 * DK_C), F32)], axis=0)
    ks.append(new[:, 0:DK_C].astype(BF16))
    vs.append(new[:, DK_C:2 * DK_C].astype(BF16))
    kpos.append(past_len + lane_blk)
    s = _dot_nt(q, jnp.concatenate(ks, axis=0)) * scale
    mask = jnp.concatenate(kpos, axis=1) <= qpos
    o_sel = _dot(_masked_softmax(s, mask).astype(BF16), jnp.concatenate(vs, axis=0))

    nbuf = wst_ref.shape[0]
    wnew = jnp.concatenate([wnew_ref[...], jnp.zeros((PAGE - n_new, 2 * DK_C), F32)], axis=0)
    kw = jnp.concatenate([wst_ref[:, 0:DK_C].astype(BF16), wnew[:, 0:DK_C].astype(BF16)], axis=0)
    vw = jnp.concatenate([wst_ref[:, DK_C:2 * DK_C].astype(BF16), wnew[:, DK_C:2 * DK_C].astype(BF16)], axis=0)
    wpos = past_len - nbuf + lax.broadcasted_iota(jnp.int32, (1, nbuf + PAGE), 1)
    d = qpos - wpos
    wmask = jnp.where(wpos >= 0, jnp.abs(2 * d - WINDOW), 4 * WINDOW) <= WINDOW
    s = _dot_nt(q, kw) * scale
    o_win = _dot(_masked_softmax(s, wmask).astype(BF16), vw)

    g = _sigmoid(g_ref[...])
    o_ref[...] = g[0] * ocmp_ref[...] + g[1] * o_sel + g[2] * o_win


def _nsa_sample_attend(page_table, idx, q8, cache, layer, nsa_new, win_state, win_new, ocmp8, gexp,
                       n_pick, past_len):
    b, n_pages = page_table.shape
    n_new = q8.shape[1]
    pt = page_table.reshape(-1)
    nbuf = win_state.shape[1]
    per_page = PAGE // SEL_BLOCK

    def blk_spec(r):
        def imap(bi, qi, pt_ref, ix_ref):
            blk = ix_ref[(bi * n_new + qi) * TOPK + r]
            return (layer, pt_ref[bi * n_pages + blk // per_page], blk % per_page, 0)
        return pl.BlockSpec((None, None, SEL_BLOCK * 4, DK_C), imap)

    per_q = lambda: pl.BlockSpec((None, None, 8, DK_C), lambda bi, qi, p, x: (bi, qi, 0, 0))
    grid_spec = pltpu.PrefetchScalarGridSpec(
        num_scalar_prefetch=2,
        grid=(b, n_new),
        in_specs=[per_q()] + [blk_spec(r) for r in range(n_pick)]
                 + [pl.BlockSpec((None, n_new, 2 * DK_C), lambda bi, qi, p, x: (bi, 0, 1)),
                    pl.BlockSpec((None, nbuf, 2 * DK_C), lambda bi, qi, p, x: (bi, 0, 0)),
                    pl.BlockSpec((None, n_new, 2 * DK_C), lambda bi, qi, p, x: (bi, 0, 0)),
                    per_q(),
                    pl.BlockSpec((None, None, 3, 8, DK_C), lambda bi, qi, p, x: (bi, qi, 0, 0, 0))],
        out_specs=per_q(),
    )
    return pl.pallas_call(
        functools.partial(_nsa_sample_attend_kernel, n_pick=n_pick, n_new=n_new, past_len=past_len),
        grid_spec=grid_spec,
        out_shape=jax.ShapeDtypeStruct((b, n_new, 8, DK_C), F32),
        compiler_params=_params("parallel", "parallel"),
        name="nsa_sample_attend",
    )(pt, idx, q8, *([cache] * n_pick), nsa_new, win_state, win_new, ocmp8, gexp)


def _tile(m, pref):
    if m <= pref:
        return m
    t = pref - pref % 128
    while m % t:
        t -= 128
    assert t > 0, (m, pref)
    return t


def _lambda_scalar(lp, layer):
    lam_init = 0.8 - 0.6 * math.exp(-0.3 * layer)
    lam = jnp.exp(jnp.sum(lp[0] * lp[1])) - jnp.exp(jnp.sum(lp[2] * lp[3])) + lam_init
    return lam.reshape(1).astype(F32), 1.0 - lam_init


def _dense_tail(x, xb, oa, ob, oc, p, w, layer, seq_len, conv_prev):
    m = x.shape[0]
    short = seq_len < 128
    mix = _merge(xb, w['w_mg'], oa, ob, oc, w['wa'], w['wb'], w['wc'], layer, _tile(m, 512), 512)
    x1, x1b = _proj_ln(mix, w['w_out'], layer, x, w['g0'], w['b0'], _tile(m, 512))
    if short:
        p1, p2 = conv_prev
        h, a = _ffn_up_short(x1b, w['w_fg'], w['w_fu'], layer, w['cw'], w['cb'], p1, p2, seq_len, 512)
        conv = a.reshape(m // seq_len, seq_len, D_FF_PAD)[:, seq_len - (CONV_W - 1):, :D_FF]
    else:
        h, st = _ffn_up_seq(x1b, w['w_fg'], w['w_fu'], layer, w['cw'], w['cb'], seq_len, _tile(seq_len, 2048), 512)
        conv = st[:, 8 - (CONV_W - 1):, :D_FF]
    x2, x2b = _ffn_down_ln(h, w['w_fd'], layer, x1, w['g1'], w['b1'], _tile(m, 512), D_FF_PAD // 4)
    x3, x3b = _ple_ln(x2b, w['w_pg'], p, w['w_pp'], layer, x2, w['g2'], w['b2'], _tile(m, 512))
    return x3, x3b, conv


def _prompt_layer(x, xb, p, w, layer, b, t, rope_tabs, ret_tabs):
    m = b * t
    u = _matmul(xb, w['w_in'], layer, F32, _tile(m, 1024), 1024, "in_proj")
    (qa, new_diff, kva, qb, kb, qc, new_nsa, sw, new_win, kcm, vcm) = _rope_split(u, rope_tabs, _tile(t, 256))

    lam, post = _lambda_scalar(w['diff_lambda'], layer)
    oa = _diff_attn_prompt(lam, qa, kva, w['diff_gain'], b, t, _tile(t, 256), post)

    chunk = RET_CHUNK
    s0 = jnp.zeros((b, H_B, DK_B, DV_B), F32)
    ob, ret = _retention(qb, kb, u, COL_BV // 512, COL_BG // 512, s0, ret_tabs, w['ret_gain'], b, t // chunk, chunk)

    nb = m // CMP_BLOCK
    xc = jnp.stack([kcm.reshape(nb, CMP_BLOCK * DK_C), vcm.reshape(nb, CMP_BLOCK * DK_C)])
    kvc = _compress(xc, w['cmp_pe'].reshape(2, 1, CMP_BLOCK * DK_C), w['cmp_w'], layer)
    kvc = kvc.reshape(2, b, t // CMP_BLOCK, DK_C).astype(BF16)
    oc = _nsa_prompt(qc, kvc[0, :, 0::2], kvc[0, :, 1::2], kvc[1, :, 0::2], kvc[1, :, 1::2], sw, u,
                     b, t, _tile(t, 128))

    x3, x3b, conv = _dense_tail(x, xb, oa, ob, oc, p, w, layer, t, None)
    nwin = min(WINDOW, t)
    states = (new_diff.reshape(b, t, 2, H_A, DV_A), new_nsa.reshape(b, t, 4, DK_C),
              new_win.reshape(b, t, 2, DK_C)[:, t - nwin:], ret, conv)
    return x3, x3b, states


def _sample_layer(x, xb, p, w, layer, b, t, past_len, rope_tabs, ret_tabs, page_table,
                  cache_diff, cache_nsa, win_state, ret_state, conv_state):
    m = b * t
    u = _matmul(xb, w['w_in'], layer, F32, m, 1024, "in_proj")
    (qa, new_diff, kva, qb, kb, qc, new_nsa, sw, new_win, kcm, vcm) = _rope_split(u, rope_tabs, m)

    lam, post = _lambda_scalar(w['diff_lambda'], layer)
    q5 = qa.reshape(b, t, H_A, 2, DK_A).transpose(0, 2, 3, 1, 4)
    eye = jnp.eye(H_A * 2, dtype=BF16).reshape(H_A, 2, 1, H_A, 2, 1)
    q_exp = (q5[:, :, :, :, None, None, :] * eye[None]).reshape(b, H_A * 2 * t, H_A * 2 * DK_A)
    n_phys = cache_diff.shape[1]
    oa = _diff_attn_paged(page_table, lam, q_exp, cache_diff.reshape(-1, n_phys, PAGE * 2 * H_A, DV_A),
                          layer, new_diff.reshape(b, t, 2 * H_A * DV_A), w['diff_gain'], 8, post)
    oa = oa.reshape(m, H_A * DV_A).astype(BF16)

    rows = RET_CHUNK
    padr = lambda a: jnp.pad(a.reshape(b, t, a.shape[-1]), ((0, 0), (0, rows - t), (0, 0))).reshape(b * rows, -1)
    ob, ret = _retention(padr(qb), padr(kb), padr(u[:, COL_BV:COL_CQ]), 0, 2, ret_state, ret_tabs,
                         w['ret_gain'], b, 1, rows)
    ob = ob.reshape(b, rows, H_B * DV_B)[:, :t].reshape(m, H_B * DV_B)

    n_pages = page_table.shape[1]
    cache_rows = cache_nsa.reshape(-1, n_phys, PAGE * 4, DK_C)
    kvc = _compress_paged(page_table, cache_rows, layer, w['cmp_pe'], w['cmp_w'], 16).astype(BF16)
    qs = qc.reshape(b, t, H_C, DK_C).transpose(0, 2, 1, 3).reshape(b, H_C * t, DK_C)
    n_pick = TOPK - 1
    assert past_len % SEL_BLOCK == 0 and t <= SEL_BLOCK and past_len // SEL_BLOCK >= n_pick
    ocmp, idx = _nsa_sample_select(qs, kvc[:, 0, 0::2], kvc[:, 0, 1::2], kvc[:, 1, 0::2], kvc[:, 1, 1::2],
                                   t, past_len, n_pick)
    pad8 = lambda a: jnp.pad(a, ((0, 0),) * (a.ndim - 2) + ((0, 8 - H_C), (0, 0)))
    q8 = pad8(qc.astype(F32).reshape(b, t, H_C, DK_C))
    ocmp8 = pad8(ocmp.reshape(b, H_C, t, DK_C).transpose(0, 2, 1, 3))
    cg = u[:, COL_CG:COL_CG + 3 * H_C].reshape(b, t, 3, H_C)
    gexp = jnp.broadcast_to(pad8(cg[..., None]), (b, t, 3, 8, DK_C))
    oc = _nsa_sample_attend(page_table, idx[:, :, :TOPK].reshape(-1), q8, cache_rows, layer,
                            new_nsa.reshape(b, t, 4 * DK_C), win_state.reshape(b, -1, 2 * DK_C),
                            new_win.reshape(b, t, 2 * DK_C), ocmp8, gexp, n_pick, past_len)
    oc = oc[:, :, :H_C].reshape(m, H_C * DK_C).astype(BF16)

    cs = jnp.pad(conv_state, ((0, 0), (0, 0), (0, D_FF_PAD - D_FF)))
    zeros = jnp.zeros((b, t - 1, D_FF_PAD), F32)
    p1 = jnp.concatenate([cs[:, 1:2], zeros], axis=1).reshape(m, D_FF_PAD)
    p2 = jnp.concatenate([cs, zeros[:, 1:]], axis=1).reshape(m, D_FF_PAD)

    x3, x3b, conv = _dense_tail(x, xb, oa, ob, oc, p, w, layer, t, (p1, p2))
    nbuf = win_state.shape[1]
    win = jnp.concatenate([win_state, new_win.reshape(b, t, 2, DK_C)], axis=1)[:, t:]
    assert win.shape[1] == nbuf
    states = (new_diff.reshape(b, t, 2, H_A, DV_A), new_nsa.reshape(b, t, 4, DK_C), win, ret, conv)
    return x3, x3b, states


def _layer_weights(i, ln_gain, ln_bias, w_in, diff_lambda, diff_norm_gain, ret_norm_gain, nsa_cmp_pos, nsa_cmp_w,
                   w_branch_a, w_branch_b, w_branch_c, w_merge_gate, w_out, w_ffn_gate, w_ffn_up, ffn_conv_w,
                   ffn_conv_b, w_ffn_down, w_ple_gate, w_ple_proj):
    bf = lambda a: a.astype(BF16)
    padc = lambda a, n: jnp.pad(a, ((0, 0),) * (a.ndim - 1) + ((0, n - a.shape[-1]),))
    return {
        'w_in': padc(bf(w_in), N_IN_PAD),
        'diff_lambda': diff_lambda[i].astype(F32),
        'diff_gain': diff_norm_gain[i].reshape(1, DV_A),
        'ret_gain': ret_norm_gain[i].reshape(1, H_B * DV_B),
        'cmp_pe': nsa_cmp_pos[i],
        'cmp_w': bf(nsa_cmp_w),
        'wa': bf(w_branch_a), 'wb': bf(w_branch_b), 'wc': bf(w_branch_c),
        'w_mg': bf(w_merge_gate), 'w_out': bf(w_out),
        'w_fg': padc(bf(w_ffn_gate), D_FF_PAD), 'w_fu': padc(bf(w_ffn_up), D_FF_PAD),
        'cw': padc(ffn_conv_w[i], D_FF_PAD), 'cb': padc(ffn_conv_b[i].reshape(1, D_FF), D_FF_PAD),
        'w_fd': jnp.pad(bf(w_ffn_down), ((0, 0), (0, D_FF_PAD - D_FF), (0, 0))),
        'w_pg': bf(w_ple_gate), 'w_pp': bf(w_ple_proj),
        'g0': ln_gain[i, 0:1], 'g1': ln_gain[i, 1:2], 'g2': ln_gain[i, 2:3],
        'b0': ln_bias[i, 0:1], 'b1': ln_bias[i, 1:2], 'b2': ln_bias[i, 2:3],
    }


def kernel(x_prompt, x_sample, cache_diff_kv, cache_nsa_kv, state_nsa_win, state_ret, state_conv, page_table,
           p_prompt, p_sample, ln_gain, ln_bias, w_in, diff_lambda, diff_norm_gain, ret_norm_gain, nsa_cmp_pos,
           nsa_cmp_w, w_branch_a, w_branch_b, w_branch_c, w_merge_gate, w_out, w_ffn_gate, w_ffn_up, ffn_conv_w,
           ffn_conv_b, w_ffn_down, w_ple_gate, w_ple_proj):
    bp, tp, _ = x_prompt.shape
    bs, ts, _ = x_sample.shape
    n_layers = w_in.shape[0]
    past_len = page_table.shape[1] * PAGE

    rope_p = _rope_tables(jnp.arange(tp, dtype=jnp.int32))
    rope_s = tuple(jnp.tile(tb, (bs, 1)) for tb in _rope_tables(past_len + jnp.arange(ts, dtype=jnp.int32)))
    ret_p = _retention_tables(RET_CHUNK, RET_CHUNK)
    ret_s = _retention_tables(ts, RET_CHUNK)

    xp = x_prompt.reshape(bp * tp, D_MODEL)
    xs = x_sample.reshape(bs * ts, D_MODEL)
    xpb, xsb = xp.astype(BF16), xs.astype(BF16)
    st_p, st_s = [], []
    for i in range(n_layers):
        w = _layer_weights(i, ln_gain, ln_bias, w_in, diff_lambda, diff_norm_gain, ret_norm_gain, nsa_cmp_pos,
                           nsa_cmp_w, w_branch_a, w_branch_b, w_branch_c, w_merge_gate, w_out, w_ffn_gate,
                           w_ffn_up, ffn_conv_w, ffn_conv_b, w_ffn_down, w_ple_gate, w_ple_proj)
        xp, xpb, sp = _prompt_layer(xp, xpb, p_prompt[i].reshape(bp * tp, PLE_DIM), w, i, bp, tp, rope_p, ret_p)
        xs, xsb, ss = _sample_layer(xs, xsb, p_sample[i].reshape(bs * ts, PLE_DIM), w, i, bs, ts, past_len,
                                    rope_s, ret_s, page_table, cache_diff_kv, cache_nsa_kv, state_nsa_win[i],
                                    state_ret[i], state_conv[i])
        st_p.append(sp)
        st_s.append(ss)
    diff_p, nsa_p, win_p, ret_p_out, conv_p = [jnp.stack(s) for s in zip(*st_p)]
    diff_s, nsa_s, win_s, ret_s_out, conv_s = [jnp.stack(s) for s in zip(*st_s)]
    return (xp.reshape(bp, tp, D_MODEL), xs.reshape(bs, ts, D_MODEL), diff_p, diff_s, nsa_p, nsa_s,
            win_p, win_s, ret_p_out, ret_s_out, conv_p, conv_s)
```

```python
import functools
import math

import jax
import jax.numpy as jnp
from jax import lax
from jax.experimental import pallas as pl
from jax.experimental.pallas import tpu as pltpu

F32 = jnp.float32
BF16 = jnp.bfloat16

D_MODEL = 2048
H_A, DK_A, DV_A = 4, 64, 128
H_B, DK_B, DV_B = 4, 128, 256
RET_CHUNK = 128
RET_THETA = 10000.0
H_C, DK_C = 4, 128
CMP_BLOCK, SEL_BLOCK, TOPK, WINDOW = 32, 64, 16, 512
FORCED_SCORE = H_C + 1.0
ROPE_THETA = 500000.0
D_FF = 5504
CONV_W = 3
PLE_DIM = 256
N_LAYERS = 4
ALPHA = (2 * N_LAYERS) ** 0.25
EPS = 1e-5
PAGE = 128

N_IN = 5900
N_IN_PAD = 6144
COL_AQ, COL_AK, COL_AV = 0, 512, 1024
COL_BQ, COL_BK, COL_BV, COL_BG = 1536, 2048, 2560, 3584
COL_CQ, COL_CKV, COL_CWIN, COL_CG = 4608, 5120, 5632, 5888
D_FF_PAD = 5632
CMP_PITCH = 40

VMEM_LIMIT = 56 * 1024 * 1024

NEG = float(jnp.finfo(jnp.float32).min)
TINY = float(jnp.finfo(jnp.float32).tiny)


def _params(*sem):
    return pltpu.CompilerParams(dimension_semantics=sem, vmem_limit_bytes=VMEM_LIMIT)


def _dot(a, b):
    return jnp.dot(a, b, preferred_element_type=F32)


def _dot_nt(a, b):
    return lax.dot_general(a, b, (((1,), (1,)), ((), ())), preferred_element_type=F32)


def _dot_tn(a, b):
    return lax.dot_general(a, b, (((0,), (0,)), ((), ())), preferred_element_type=F32)


def _masked_softmax(s, mask):
    s = jnp.where(mask, s, NEG)
    e = jnp.where(mask, jnp.exp(s - jnp.max(s, axis=-1, keepdims=True)), 0.0)
    return e / jnp.maximum(jnp.sum(e, axis=-1, keepdims=True), TINY)


def _visible_softmax_terms(s, mask):
    s = jnp.where(mask, s, NEG)
    e = jnp.exp(s - jnp.max(s, axis=-1, keepdims=True))
    return e, jnp.sum(e, axis=-1, keepdims=True)


def _layer_norm(y, g, b):
    mu = jnp.mean(y, axis=-1, keepdims=True)
    d = y - mu
    var = jnp.mean(d * d, axis=-1, keepdims=True)
    return d * lax.rsqrt(var + EPS) * g + b


def _sigmoid(x):
    return 1.0 / (1.0 + jnp.exp(-x))


def _mm_kernel(x_ref, w_ref, o_ref):
    o_ref[...] = _dot(x_ref[...], w_ref[...]).astype(o_ref.dtype)


def _matmul(x, w, layer, out_dtype, tm, tn, name):
    m, k = x.shape
    n = w.shape[2]
    return pl.pallas_call(
        _mm_kernel,
        grid=(n // tn, m // tm),
        in_specs=[pl.BlockSpec((tm, k), lambda j, i: (i, 0)),
                  pl.BlockSpec((None, k, tn), lambda j, i: (layer, 0, j))],
        out_specs=pl.BlockSpec((tm, tn), lambda j, i: (i, j)),
        out_shape=jax.ShapeDtypeStruct((m, n), out_dtype),
        compiler_params=_params("parallel", "parallel"),
        name=name,
    )(x, w)


def _merge_kernel(x_ref, w0_ref, w1_ref, w2_ref, oa_ref, ob_ref, oc_ref, wa_ref, wb_ref, wc_ref, o_ref):
    x = x_ref[...]
    m = _sigmoid(_dot(x, w0_ref[...])) * _dot(oa_ref[...], wa_ref[...])
    m = m + _sigmoid(_dot(x, w1_ref[...])) * _dot(ob_ref[...], wb_ref[...])
    m = m + _sigmoid(_dot(x, w2_ref[...])) * _dot(oc_ref[...], wc_ref[...])
    o_ref[...] = m.astype(o_ref.dtype)


def _merge(xb, w_mg, oa, ob, oc, wa, wb, wc, layer, tm, tn):
    m = xb.shape[0]
    nj = D_MODEL // tn
    row = lambda width: pl.BlockSpec((tm, width), lambda j, i: (i, 0))
    col = lambda k, off: pl.BlockSpec((None, k, tn), lambda j, i: (layer, 0, j + off))
    return pl.pallas_call(
        _merge_kernel,
        grid=(nj, m // tm),
        in_specs=[row(D_MODEL), col(D_MODEL, 0), col(D_MODEL, nj), col(D_MODEL, 2 * nj),
                  row(H_A * DV_A), row(H_B * DV_B), row(H_C * DK_C),
                  col(H_A * DV_A, 0), col(H_B * DV_B, 0), col(H_C * DK_C, 0)],
        out_specs=pl.BlockSpec((tm, tn), lambda j, i: (i, j)),
        out_shape=jax.ShapeDtypeStruct((m, D_MODEL), BF16),
        compiler_params=_params("parallel", "parallel"),
        name="merge_gate",
    )(xb, w_mg, w_mg, w_mg, oa, ob, oc, wa, wb, wc)


def _proj_ln_kernel(m_ref, w_ref, x_ref, g_ref, b_ref, o_ref, ob_ref):
    y = ALPHA * x_ref[...] + _dot(m_ref[...], w_ref[...])
    out = _layer_norm(y, g_ref[...], b_ref[...])
    o_ref[...] = out
    ob_ref[...] = out.astype(BF16)


def _resident_weight(k, n, layer):
    return pl.BlockSpec((None, k, n), lambda i: (layer, 0, 0), pipeline_mode=pl.Buffered(1))


def _proj_ln(mb, w, layer, x, g, b, tm):
    m = x.shape[0]
    k = mb.shape[1]
    row = lambda width: pl.BlockSpec((tm, width), lambda i: (i, 0))
    full = lambda r, c: pl.BlockSpec((r, c), lambda i: (0, 0))
    return pl.pallas_call(
        _proj_ln_kernel,
        grid=(m // tm,),
        in_specs=[row(k), _resident_weight(k, D_MODEL, layer), row(D_MODEL), full(1, D_MODEL), full(1, D_MODEL)],
        out_specs=[row(D_MODEL), row(D_MODEL)],
        out_shape=[jax.ShapeDtypeStruct((m, D_MODEL), F32), jax.ShapeDtypeStruct((m, D_MODEL), BF16)],
        compiler_params=_params("parallel"),
        name="out_proj_ln",
    )(mb, w, x, g, b)


def _ple_ln_kernel(xb_ref, wg_ref, p_ref, wp_ref, x_ref, g_ref, b_ref, o_ref, ob_ref):
    pe = _sigmoid(_dot(xb_ref[...], wg_ref[...])) * _dot(p_ref[...].astype(BF16), wp_ref[...])
    out = _layer_norm(ALPHA * x_ref[...] + pe, g_ref[...], b_ref[...])
    o_ref[...] = out
    ob_ref[...] = out.astype(BF16)


def _ple_ln(xb, wg, p, wp, layer, x, g, b, tm):
    m = x.shape[0]
    row = lambda width: pl.BlockSpec((tm, width), lambda i: (i, 0))
    full = lambda r, c: pl.BlockSpec((r, c), lambda i: (0, 0))
    return pl.pallas_call(
        _ple_ln_kernel,
        grid=(m // tm,),
        in_specs=[row(D_MODEL), _resident_weight(D_MODEL, D_MODEL, layer), row(PLE_DIM),
                  _resident_weight(PLE_DIM, D_MODEL, layer), row(D_MODEL), full(1, D_MODEL), full(1, D_MODEL)],
        out_specs=[row(D_MODEL), row(D_MODEL)],
        out_shape=[jax.ShapeDtypeStruct((m, D_MODEL), F32), jax.ShapeDtypeStruct((m, D_MODEL), BF16)],
        compiler_params=_params("parallel"),
        name="ple_ln",
    )(xb, wg, p, wp, x, g, b)


def _conv_silu(a, a1, a2, u, cw_ref, cb_ref):
    ac = cb_ref[...] + ((a2 * cw_ref[0:1, :] + a1 * cw_ref[1:2, :]) + a * cw_ref[2:3, :])
    return (ac * _sigmoid(ac)) * u


def _ffn_up_seq_kernel(x_ref, wg_ref, wu_ref, cw_ref, cb_ref, h_ref, st_ref, carry_ref, *, tiles_per_seq):
    i = pl.program_id(1)
    x = x_ref[...]
    a = _dot(x, wg_ref[...])
    u = _dot(x, wu_ref[...])
    tm = a.shape[0]

    @pl.when(i % tiles_per_seq == 0)
    def _():
        carry_ref[...] = jnp.zeros_like(carry_ref)

    prev = carry_ref[...]
    row = lax.broadcasted_iota(jnp.int32, a.shape, 0)
    a1 = jnp.where(row == 0, prev[7:8, :], pltpu.roll(a, 1, 0))
    a2 = jnp.where(row == 0, prev[6:7, :], jnp.where(row == 1, prev[7:8, :], pltpu.roll(a, 2, 0)))
    h_ref[...] = _conv_silu(a, a1, a2, u, cw_ref, cb_ref).astype(h_ref.dtype)
    tail = a[tm - 8:tm, :]
    carry_ref[...] = tail
    st_ref[...] = tail


def _ffn_up_seq(xb, wg, wu, layer, cw, cb, seq_len, tm, tn):
    m = xb.shape[0]
    tps = seq_len // tm
    return pl.pallas_call(
        functools.partial(_ffn_up_seq_kernel, tiles_per_seq=tps),
        grid=(D_FF_PAD // tn, m // tm),
        in_specs=[pl.BlockSpec((tm, D_MODEL), lambda j, i: (i, 0)),
                  pl.BlockSpec((None, D_MODEL, tn), lambda j, i: (layer, 0, j)),
                  pl.BlockSpec((None, D_MODEL, tn), lambda j, i: (layer, 0, j)),
                  pl.BlockSpec((CONV_W, tn), lambda j, i: (0, j)),
                  pl.BlockSpec((1, tn), lambda j, i: (0, j))],
        out_specs=[pl.BlockSpec((tm, tn), lambda j, i: (i, j)),
                   pl.BlockSpec((None, 8, tn), lambda j, i: (i // tps, 0, j))],
        out_shape=[jax.ShapeDtypeStruct((m, D_FF_PAD), BF16),
                   jax.ShapeDtypeStruct((m // seq_len, 8, D_FF_PAD), F32)],
        scratch_shapes=[pltpu.VMEM((8, tn), F32)],
        compiler_params=_params("arbitrary", "arbitrary"),
        name="ffn_up_conv_prompt",
    )(xb, wg, wu, cw, cb)


def _ffn_up_short_kernel(x_ref, wg_ref, wu_ref, cw_ref, cb_ref, p1_ref, p2_ref, h_ref, a_ref, *, seq_len):
    x = x_ref[...]
    a = _dot(x, wg_ref[...])
    u = _dot(x, wu_ref[...])
    t = lax.broadcasted_iota(jnp.int32, a.shape, 0) % seq_len
    a1 = jnp.where(t == 0, p1_ref[...], pltpu.roll(a, 1, 0))
    a2 = jnp.where(t < 2, p2_ref[...], pltpu.roll(a, 2, 0))
    h_ref[...] = _conv_silu(a, a1, a2, u, cw_ref, cb_ref).astype(h_ref.dtype)
    a_ref[...] = a


def _ffn_up_short(xb, wg, wu, layer, cw, cb, p1, p2, seq_len, tn):
    m = xb.shape[0]
    colb = lambda r: pl.BlockSpec((r, tn), lambda j: (0, j))
    wcol = pl.BlockSpec((None, D_MODEL, tn), lambda j: (layer, 0, j))
    return pl.pallas_call(
        functools.partial(_ffn_up_short_kernel, seq_len=seq_len),
        grid=(D_FF_PAD // tn,),
        in_specs=[pl.BlockSpec((m, D_MODEL), lambda j: (0, 0)), wcol, wcol,
                  colb(CONV_W), colb(1), colb(m), colb(m)],
        out_specs=[colb(m), colb(m)],
        out_shape=[jax.ShapeDtypeStruct((m, D_FF_PAD), BF16), jax.ShapeDtypeStruct((m, D_FF_PAD), F32)],
        compiler_params=_params("parallel"),
        name="ffn_up_conv_sample",
    )(xb, wg, wu, cw, cb, p1, p2)


def _ffn_down_ln_kernel(h_ref, w_ref, x_ref, g_ref, b_ref, o_ref, ob_ref, acc_ref):
    k = pl.program_id(1)

    @pl.when(k == 0)
    def _():
        acc_ref[...] = jnp.zeros_like(acc_ref)

    acc_ref[...] += _dot(h_ref[...], w_ref[...])

    @pl.when(k == pl.num_programs(1) - 1)
    def _():
        out = _layer_norm(ALPHA * x_ref[...] + acc_ref[...], g_ref[...], b_ref[...])
        o_ref[...] = out
        ob_ref[...] = out.astype(BF16)


def _ffn_down_ln(h, w, layer, x, g, b, tm, tk):
    m = x.shape[0]
    row = pl.BlockSpec((tm, D_MODEL), lambda i, k: (i, 0))
    vec = pl.BlockSpec((1, D_MODEL), lambda i, k: (0, 0))
    return pl.pallas_call(
        _ffn_down_ln_kernel,
        grid=(m // tm, D_FF_PAD // tk),
        in_specs=[pl.BlockSpec((tm, tk), lambda i, k: (i, k)),
                  pl.BlockSpec((None, tk, D_MODEL), lambda i, k: (layer, k, 0)), row, vec, vec],
        out_specs=[row, row],
        out_shape=[jax.ShapeDtypeStruct((m, D_MODEL), F32), jax.ShapeDtypeStruct((m, D_MODEL), BF16)],
        scratch_shapes=[pltpu.VMEM((tm, D_MODEL), F32)],
        compiler_params=_params("parallel", "arbitrary"),
        name="ffn_down_ln",
    )(h, w, x, g, b)


def _rope128(x, c, s_up, s_dn, shift):
    y = x * c + pltpu.roll(x, 128 - shift, 1) * s_up
    if s_dn is not None:
        y = y + pltpu.roll(x, shift, 1) * s_dn
    return y


def _rope_split_kernel(aq_ref, ak_ref, av_ref, bq_ref, bk_ref, cq_ref, ckv_ref, cw_ref,
                       ca_ref, sau_ref, sad_ref, cb_ref, sb_ref, cc_ref, scu_ref, scd_ref,
                       qa_ref, nd_ref, kva_ref, qb_ref, kb_ref, qc_ref, nn_ref, sw_ref, nw_ref, kcm_ref, vcm_ref):
    ca, sau, sad = ca_ref[...], sau_ref[...], sad_ref[...]
    cb, sb = cb_ref[...], sb_ref[...]
    cc, scu, scd = cc_ref[...], scu_ref[...], scd_ref[...]
    rot_a = DK_A // 8
    rot_c = DK_C // 8
    for h in range(4):
        sl = slice(h * 128, (h + 1) * 128)
        qa_ref[:, sl] = _rope128(aq_ref[:, sl], ca, sau, sad, rot_a).astype(BF16)
        ka = _rope128(ak_ref[:, sl], ca, sau, sad, rot_a)
        nd_ref[:, sl] = ka
        kva_ref[:, sl] = ka.astype(BF16)
        qb_ref[:, sl] = _rope128(bq_ref[:, sl], cb, sb, None, DK_B // 2).astype(BF16)
        kb_ref[:, sl] = _rope128(bk_ref[:, sl], cb, sb, None, DK_B // 2) * (DK_B ** -0.5)
        qc_ref[:, sl] = _rope128(cq_ref[:, sl], cc, scu, scd, rot_c).astype(BF16)
    av = av_ref[...]
    nd_ref[:, 512:1024] = av
    kva_ref[:, 512:1024] = av.astype(BF16)
    k_cmp = _rope128(ckv_ref[:, 0:128], cc, scu, scd, rot_c)
    v_cmp = ckv_ref[:, 128:256]
    k_sel = _rope128(ckv_ref[:, 256:384], cc, scu, scd, rot_c)
    v_sel = ckv_ref[:, 384:512]
    k_win = _rope128(cw_ref[:, 0:128], cc, scu, scd, rot_c)
    v_win = cw_ref[:, 128:256]
    nn_ref[:, 0:128] = k_cmp
    nn_ref[:, 128:256] = v_cmp
    nn_ref[:, 256:384] = k_sel
    nn_ref[:, 384:512] = v_sel
    sw_ref[:, 0:128] = k_sel.astype(BF16)
    sw_ref[:, 128:256] = v_sel.astype(BF16)
    sw_ref[:, 256:384] = k_win.astype(BF16)
    sw_ref[:, 384:512] = v_win.astype(BF16)
    nw_ref[:, 0:128] = k_win
    nw_ref[:, 128:256] = v_win
    kcm_ref[...] = k_cmp
    vcm_ref[...] = v_cmp


def _rope_split(u, tables, tm):
    m = u.shape[0]
    n_tab = tables[0].shape[0] // tm
    ub = lambda width, blk: pl.BlockSpec((tm, width), lambda i: (i, blk))
    tab = pl.BlockSpec((tm, 128), lambda i: (i % n_tab, 0))
    out = lambda width: pl.BlockSpec((tm, width), lambda i: (i, 0))
    shp = lambda width, dt: jax.ShapeDtypeStruct((m, width), dt)
    return pl.pallas_call(
        _rope_split_kernel,
        grid=(m // tm,),
        in_specs=[ub(512, COL_AQ // 512), ub(512, COL_AK // 512), ub(512, COL_AV // 512),
                  ub(512, COL_BQ // 512), ub(512, COL_BK // 512), ub(512, COL_CQ // 512),
                  ub(512, COL_CKV // 512), ub(256, COL_CWIN // 256)] + [tab] * 8,
        out_specs=[out(512), out(1024), out(1024), out(512), out(512), out(512), out(512), out(512),
                   out(256), out(128), out(128)],
        out_shape=[shp(512, BF16), shp(1024, F32), shp(1024, BF16), shp(512, BF16), shp(512, F32),
                   shp(512, BF16), shp(512, F32), shp(512, BF16), shp(256, F32), shp(128, F32), shp(128, F32)],
        compiler_params=_params("parallel"),
        name="rope_split",
    )(u, u, u, u, u, u, u, u, *tables)


def _rope_tables(pos):
    posf = pos.astype(F32)[:, None]
    lane = jnp.arange(128)

    def cs(half, theta):
        inv_freq = jnp.exp(-math.log(theta) * jnp.arange(half, dtype=F32) / half)
        ang = posf * inv_freq[None, :]
        return jnp.cos(ang), jnp.sin(ang)

    def partial_tables(head_dim, half, theta):
        cos, sin = cs(half, theta)
        d = lane % head_dim
        lo = d < half
        hi = (d >= half) & (d < 2 * half)
        idx = jnp.where(lo, d, jnp.where(hi, d - half, 0))
        c = jnp.where((lo | hi)[None, :], cos[:, idx], 1.0)
        s_up = jnp.where(lo[None, :], -sin[:, idx], 0.0)
        s_dn = jnp.where(hi[None, :], sin[:, idx], 0.0)
        return c, s_up, s_dn

    ca, sau, sad = partial_tables(DK_A, DK_A // 8, ROPE_THETA)
    cc, scu, scd = partial_tables(DK_C, DK_C // 8, ROPE_THETA)
    cosb, sinb = cs(DK_B // 2, RET_THETA)
    cb = jnp.concatenate([cosb, cosb], axis=1)
    sb = jnp.concatenate([-sinb, sinb], axis=1)
    return (ca, sau, sad, cb, sb, cc, scu, scd)


def _rms_head(o, gain, post_scale):
    return o * lax.rsqrt(jnp.mean(o * o, axis=-1, keepdims=True) + EPS) * gain * post_scale


def _diff_attn_prompt_kernel(lam_ref, q_ref, k_ref, v_ref, gain_ref, o_ref, *, tq, post_scale, n_bucket):
    qi = pl.program_id(2)
    q = q_ref[...]
    t = k_ref.shape[0]
    lane = lax.broadcasted_iota(jnp.int32, q.shape, 1)
    zero = jnp.zeros_like(q)
    scale = DK_A ** -0.5
    q1 = jnp.where(lane < DK_A, q, zero) * scale
    q2 = jnp.where(lane >= DK_A, q, zero) * scale
    qpos = qi * tq + lax.broadcasted_iota(jnp.int32, (tq, 1), 0)
    kb = t // n_bucket
    bucket = lax.div((qi + 1) * tq - 1, kb)

    for nb in range(n_bucket):
        @pl.when(bucket == nb)
        def _(nkeys=(nb + 1) * kb):
            k = k_ref[0:nkeys, :]
            v = v_ref[0:nkeys, :]
            mask = lax.broadcasted_iota(jnp.int32, (1, nkeys), 1) <= qpos
            e1, l1 = _visible_softmax_terms(_dot_nt(q1, k), mask)
            e2, l2 = _visible_softmax_terms(_dot_nt(q2, k), mask)
            o = _dot(e1.astype(BF16), v) * (1.0 / l1) - (lam_ref[0] / l2) * _dot(e2.astype(BF16), v)
            o_ref[...] = _rms_head(o, gain_ref[...], post_scale).astype(o_ref.dtype)


def _key_buckets(t, tq):
    n_bucket = 4
    return n_bucket if t % (n_bucket * 128) == 0 and (t // n_bucket) % tq == 0 else 1


def _diff_attn_prompt(lam, qa, kva, gain, b, t, tq, post_scale):
    m = qa.shape[0]
    nq = t // tq
    return pl.pallas_call(
        functools.partial(_diff_attn_prompt_kernel, tq=tq, post_scale=post_scale, n_bucket=_key_buckets(t, tq)),
        grid=(b, H_A, nq),
        in_specs=[pl.BlockSpec(memory_space=pltpu.SMEM),
                  pl.BlockSpec((tq, 128), lambda bi, h, qi: (bi * nq + qi, h)),
                  pl.BlockSpec((t, 128), lambda bi, h, qi: (bi, h)),
                  pl.BlockSpec((t, 128), lambda bi, h, qi: (bi, H_A + h)),
                  pl.BlockSpec((1, 128), lambda bi, h, qi: (0, 0))],
        out_specs=pl.BlockSpec((tq, 128), lambda bi, h, qi: (bi * nq + qi, h)),
        out_shape=jax.ShapeDtypeStruct((m, H_A * DV_A), BF16),
        compiler_params=_params("parallel", "parallel", "parallel"),
        name="diff_attn_prompt",
    )(lam, qa, kva, kva, gain)


def _diff_attn_paged_kernel(pt_ref, lam_ref, q_ref, *rest, n_group, n_new, post_scale):
    page_refs = rest[:n_group]
    kvn_ref, gain_ref, o_ref, m_ref, l_ref, acc_ref = rest[n_group:]
    j = pl.program_id(1)
    scale = DK_A ** -0.5
    q = q_ref[...]
    hk = H_A * 2 * DK_A

    @pl.when(j == 0)
    def _():
        m_ref[...] = jnp.full_like(m_ref, NEG)
        l_ref[...] = jnp.zeros_like(l_ref)
        acc_ref[...] = jnp.zeros_like(acc_ref)

    def update(s, v):
        m_old = m_ref[...]
        m_new = jnp.maximum(m_old, jnp.max(s, axis=-1, keepdims=True))
        a = jnp.exp(m_old - m_new)
        p = jnp.exp(s - m_new[:, 0:1])
        l_ref[...] = a * l_ref[...] + jnp.sum(p, axis=-1, keepdims=True)
        acc_ref[...] = a[:, 0:1] * acc_ref[...] + _dot(p.astype(BF16), v)
        m_ref[...] = m_new

    def heads(r, first):
        return jnp.concatenate([r[pl.ds(first + h, PAGE, stride=2 * H_A), :].astype(BF16) for h in range(H_A)],
                               axis=1)

    k = jnp.concatenate([heads(r, 0) for r in page_refs], axis=0)
    v = jnp.concatenate([heads(r, H_A) for r in page_refs], axis=0)
    update(_dot_nt(q, k) * scale, v)

    @pl.when(j == pl.num_programs(1) - 1)
    def _():
        pad = jnp.zeros((PAGE - n_new, 2 * hk), F32)
        kvn = jnp.concatenate([kvn_ref[...], pad], axis=0)
        s = _dot_nt(q, kvn[:, 0:hk].astype(BF16)) * scale
        tq = lax.broadcasted_iota(jnp.int32, s.shape, 0) % n_new
        tk = lax.broadcasted_iota(jnp.int32, s.shape, 1)
        update(jnp.where(tk <= tq, s, NEG), kvn[:, hk:2 * hk].astype(BF16))
        o = acc_ref[...] / l_ref[:, 0:1]
        lam = lam_ref[0]
        for h in range(H_A):
            r0 = h * 2 * n_new
            sl = slice(h * DV_A, (h + 1) * DV_A)
            oh = o[r0:r0 + n_new, sl] - lam * o[r0 + n_new:r0 + 2 * n_new, sl]
            o_ref[:, sl] = _rms_head(oh, gain_ref[...], post_scale)


def _diff_attn_paged(page_table, lam, q_exp, cache, layer, kv_new, gain, n_group, post_scale):
    b, n_pages = page_table.shape
    n_new = kv_new.shape[1]
    rows = q_exp.shape[1]
    width = kv_new.shape[2]
    pt = page_table.reshape(-1)

    def page_spec(g):
        return pl.BlockSpec((None, None, PAGE * 2 * H_A, DV_A),
                            lambda bi, j, pt_ref: (layer, pt_ref[bi * n_pages + j * n_group + g], 0, 0))

    grid_spec = pltpu.PrefetchScalarGridSpec(
        num_scalar_prefetch=1,
        grid=(b, n_pages // n_group),
        in_specs=[pl.BlockSpec(memory_space=pltpu.SMEM),
                  pl.BlockSpec((None, rows, q_exp.shape[2]), lambda bi, j, pt_ref: (bi, 0, 0))]
                 + [page_spec(g) for g in range(n_group)]
                 + [pl.BlockSpec((None, n_new, width), lambda bi, j, pt_ref: (bi, 0, 0)),
                    pl.BlockSpec((1, DV_A), lambda bi, j, pt_ref: (0, 0))],
        out_specs=pl.BlockSpec((None, n_new, H_A * DV_A), lambda bi, j, pt_ref: (bi, 0, 0)),
        scratch_shapes=[pltpu.VMEM((rows, 128), F32), pltpu.VMEM((rows, 128), F32),
                        pltpu.VMEM((rows, H_A * DV_A), F32)],
    )
    return pl.pallas_call(
        functools.partial(_diff_attn_paged_kernel, n_group=n_group, n_new=n_new, post_scale=post_scale),
        grid_spec=grid_spec,
        out_shape=jax.ShapeDtypeStruct((b, n_new, H_A * DV_A), F32),
        compiler_params=_params("parallel", "arbitrary"),
        name="diff_attn_paged",
    )(pt, lam, q_exp, *([cache] * n_group), kv_new, gain)


def _retention_kernel(q_ref, k_ref, v0_ref, v1_ref, g0_ref, g1_ref, s0_ref, dm_ref, qd_ref, kd_ref, cd_ref,
                      gain_ref, o_ref, sf_ref, s_ref):
    c = pl.program_id(1)

    @pl.when(c == 0)
    def _():
        s_ref[...] = s0_ref[...]

    for h in range(H_B):
        ks = slice(h * DK_B, (h + 1) * DK_B)
        half = slice((h % 2) * DV_B, (h % 2 + 1) * DV_B)
        v_ref, g_ref = (v0_ref, g0_ref) if h < 2 else (v1_ref, g1_ref)
        q = q_ref[:, ks]
        k = k_ref[:, ks]
        v = v_ref[:, half].astype(BF16)
        s = s_ref[h]
        att = _dot_nt(q, k.astype(BF16)) * dm_ref[h]
        o = _dot(att.astype(BF16), v) + _dot(q, s.astype(BF16)) * qd_ref[h]
        s_new = s * cd_ref[h] + _dot_tn((k * kd_ref[h]).astype(BF16), v)
        s_ref[h] = s_new
        sf_ref[h] = s_new
        mu = jnp.mean(o, axis=-1, keepdims=True)
        d = o - mu
        var = jnp.mean(d * d, axis=-1, keepdims=True)
        y = d * lax.rsqrt(var + EPS) * gain_ref[:, h * DV_B:(h + 1) * DV_B]
        g = g_ref[:, half]
        o_ref[:, h * DV_B:(h + 1) * DV_B] = (y * (g * _sigmoid(g))).astype(o_ref.dtype)


def _retention(q, k, vg, v_blk, g_blk, s0, tables, gain, b, nc, chunk):
    m = q.shape[0]
    dmask, qdec, kdec, cdec = tables
    row = lambda width, blk: pl.BlockSpec((chunk, width), lambda bi, c: (bi * nc + c, blk))
    full = lambda a: pl.BlockSpec(a.shape, lambda bi, c: (0,) * a.ndim)
    state = pl.BlockSpec((None, H_B, DK_B, DV_B), lambda bi, c: (bi, 0, 0, 0))
    return pl.pallas_call(
        _retention_kernel,
        grid=(b, nc),
        in_specs=[row(H_B * DK_B, 0), row(H_B * DK_B, 0),
                  row(2 * DV_B, v_blk), row(2 * DV_B, v_blk + 1), row(2 * DV_B, g_blk), row(2 * DV_B, g_blk + 1),
                  state, full(dmask), full(qdec), full(kdec), full(cdec), full(gain)],
        out_specs=[row(H_B * DV_B, 0), state],
        out_shape=[jax.ShapeDtypeStruct((m, H_B * DV_B), BF16),
                   jax.ShapeDtypeStruct((b, H_B, DK_B, DV_B), F32)],
        scratch_shapes=[pltpu.VMEM((H_B, DK_B, DV_B), F32)],
        compiler_params=_params("parallel", "arbitrary"),
        name="retention",
    )(q, k, vg, vg, vg, vg, s0, dmask, qdec, kdec, cdec, gain)


def _retention_tables(chunk, rows):
    log_g = jnp.log(1.0 - jnp.exp2(-5.0 - jnp.arange(H_B, dtype=F32)))
    idx = jnp.arange(chunk, dtype=F32)
    rel = idx[:, None] - idx[None, :]
    dmask = jnp.where(rel >= 0, jnp.exp(log_g[:, None, None] * jnp.maximum(rel, 0.0)), 0.0)
    q_dec = jnp.exp(log_g[:, None] * (idx[None, :] + 1.0))
    k_dec = jnp.exp(log_g[:, None] * (chunk - 1.0 - idx[None, :]))
    c_dec = jnp.exp(log_g * chunk)
    padn = rows - chunk
    dmask = jnp.pad(dmask, ((0, 0), (0, padn), (0, padn)))
    q_dec = jnp.pad(q_dec, ((0, 0), (0, padn)))
    k_dec = jnp.pad(k_dec, ((0, 0), (0, padn)))
    qd = jnp.broadcast_to(q_dec[:, :, None], (H_B, rows, DV_B))
    kd = jnp.broadcast_to(k_dec[:, :, None], (H_B, rows, DK_B))
    cd = jnp.broadcast_to(c_dec[:, None, None], (H_B, 1, DV_B))
    return dmask, qd, kd, cd


def _compress_kernel(x_ref, pe_ref, w_ref, o_ref):
    o_ref[...] = _dot((x_ref[...] + pe_ref[...]).astype(BF16), w_ref[...])


def _compress(x, pe, w, layer):
    nb = x.shape[1]
    kdim = CMP_BLOCK * DK_C
    return pl.pallas_call(
        _compress_kernel,
        grid=(2,),
        in_specs=[pl.BlockSpec((None, nb, kdim), lambda i: (i, 0, 0)),
                  pl.BlockSpec((None, 1, kdim), lambda i: (i, 0, 0)),
                  pl.BlockSpec((None, None, kdim, DK_C), lambda i: (layer, i, 0, 0))],
        out_specs=pl.BlockSpec((None, nb, DK_C), lambda i: (i, 0, 0)),
        out_shape=jax.ShapeDtypeStruct((2, nb, DK_C), F32),
        compiler_params=_params("parallel"),
        name="nsa_compress_prompt",
    )(x, pe, w)


def _cmp_attention(qs, qpos, kce, kco, vce, vco):
    scale = DK_C ** -0.5
    nh = kce.shape[0]
    n = lax.broadcasted_iota(jnp.int32, (1, nh), 1)
    me = (2 * CMP_BLOCK * n + (CMP_BLOCK - 1)) <= qpos
    mo = (2 * CMP_BLOCK * n + (2 * CMP_BLOCK - 1)) <= qpos
    se = jnp.where(me, _dot_nt(qs, kce) * scale, NEG)
    so = jnp.where(mo, _dot_nt(qs, kco) * scale, NEG)
    mx = jnp.maximum(jnp.max(se, axis=-1, keepdims=True), jnp.max(so, axis=-1, keepdims=True))
    ee = jnp.where(me, jnp.exp(se - mx), 0.0)
    eo = jnp.where(mo, jnp.exp(so - mx), 0.0)
    den = jnp.maximum(jnp.sum(ee, axis=-1, keepdims=True) + jnp.sum(eo, axis=-1, keepdims=True), TINY)
    pe = ee / den
    po = eo / den
    o = _dot(pe.astype(BF16), vce) + _dot(po.astype(BF16), vco)
    return o, pe, po


def _top_blocks(score, n_pick):
    blk = lax.broadcasted_iota(jnp.int32, score.shape, 1).astype(F32)
    big = 1e9
    work = score
    sel = jnp.zeros(score.shape, F32)
    idxs, vals = [], []
    for _ in range(n_pick):
        mval = jnp.max(work, axis=-1, keepdims=True)
        idx = jnp.min(jnp.where(work == mval, blk, big), axis=-1, keepdims=True)
        pick = blk == idx
        sel = jnp.where(pick, jnp.where(mval >= 0.0, 1.0, 0.0), sel)
        work = jnp.where(pick, -2.0, work)
        idxs.append(idx.astype(jnp.int32))
        vals.append(mval)
    return sel, idxs, vals


def _block_scores(imp, qpos):
    blk = lax.broadcasted_iota(jnp.int32, imp.shape, 1)
    cur = lax.shift_right_arithmetic(qpos, int(math.log2(SEL_BLOCK)))
    forced = jnp.where(blk == 0, 1, 0) + jnp.where(blk == cur, 1, 0) + jnp.where(blk == cur - 1, 1, 0)
    valid = blk * SEL_BLOCK <= qpos
    return jnp.where(valid, jnp.where(forced > 0, FORCED_SCORE, imp), -1.0)


def _nsa_prompt_kernel(q_ref, kce_ref, kco_ref, vce_ref, vco_ref, sw_ref, g_ref, o_ref, osel_ref, *,
                       tq, n_bucket, n_sel):
    qi = pl.program_id(1)
    scale = DK_C ** -0.5
    q = q_ref[...]
    qs = jnp.concatenate([q[:, h * DK_C:(h + 1) * DK_C] for h in range(H_C)], axis=0)
    qpos1 = qi * tq + lax.broadcasted_iota(jnp.int32, (tq, 1), 0)
    qpos = jnp.concatenate([qpos1] * H_C, axis=0)
    qrow1 = qi * tq + lax.broadcasted_iota(jnp.int32, (1, tq), 1)
    qrow = jnp.concatenate([qrow1] * H_C, axis=1)
    t = sw_ref.shape[0]
    ncp = kce_ref.shape[0]

    n = lax.broadcasted_iota(jnp.int32, (ncp, 1), 0)
    me = (2 * CMP_BLOCK * n + (CMP_BLOCK - 1)) <= qrow
    mo = (2 * CMP_BLOCK * n + (2 * CMP_BLOCK - 1)) <= qrow
    se = jnp.where(me, _dot_nt(kce_ref[...], qs) * scale, NEG)
    so = jnp.where(mo, _dot_nt(kco_ref[...], qs) * scale, NEG)
    mx = jnp.maximum(jnp.max(se, axis=0, keepdims=True), jnp.max(so, axis=0, keepdims=True))
    ee = jnp.where(me, jnp.exp(se - mx), 0.0)
    eo = jnp.where(mo, jnp.exp(so - mx), 0.0)
    den = jnp.maximum(jnp.sum(ee, axis=0, keepdims=True) + jnp.sum(eo, axis=0, keepdims=True), TINY)
    pe = ee / den
    po = eo / den
    o_cmp = _dot(pe.T.astype(BF16), vce_ref[...]) + _dot(po.T.astype(BF16), vco_ref[...])
    spe = pe[:, 0:tq] + pe[:, tq:2 * tq] + pe[:, 2 * tq:3 * tq] + pe[:, 3 * tq:4 * tq]
    spo = po[:, 0:tq] + po[:, tq:2 * tq] + po[:, 2 * tq:3 * tq] + po[:, 3 * tq:4 * tq]

    imp = (spe + spo)[0:n_sel]
    blk = lax.broadcasted_iota(jnp.int32, (n_sel, tq), 0)
    cur = lax.shift_right_arithmetic(qrow1, int(math.log2(SEL_BLOCK)))
    forced = jnp.where(blk == 0, 1, 0) + jnp.where(blk == cur, 1, 0) + jnp.where(blk == cur - 1, 1, 0)
    score = jnp.where(blk * SEL_BLOCK <= qrow1, jnp.where(forced > 0, FORCED_SCORE, imp), -1.0)
    blkf = blk.astype(F32)
    sel_t = jnp.zeros((n_sel, tq), F32)
    for _ in range(min(TOPK, n_sel)):
        mval = jnp.max(score, axis=0, keepdims=True)
        idx = jnp.min(jnp.where(score == mval, blkf, 1e9), axis=0, keepdims=True)
        pick = blkf == idx
        sel_t = jnp.where(pick, jnp.where(mval >= 0.0, 1.0, 0.0), sel_t)
        score = jnp.where(pick, -2.0, score)
    selb = jnp.concatenate([sel_t, jnp.zeros((ncp - n_sel, tq), F32)], axis=0).T.astype(BF16)

    kb = t // n_bucket
    bucket = lax.div((qi + 1) * tq - 1, kb)
    for nb in range(n_bucket):
        @pl.when(bucket == nb)
        def _(nkeys=(nb + 1) * kb):
            key_blk = lax.shift_right_arithmetic(lax.broadcasted_iota(jnp.int32, (ncp, nkeys), 1),
                                                 int(math.log2(SEL_BLOCK)))
            expand = jnp.where(key_blk == lax.broadcasted_iota(jnp.int32, (ncp, nkeys), 0), 1.0, 0.0).astype(BF16)
            kpos = lax.broadcasted_iota(jnp.int32, (1, nkeys), 1)
            selk1 = jnp.where(kpos <= qpos1, _dot(selb, expand), 0.0)
            selk = jnp.concatenate([selk1] * H_C, axis=0) > 0.5
            e, l = _visible_softmax_terms(_dot_nt(qs, sw_ref[0:nkeys, 0:128]) * scale, selk)
            osel_ref[...] = _dot(e.astype(BF16), sw_ref[0:nkeys, 128:256]) * (1.0 / l)
    o_sel = osel_ref[...]

    wlen = min(t, WINDOW + tq)
    start = pl.multiple_of(jnp.clip(qi * tq - WINDOW, 0, t - wlen), 128) if wlen < t else 0
    wpos = start + lax.broadcasted_iota(jnp.int32, (1, wlen), 1)
    d = qpos - wpos
    wmask = jnp.abs(2 * d - WINDOW) <= WINDOW
    e, l = _visible_softmax_terms(_dot_nt(qs, sw_ref[pl.ds(start, wlen), 256:384]) * scale, wmask)
    o_win = _dot(e.astype(BF16), sw_ref[pl.ds(start, wlen), 384:512]) * (1.0 / l)

    g = _sigmoid(g_ref[...])
    for h in range(H_C):
        r = slice(h * tq, (h + 1) * tq)
        oc = (g[:, h:h + 1] * o_cmp[r] + g[:, H_C + h:H_C + h + 1] * o_sel[r]
              + g[:, 2 * H_C + h:2 * H_C + h + 1] * o_win[r])
        o_ref[:, h * DK_C:(h + 1) * DK_C] = oc.astype(o_ref.dtype)


def _nsa_prompt(qc, kce, kco, vce, vco, sw, u, b, t, tq):
    m = qc.shape[0]
    nq = t // tq
    nh = kce.shape[1]
    n_sel = t // SEL_BLOCK
    assert nh == 128 and n_sel <= nh and t % SEL_BLOCK == 0 and 2 * CMP_BLOCK * n_sel >= t
    cm = pl.BlockSpec((None, nh, DK_C), lambda bi, qi: (bi, 0, 0))
    return pl.pallas_call(
        functools.partial(_nsa_prompt_kernel, tq=tq, n_bucket=_key_buckets(t, tq), n_sel=n_sel),
        grid=(b, nq),
        scratch_shapes=[pltpu.VMEM((H_C * tq, DK_C), F32)],
        in_specs=[pl.BlockSpec((tq, H_C * DK_C), lambda bi, qi: (bi * nq + qi, 0)), cm, cm, cm, cm,
                  pl.BlockSpec((t, 512), lambda bi, qi: (bi, 0)),
                  pl.BlockSpec((tq, 128), lambda bi, qi: (bi * nq + qi, COL_CG // 128))],
        out_specs=pl.BlockSpec((tq, H_C * DK_C), lambda bi, qi: (bi * nq + qi, 0)),
        out_shape=jax.ShapeDtypeStruct((m, H_C * DK_C), BF16),
        compiler_params=_params("parallel", "parallel"),
        name="nsa_prompt",
    )(qc, kce, kco, vce, vco, sw, u)


def _compress_paged_kernel(pt_ref, *rest, n_group):
    page_refs = rest[:n_group]
    pe_ref, w_ref, o_ref, xk_ref, xv_ref = rest[n_group:]
    j = pl.program_id(1)
    per_page = PAGE // CMP_BLOCK
    for g, r in enumerate(page_refs):
        for n in range(per_page):
            blk = (j * n_group + g) * per_page + n
            dst = pl.ds(pl.multiple_of(blk * CMP_PITCH, 8), CMP_BLOCK)
            xk_ref[dst, :] = r[pl.ds(n * CMP_BLOCK * 4, CMP_BLOCK, stride=4), :]
            xv_ref[dst, :] = r[pl.ds(n * CMP_BLOCK * 4 + 1, CMP_BLOCK, stride=4), :]

    @pl.when(j == pl.num_programs(1) - 1)
    def _():
        nb = xk_ref.shape[0] // CMP_PITCH
        acc_k = jnp.zeros((nb, DK_C), F32)
        acc_v = jnp.zeros((nb, DK_C), F32)
        for i in range(0, CMP_BLOCK, 2):
            def pair(x_ref, c):
                return jnp.concatenate(
                    [x_ref[pl.ds(i + d, nb, stride=CMP_PITCH), :] + pe_ref[c, i + d:i + d + 1, :] for d in (0, 1)],
                    axis=1).astype(BF16)
            wsl = slice(i * DK_C, (i + 2) * DK_C)
            acc_k = acc_k + _dot(pair(xk_ref, 0), w_ref[0, wsl, :])
            acc_v = acc_v + _dot(pair(xv_ref, 1), w_ref[1, wsl, :])
        o_ref[0] = acc_k
        o_ref[1] = acc_v


def _compress_paged(page_table, cache, layer, pe, w, n_group):
    b, n_pages = page_table.shape
    pt = page_table.reshape(-1)
    nb = n_pages * PAGE // CMP_BLOCK

    def page_spec(g):
        return pl.BlockSpec((None, None, PAGE * 4, DK_C),
                            lambda bi, j, pt_ref: (layer, pt_ref[bi * n_pages + j * n_group + g], 0, 0))

    grid_spec = pltpu.PrefetchScalarGridSpec(
        num_scalar_prefetch=1,
        grid=(b, n_pages // n_group),
        in_specs=[page_spec(g) for g in range(n_group)]
                 + [pl.BlockSpec((2, CMP_BLOCK, DK_C), lambda bi, j, pt_ref: (0, 0, 0)),
                    pl.BlockSpec((None, 2, CMP_BLOCK * DK_C, DK_C), lambda bi, j, pt_ref: (layer, 0, 0, 0))],
        out_specs=pl.BlockSpec((None, 2, nb, DK_C), lambda bi, j, pt_ref: (bi, 0, 0, 0)),
        scratch_shapes=[pltpu.VMEM((nb * CMP_PITCH, DK_C), F32), pltpu.VMEM((nb * CMP_PITCH, DK_C), F32)],
    )
    return pl.pallas_call(
        functools.partial(_compress_paged_kernel, n_group=n_group),
        grid_spec=grid_spec,
        out_shape=jax.ShapeDtypeStruct((b, 2, nb, DK_C), F32),
        compiler_params=_params("parallel", "arbitrary"),
        name="nsa_compress_paged",
    )(pt, *([cache] * n_group), pe, w)


def _nsa_sample_select_kernel(q_ref, kce_ref, kco_ref, vce_ref, vco_ref, o_ref, idx_ref, *, n_new, past_len, n_pick):
    qs = q_ref[...]
    rows = qs.shape[0]
    qpos = past_len + lax.broadcasted_iota(jnp.int32, (rows, 1), 0) % n_new
    o_cmp, pe, po = _cmp_attention(qs, qpos, kce_ref[...], kco_ref[...], vce_ref[...], vco_ref[...])
    o_ref[...] = o_cmp
    spe = pe[0:n_new]
    spo = po[0:n_new]
    for h in range(1, H_C):
        spe = spe + pe[h * n_new:(h + 1) * n_new]
        spo = spo + po[h * n_new:(h + 1) * n_new]
    score = _block_scores(spe + spo, qpos[0:n_new])
    _, idxs, _ = _top_blocks(score, n_pick)
    lane = lax.broadcasted_iota(jnp.int32, (n_new, 128), 1)
    out = jnp.zeros((n_new, 128), jnp.int32)
    for r, idx in enumerate(idxs):
        out = jnp.where(lane == r, idx, out)
    idx_ref[...] = out


def _nsa_sample_select(qs, kce, kco, vce, vco, n_new, past_len, n_pick):
    b, rows, _ = qs.shape
    nh = kce.shape[1]
    cm = pl.BlockSpec((None, nh, DK_C), lambda bi: (bi, 0, 0))
    return pl.pallas_call(
        functools.partial(_nsa_sample_select_kernel, n_new=n_new, past_len=past_len, n_pick=n_pick),
        grid=(b,),
        in_specs=[pl.BlockSpec((None, rows, DK_C), lambda bi: (bi, 0, 0)), cm, cm, cm, cm],
        out_specs=[pl.BlockSpec((None, rows, DK_C), lambda bi: (bi, 0, 0)),
                   pl.BlockSpec((None, n_new, 128), lambda bi: (bi, 0, 0))],
        out_shape=[jax.ShapeDtypeStruct((b, rows, DK_C), F32), jax.ShapeDtypeStruct((b, n_new, 128), jnp.int32)],
        compiler_params=_params("parallel"),
        name="nsa_sample_select",
    )(qs, kce, kco, vce, vco)


def _nsa_sample_attend_kernel(pt_ref, ix_ref, q_ref, *rest, n_pick, n_new, past_len):
    blk_refs = rest[:n_pick]
    new_ref, wst_ref, wnew_ref, ocmp_ref, g_ref, o_ref = rest[n_pick:]
    bi = pl.program_id(0)
    qi = pl.program_id(1)
    scale = DK_C ** -0.5
    q = q_ref[...].astype(BF16)
    qpos = past_len + qi
    lane_blk = lax.broadcasted_iota(jnp.int32, (1, SEL_BLOCK), 1)

    ks = [r[pl.ds(2, SEL_BLOCK, stride=4), :].astype(BF16) for r in blk_refs]
    vs = [r[pl.ds(3, SEL_BLOCK, stride=4), :].astype(BF16) for r in blk_refs]
    kpos = [ix_ref[(bi * n_new + qi) * TOPK + r] * SEL_BLOCK + lane_blk for r in range(n_pick)]
    new = jnp.concatenate([new_ref[...], jnp.zeros((SEL_BLOCK - n_new, 2 * DK_C), F32)], axis=0)
    ks.append(new[:, 0:DK_C].astype(BF16))
    vs.append(new[:, DK_C:2 * DK_C].astype(BF16))
    kpos.append(past_len + lane_blk)
    s = _dot_nt(q, jnp.concatenate(ks, axis=0)) * scale
    mask = jnp.concatenate(kpos, axis=1) <= qpos
    o_sel = _dot(_masked_softmax(s, mask).astype(BF16), jnp.concatenate(vs, axis=0))

    nbuf = wst_ref.shape[0]
    wnew = jnp.concatenate([wnew_ref[...], jnp.zeros((PAGE - n_new, 2 * DK_C), F32)], axis=0)
    kw = jnp.concatenate([wst_ref[:, 0:DK_C].astype(BF16), wnew[:, 0:DK_C].astype(BF16)], axis=0)
    vw = jnp.concatenate([wst_ref[:, DK_C:2 * DK_C].astype(BF16), wnew[:, DK_C:2 * DK_C].astype(BF16)], axis=0)
    wpos = past_len - nbuf + lax.broadcasted_iota(jnp.int32, (1, nbuf + PAGE), 1)
    d = qpos - wpos
    wmask = jnp.where(wpos >= 0, jnp.abs(2 * d - WINDOW), 4 * WINDOW) <= WINDOW
    s = _dot_nt(q, kw) * scale
    o_win = _dot(_masked_softmax(s, wmask).astype(BF16), vw)

    g = _sigmoid(g_ref[...])
    o_ref[...] = g[0] * ocmp_ref[...] + g[1] * o_sel + g[2] * o_win


def _nsa_sample_attend(page_table, idx, q8, cache, layer, nsa_new, win_state, win_new, ocmp8, gexp,
                       n_pick, past_len):
    b, n_pages = page_table.shape
    n_new = q8.shape[1]
    pt = page_table.reshape(-1)
    nbuf = win_state.shape[1]
    per_page = PAGE // SEL_BLOCK

    def blk_spec(r):
        def imap(bi, qi, pt_ref, ix_ref):
            blk = ix_ref[(bi * n_new + qi) * TOPK + r]
            return (layer, pt_ref[bi * n_pages + blk // per_page], blk % per_page, 0)
        return pl.BlockSpec((None, None, SEL_BLOCK * 4, DK_C), imap)

    per_q = lambda: pl.BlockSpec((None, None, 8, DK_C), lambda bi, qi, p, x: (bi, qi, 0, 0))
    grid_spec = pltpu.PrefetchScalarGridSpec(
        num_scalar_prefetch=2,
        grid=(b, n_new),
        in_specs=[per_q()] + [blk_spec(r) for r in range(n_pick)]
                 + [pl.BlockSpec((None, n_new, 2 * DK_C), lambda bi, qi, p, x: (bi, 0, 1)),
                    pl.BlockSpec((None, nbuf, 2 * DK_C), lambda bi, qi, p, x: (bi, 0, 0)),
                    pl.BlockSpec((None, n_new, 2 * DK_C), lambda bi, qi, p, x: (bi, 0, 0)),
                    per_q(),
                    pl.BlockSpec((None, None, 3, 8, DK_C), lambda bi, qi, p, x: (bi, qi, 0, 0, 0))],
        out_specs=per_q(),
    )
    return pl.pallas_call(
        functools.partial(_nsa_sample_attend_kernel, n_pick=n_pick, n_new=n_new, past_len=past_len),
        grid_spec=grid_spec,
        out_shape=jax.ShapeDtypeStruct((b, n_new, 8, DK_C), F32),
        compiler_params=_params("parallel", "parallel"),
        name="nsa_sample_attend",
    )(pt, idx, q8, *([cache] * n_pick), nsa_new, win_state, win_new, ocmp8, gexp)


def _tile(m, pref):
    if m <= pref:
        return m
    t = pref - pref % 128
    while m % t:
        t -= 128
    assert t > 0, (m, pref)
    return t


def _lambda_scalar(lp, layer):
    lam_init = 0.8 - 0.6 * math.exp(-0.3 * layer)
    lam = jnp.exp(jnp.sum(lp[0] * lp[1])) - jnp.exp(jnp.sum(lp[2] * lp[3])) + lam_init
    return lam.reshape(1).astype(F32), 1.0 - lam_init


def _dense_tail(x, xb, oa, ob, oc, p, w, layer, seq_len, conv_prev):
    m = x.shape[0]
    short = seq_len < 128
    mix = _merge(xb, w['w_mg'], oa, ob, oc, w['wa'], w['wb'], w['wc'], layer, _tile(m, 512), 512)
    x1, x1b = _proj_ln(mix, w['w_out'], layer, x, w['g0'], w['b0'], _tile(m, 512))
    if short:
        p1, p2 = conv_prev
        h, a = _ffn_up_short(x1b, w['w_fg'], w['w_fu'], layer, w['cw'], w['cb'], p1, p2, seq_len, 512)
        conv = a.reshape(m // seq_len, seq_len, D_FF_PAD)[:, seq_len - (CONV_W - 1):, :D_FF]
    else:
        h, st = _ffn_up_seq(x1b, w['w_fg'], w['w_fu'], layer, w['cw'], w['cb'], seq_len, _tile(seq_len, 2048), 512)
        conv = st[:, 8 - (CONV_W - 1):, :D_FF]
    x2, x2b = _ffn_down_ln(h, w['w_fd'], layer, x1, w['g1'], w['b1'], _tile(m, 512), D_FF_PAD // 4)
    x3, x3b = _ple_ln(x2b, w['w_pg'], p, w['w_pp'], layer, x2, w['g2'], w['b2'], _tile(m, 512))
    return x3, x3b, conv


def _prompt_layer(x, xb, p, w, layer, b, t, rope_tabs, ret_tabs):
    m = b * t
    u = _matmul(xb, w['w_in'], layer, F32, _tile(m, 1024), 1024, "in_proj")
    (qa, new_diff, kva, qb, kb, qc, new_nsa, sw, new_win, kcm, vcm) = _rope_split(u, rope_tabs, _tile(t, 256))

    lam, post = _lambda_scalar(w['diff_lambda'], layer)
    oa = _diff_attn_prompt(lam, qa, kva, w['diff_gain'], b, t, _tile(t, 256), post)

    chunk = RET_CHUNK
    s0 = jnp.zeros((b, H_B, DK_B, DV_B), F32)
    ob, ret = _retention(qb, kb, u, COL_BV // 512, COL_BG // 512, s0, ret_tabs, w['ret_gain'], b, t // chunk, chunk)

    nb = m // CMP_BLOCK
    xc = jnp.stack([kcm.reshape(nb, CMP_BLOCK * DK_C), vcm.reshape(nb, CMP_BLOCK * DK_C)])
    kvc = _compress(xc, w['cmp_pe'].reshape(2, 1, CMP_BLOCK * DK_C), w['cmp_w'], layer)
    kvc = kvc.reshape(2, b, t // CMP_BLOCK, DK_C).astype(BF16)
    half = lambda a: jnp.pad(a, ((0, 0), (0, 128 - a.shape[1]), (0, 0)))
    oc = _nsa_prompt(qc, half(kvc[0, :, 0::2]), half(kvc[0, :, 1::2]), half(kvc[1, :, 0::2]),
                     half(kvc[1, :, 1::2]), sw, u, b, t, _tile(t, 128))

    x3, x3b, conv = _dense_tail(x, xb, oa, ob, oc, p, w, layer, t, None)
    nwin = min(WINDOW, t)
    states = (new_diff.reshape(b, t, 2, H_A, DV_A), new_nsa.reshape(b, t, 4, DK_C),
              new_win.reshape(b, t, 2, DK_C)[:, t - nwin:], ret, conv)
    return x3, x3b, states


def _sample_layer(x, xb, p, w, layer, b, t, past_len, rope_tabs, ret_tabs, page_table,
                  cache_diff, cache_nsa, win_state, ret_state, conv_state):
    m = b * t
    u = _matmul(xb, w['w_in'], layer, F32, m, 1024, "in_proj")
    (qa, new_diff, kva, qb, kb, qc, new_nsa, sw, new_win, kcm, vcm) = _rope_split(u, rope_tabs, m)

    lam, post = _lambda_scalar(w['diff_lambda'], layer)
    q5 = qa.reshape(b, t, H_A, 2, DK_A).transpose(0, 2, 3, 1, 4)
    eye = jnp.eye(H_A * 2, dtype=BF16).reshape(H_A, 2, 1, H_A, 2, 1)
    q_exp = (q5[:, :, :, :, None, None, :] * eye[None]).reshape(b, H_A * 2 * t, H_A * 2 * DK_A)
    n_phys = cache_diff.shape[1]
    oa = _diff_attn_paged(page_table, lam, q_exp, cache_diff.reshape(-1, n_phys, PAGE * 2 * H_A, DV_A),
                          layer, new_diff.reshape(b, t, 2 * H_A * DV_A), w['diff_gain'], 16, post)
    oa = oa.reshape(m, H_A * DV_A).astype(BF16)

    rows = RET_CHUNK
    padr = lambda a: jnp.pad(a.reshape(b, t, a.shape[-1]), ((0, 0), (0, rows - t), (0, 0))).reshape(b * rows, -1)
    ob, ret = _retention(padr(qb), padr(kb), padr(u[:, COL_BV:COL_CQ]), 0, 2, ret_state, ret_tabs,
                         w['ret_gain'], b, 1, rows)
    ob = ob.reshape(b, rows, H_B * DV_B)[:, :t].reshape(m, H_B * DV_B)

    n_pages = page_table.shape[1]
    cache_rows = cache_nsa.reshape(-1, n_phys, PAGE * 4, DK_C)
    kvc = _compress_paged(page_table, cache_rows, layer, w['cmp_pe'], w['cmp_w'], 16).astype(BF16)
    qs = qc.reshape(b, t, H_C, DK_C).transpose(0, 2, 1, 3).reshape(b, H_C * t, DK_C)
    n_pick = TOPK - 1
    assert past_len % SEL_BLOCK == 0 and t <= SEL_BLOCK and past_len // SEL_BLOCK >= n_pick
    ocmp, idx = _nsa_sample_select(qs, kvc[:, 0, 0::2], kvc[:, 0, 1::2], kvc[:, 1, 0::2], kvc[:, 1, 1::2],
                                   t, past_len, n_pick)
    pad8 = lambda a: jnp.pad(a, ((0, 0),) * (a.ndim - 2) + ((0, 8 - H_C), (0, 0)))
    q8 = pad8(qc.astype(F32).reshape(b, t, H_C, DK_C))
    ocmp8 = pad8(ocmp.reshape(b, H_C, t, DK_C).transpose(0, 2, 1, 3))
    cg = u[:, COL_CG:COL_CG + 3 * H_C].reshape(b, t, 3, H_C)
    gexp = jnp.broadcast_to(pad8(cg[..., None]), (b, t, 3, 8, DK_C))
    oc = _nsa_sample_attend(page_table, idx[:, :, :TOPK].reshape(-1), q8, cache_rows, layer,
                            new_nsa.reshape(b, t, 4 * DK_C), win_state.reshape(b, -1, 2 * DK_C),
                            new_win.reshape(b, t, 2 * DK_C), ocmp8, gexp, n_pick, past_len)
    oc = oc[:, :, :H_C].reshape(m, H_C * DK_C).astype(BF16)

    cs = jnp.pad(conv_state, ((0, 0), (0, 0), (0, D_FF_PAD - D_FF)))
    zeros = jnp.zeros((b, t - 1, D_FF_PAD), F32)
    p1 = jnp.concatenate([cs[:, 1:2], zeros], axis=1).reshape(m, D_FF_PAD)
    p2 = jnp.concatenate([cs, zeros[:, 1:]], axis=1).reshape(m, D_FF_PAD)

    x3, x3b, conv = _dense_tail(x, xb, oa, ob, oc, p, w, layer, t, (p1, p2))
    nbuf = win_state.shape[1]
    win = jnp.concatenate([win_state, new_win.reshape(b, t, 2, DK_C)], axis=1)[:, t:]
    assert win.shape[1] == nbuf
    states = (new_diff.reshape(b, t, 2, H_A, DV_A), new_nsa.reshape(b, t, 4, DK_C), win, ret, conv)
    return x3, x3b, states


def _layer_weights(i, ln_gain, ln_bias, w_in, diff_lambda, diff_norm_gain, ret_norm_gain, nsa_cmp_pos, nsa_cmp_w,
                   w_branch_a, w_branch_b, w_branch_c, w_merge_gate, w_out, w_ffn_gate, w_ffn_up, ffn_conv_w,
                   ffn_conv_b, w_ffn_down, w_ple_gate, w_ple_proj):
    bf = lambda a: a.astype(BF16)
    padc = lambda a, n: jnp.pad(a, ((0, 0),) * (a.ndim - 1) + ((0, n - a.shape[-1]),))
    return {
        'w_in': bf(padc(w_in, N_IN_PAD)),
        'diff_lambda': diff_lambda[i].astype(F32),
        'diff_gain': diff_norm_gain[i].reshape(1, DV_A),
        'ret_gain': ret_norm_gain[i].reshape(1, H_B * DV_B),
        'cmp_pe': nsa_cmp_pos[i],
        'cmp_w': bf(nsa_cmp_w),
        'wa': bf(w_branch_a), 'wb': bf(w_branch_b), 'wc': bf(w_branch_c),
        'w_mg': bf(w_merge_gate), 'w_out': bf(w_out),
        'w_fg': bf(padc(w_ffn_gate, D_FF_PAD)), 'w_fu': bf(padc(w_ffn_up, D_FF_PAD)),
        'cw': padc(ffn_conv_w[i], D_FF_PAD), 'cb': padc(ffn_conv_b[i].reshape(1, D_FF), D_FF_PAD),
        'w_fd': bf(jnp.pad(w_ffn_down, ((0, 0), (0, D_FF_PAD - D_FF), (0, 0)))),
        'w_pg': bf(w_ple_gate), 'w_pp': bf(w_ple_proj),
        'g0': ln_gain[i, 0:1], 'g1': ln_gain[i, 1:2], 'g2': ln_gain[i, 2:3],
        'b0': ln_bias[i, 0:1], 'b1': ln_bias[i, 1:2], 'b2': ln_bias[i, 2:3],
    }


def kernel(x_prompt, x_sample, cache_diff_kv, cache_nsa_kv, state_nsa_win, state_ret, state_conv, page_table,
           p_prompt, p_sample, ln_gain, ln_bias, w_in, diff_lambda, diff_norm_gain, ret_norm_gain, nsa_cmp_pos,
           nsa_cmp_w, w_branch_a, w_branch_b, w_branch_c, w_merge_gate, w_out, w_ffn_gate, w_ffn_up, ffn_conv_w,
           ffn_conv_b, w_ffn_down, w_ple_gate, w_ple_proj):
    bp, tp, _ = x_prompt.shape
    bs, ts, _ = x_sample.shape
    n_layers = w_in.shape[0]
    past_len = page_table.shape[1] * PAGE

    rope_p = _rope_tables(jnp.arange(tp, dtype=jnp.int32))
    rope_s = tuple(jnp.tile(tb, (bs, 1)) for tb in _rope_tables(past_len + jnp.arange(ts, dtype=jnp.int32)))
    ret_p = _retention_tables(RET_CHUNK, RET_CHUNK)
    ret_s = _retention_tables(ts, RET_CHUNK)

    xp = x_prompt.reshape(bp * tp, D_MODEL)
    xs = x_sample.reshape(bs * ts, D_MODEL)
    xpb, xsb = xp.astype(BF16), xs.astype(BF16)
    st_p, st_s = [], []
    for i in range(n_layers):
        w = _layer_weights(i, ln_gain, ln_bias, w_in, diff_lambda, diff_norm_gain, ret_norm_gain, nsa_cmp_pos,
                           nsa_cmp_w, w_branch_a, w_branch_b, w_branch_c, w_merge_gate, w_out, w_ffn_gate,
                           w_ffn_up, ffn_conv_w, ffn_conv_b, w_ffn_down, w_ple_gate, w_ple_proj)
        xp, xpb, sp = _prompt_layer(xp, xpb, p_prompt[i].reshape(bp * tp, PLE_DIM), w, i, bp, tp, rope_p, ret_p)
        xs, xsb, ss = _sample_layer(xs, xsb, p_sample[i].reshape(bs * ts, PLE_DIM), w, i, bs, ts, past_len,
                                    rope_s, ret_s, page_table, cache_diff_kv, cache_nsa_kv, state_nsa_win[i],
                                    state_ret[i], state_conv[i])
        st_p.append(sp)
        st_s.append(ss)
    diff_p, nsa_p, win_p, ret_p_out, conv_p = [jnp.stack(s) for s in zip(*st_p)]
    diff_s, nsa_s, win_s, ret_s_out, conv_s = [jnp.stack(s) for s in zip(*st_s)]
    return (xp.reshape(bp, tp, D_MODEL), xs.reshape(bs, ts, D_MODEL), diff_p, diff_s, nsa_p, nsa_s,
            win_p, win_s, ret_p_out, ret_s_out, conv_p, conv_s)
```

```python
import functools
import math

import jax
import jax.numpy as jnp
from jax import lax
from jax.experimental import pallas as pl
from jax.experimental.pallas import tpu as pltpu

F32 = jnp.float32
BF16 = jnp.bfloat16

D_MODEL = 2048
H_A, DK_A, DV_A = 4, 64, 128
H_B, DK_B, DV_B = 4, 128, 256
RET_CHUNK = 128
RET_THETA = 10000.0
H_C, DK_C = 4, 128
CMP_BLOCK, SEL_BLOCK, TOPK, WINDOW = 32, 64, 16, 512
FORCED_SCORE = H_C + 1.0
ROPE_THETA = 500000.0
D_FF = 5504
CONV_W = 3
PLE_DIM = 256
N_LAYERS = 4
ALPHA = (2 * N_LAYERS) ** 0.25
EPS = 1e-5
PAGE = 128

N_IN = 5900
N_IN_PAD = 6144
COL_AQ, COL_AK, COL_AV = 0, 512, 1024
COL_BQ, COL_BK, COL_BV, COL_BG = 1536, 2048, 2560, 3584
COL_CQ, COL_CKV, COL_CWIN, COL_CG = 4608, 5120, 5632, 5888
D_FF_PAD = 5632
CMP_PITCH = 40

VMEM_LIMIT = 56 * 1024 * 1024

NEG = float(jnp.finfo(jnp.float32).min)
TINY = float(jnp.finfo(jnp.float32).tiny)


def _params(*sem):
    return pltpu.CompilerParams(dimension_semantics=sem, vmem_limit_bytes=VMEM_LIMIT)


def _dot(a, b):
    return jnp.dot(a, b, preferred_element_type=F32)


def _dot_nt(a, b):
    return lax.dot_general(a, b, (((1,), (1,)), ((), ())), preferred_element_type=F32)


def _dot_tn(a, b):
    return lax.dot_general(a, b, (((0,), (0,)), ((), ())), preferred_element_type=F32)


def _masked_softmax(s, mask):
    s = jnp.where(mask, s, NEG)
    e = jnp.where(mask, jnp.exp(s - jnp.max(s, axis=-1, keepdims=True)), 0.0)
    return e / jnp.maximum(jnp.sum(e, axis=-1, keepdims=True), TINY)


def _visible_softmax_terms(s, mask):
    s = jnp.where(mask, s, NEG)
    e = jnp.exp(s - jnp.max(s, axis=-1, keepdims=True))
    return e, jnp.sum(e, axis=-1, keepdims=True)


def _layer_norm(y, g, b):
    mu = jnp.mean(y, axis=-1, keepdims=True)
    d = y - mu
    var = jnp.mean(d * d, axis=-1, keepdims=True)
    return d * lax.rsqrt(var + EPS) * g + b


def _sigmoid(x):
    return 1.0 / (1.0 + jnp.exp(-x))


def _mm_kernel(x_ref, w_ref, o_ref):
    o_ref[...] = _dot(x_ref[...], w_ref[...]).astype(o_ref.dtype)


def _matmul(x, w, layer, out_dtype, tm, tn, name):
    m, k = x.shape
    n = w.shape[2]
    return pl.pallas_call(
        _mm_kernel,
        grid=(n // tn, m // tm),
        in_specs=[pl.BlockSpec((tm, k), lambda j, i: (i, 0)),
                  pl.BlockSpec((None, k, tn), lambda j, i: (layer, 0, j))],
        out_specs=pl.BlockSpec((tm, tn), lambda j, i: (i, j)),
        out_shape=jax.ShapeDtypeStruct((m, n), out_dtype),
        compiler_params=_params("parallel", "parallel"),
        name=name,
    )(x, w)


def _merge_kernel(x_ref, w0_ref, w1_ref, w2_ref, oa_ref, ob_ref, oc_ref, wa_ref, wb_ref, wc_ref, o_ref):
    x = x_ref[...]
    m = _sigmoid(_dot(x, w0_ref[...])) * _dot(oa_ref[...], wa_ref[...])
    m = m + _sigmoid(_dot(x, w1_ref[...])) * _dot(ob_ref[...], wb_ref[...])
    m = m + _sigmoid(_dot(x, w2_ref[...])) * _dot(oc_ref[...], wc_ref[...])
    o_ref[...] = m.astype(o_ref.dtype)


def _merge(xb, w_mg, oa, ob, oc, wa, wb, wc, layer, tm, tn):
    m = xb.shape[0]
    nj = D_MODEL // tn
    row = lambda width: pl.BlockSpec((tm, width), lambda j, i: (i, 0))
    col = lambda k, off: pl.BlockSpec((None, k, tn), lambda j, i: (layer, 0, j + off))
    return pl.pallas_call(
        _merge_kernel,
        grid=(nj, m // tm),
        in_specs=[row(D_MODEL), col(D_MODEL, 0), col(D_MODEL, nj), col(D_MODEL, 2 * nj),
                  row(H_A * DV_A), row(H_B * DV_B), row(H_C * DK_C),
                  col(H_A * DV_A, 0), col(H_B * DV_B, 0), col(H_C * DK_C, 0)],
        out_specs=pl.BlockSpec((tm, tn), lambda j, i: (i, j)),
        out_shape=jax.ShapeDtypeStruct((m, D_MODEL), BF16),
        compiler_params=_params("parallel", "parallel"),
        name="merge_gate",
    )(xb, w_mg, w_mg, w_mg, oa, ob, oc, wa, wb, wc)


def _proj_ln_kernel(m_ref, w_ref, x_ref, g_ref, b_ref, o_ref, ob_ref):
    y = ALPHA * x_ref[...] + _dot(m_ref[...], w_ref[...])
    out = _layer_norm(y, g_ref[...], b_ref[...])
    o_ref[...] = out
    ob_ref[...] = out.astype(BF16)


def _resident_weight(k, n, layer):
    return pl.BlockSpec((None, k, n), lambda i: (layer, 0, 0), pipeline_mode=pl.Buffered(1))


def _proj_ln(mb, w, layer, x, g, b, tm):
    m = x.shape[0]
    k = mb.shape[1]
    row = lambda width: pl.BlockSpec((tm, width), lambda i: (i, 0))
    full = lambda r, c: pl.BlockSpec((r, c), lambda i: (0, 0))
    return pl.pallas_call(
        _proj_ln_kernel,
        grid=(m // tm,),
        in_specs=[row(k), _resident_weight(k, D_MODEL, layer), row(D_MODEL), full(1, D_MODEL), full(1, D_MODEL)],
        out_specs=[row(D_MODEL), row(D_MODEL)],
        out_shape=[jax.ShapeDtypeStruct((m, D_MODEL), F32), jax.ShapeDtypeStruct((m, D_MODEL), BF16)],
        compiler_params=_params("parallel"),
        name="out_proj_ln",
    )(mb, w, x, g, b)


def _ple_ln_kernel(xb_ref, wg_ref, p_ref, wp_ref, x_ref, g_ref, b_ref, o_ref, ob_ref):
    pe = _sigmoid(_dot(xb_ref[...], wg_ref[...])) * _dot(p_ref[...].astype(BF16), wp_ref[...])
    out = _layer_norm(ALPHA * x_ref[...] + pe, g_ref[...], b_ref[...])
    o_ref[...] = out
    ob_ref[...] = out.astype(BF16)


def _ple_ln(xb, wg, p, wp, layer, x, g, b, tm):
    m = x.shape[0]
    row = lambda width: pl.BlockSpec((tm, width), lambda i: (i, 0))
    full = lambda r, c: pl.BlockSpec((r, c), lambda i: (0, 0))
    return pl.pallas_call(
        _ple_ln_kernel,
        grid=(m // tm,),
        in_specs=[row(D_MODEL), _resident_weight(D_MODEL, D_MODEL, layer), row(PLE_DIM),
                  _resident_weight(PLE_DIM, D_MODEL, layer), row(D_MODEL), full(1, D_MODEL), full(1, D_MODEL)],
        out_specs=[row(D_MODEL), row(D_MODEL)],
        out_shape=[jax.ShapeDtypeStruct((m, D_MODEL), F32), jax.ShapeDtypeStruct((m, D_MODEL), BF16)],
        compiler_params=_params("parallel"),
        name="ple_ln",
    )(xb, wg, p, wp, x, g, b)


def _conv_silu(a, a1, a2, u, cw_ref, cb_ref):
    ac = cb_ref[...] + ((a2 * cw_ref[0:1, :] + a1 * cw_ref[1:2, :]) + a * cw_ref[2:3, :])
    return (ac * _sigmoid(ac)) * u


def _ffn_up_seq_kernel(x_ref, wg_ref, wu_ref, cw_ref, cb_ref, h_ref, st_ref, carry_ref, *, tiles_per_seq):
    i = pl.program_id(1)
    x = x_ref[...]
    a = _dot(x, wg_ref[...])
    u = _dot(x, wu_ref[...])
    tm = a.shape[0]

    @pl.when(i % tiles_per_seq == 0)
    def _():
        carry_ref[...] = jnp.zeros_like(carry_ref)

    prev = carry_ref[...]
    row = lax.broadcasted_iota(jnp.int32, a.shape, 0)
    a1 = jnp.where(row == 0, prev[7:8, :], pltpu.roll(a, 1, 0))
    a2 = jnp.where(row == 0, prev[6:7, :], jnp.where(row == 1, prev[7:8, :], pltpu.roll(a, 2, 0)))
    h_ref[...] = _conv_silu(a, a1, a2, u, cw_ref, cb_ref).astype(h_ref.dtype)
    tail = a[tm - 8:tm, :]
    carry_ref[...] = tail
    st_ref[...] = tail


def _ffn_up_seq(xb, wg, wu, layer, cw, cb, seq_len, tm, tn):
    m = xb.shape[0]
    tps = seq_len // tm
    return pl.pallas_call(
        functools.partial(_ffn_up_seq_kernel, tiles_per_seq=tps),
        grid=(D_FF_PAD // tn, m // tm),
        in_specs=[pl.BlockSpec((tm, D_MODEL), lambda j, i: (i, 0)),
                  pl.BlockSpec((None, D_MODEL, tn), lambda j, i: (layer, 0, j)),
                  pl.BlockSpec((None, D_MODEL, tn), lambda j, i: (layer, 0, j)),
                  pl.BlockSpec((CONV_W, tn), lambda j, i: (0, j)),
                  pl.BlockSpec((1, tn), lambda j, i: (0, j))],
        out_specs=[pl.BlockSpec((tm, tn), lambda j, i: (i, j)),
                   pl.BlockSpec((None, 8, tn), lambda j, i: (i // tps, 0, j))],
        out_shape=[jax.ShapeDtypeStruct((m, D_FF_PAD), BF16),
                   jax.ShapeDtypeStruct((m // seq_len, 8, D_FF_PAD), F32)],
        scratch_shapes=[pltpu.VMEM((8, tn), F32)],
        compiler_params=_params("arbitrary", "arbitrary"),
        name="ffn_up_conv_prompt",
    )(xb, wg, wu, cw, cb)


def _ffn_up_short_kernel(x_ref, wg_ref, wu_ref, cw_ref, cb_ref, p1_ref, p2_ref, h_ref, a_ref, *, seq_len):
    x = x_ref[...]
    a = _dot(x, wg_ref[...])
    u = _dot(x, wu_ref[...])
    t = lax.broadcasted_iota(jnp.int32, a.shape, 0) % seq_len
    a1 = jnp.where(t == 0, p1_ref[...], pltpu.roll(a, 1, 0))
    a2 = jnp.where(t < 2, p2_ref[...], pltpu.roll(a, 2, 0))
    h_ref[...] = _conv_silu(a, a1, a2, u, cw_ref, cb_ref).astype(h_ref.dtype)
    a_ref[...] = a


def _ffn_up_short(xb, wg, wu, layer, cw, cb, p1, p2, seq_len, tn):
    m = xb.shape[0]
    colb = lambda r: pl.BlockSpec((r, tn), lambda j: (0, j))
    wcol = pl.BlockSpec((None, D_MODEL, tn), lambda j: (layer, 0, j))
    return pl.pallas_call(
        functools.partial(_ffn_up_short_kernel, seq_len=seq_len),
        grid=(D_FF_PAD // tn,),
        in_specs=[pl.BlockSpec((m, D_MODEL), lambda j: (0, 0)), wcol, wcol,
                  colb(CONV_W), colb(1), colb(m), colb(m)],
        out_specs=[colb(m), colb(m)],
        out_shape=[jax.ShapeDtypeStruct((m, D_FF_PAD), BF16), jax.ShapeDtypeStruct((m, D_FF_PAD), F32)],
        compiler_params=_params("parallel"),
        name="ffn_up_conv_sample",
    )(xb, wg, wu, cw, cb, p1, p2)


def _ffn_down_ln_kernel(h_ref, w_ref, x_ref, g_ref, b_ref, o_ref, ob_ref, acc_ref):
    k = pl.program_id(1)

    @pl.when(k == 0)
    def _():
        acc_ref[...] = jnp.zeros_like(acc_ref)

    acc_ref[...] += _dot(h_ref[...], w_ref[...])

    @pl.when(k == pl.num_programs(1) - 1)
    def _():
        out = _layer_norm(ALPHA * x_ref[...] + acc_ref[...], g_ref[...], b_ref[...])
        o_ref[...] = out
        ob_ref[...] = out.astype(BF16)


def _ffn_down_ln(h, w, layer, x, g, b, tm, tk):
    m = x.shape[0]
    row = pl.BlockSpec((tm, D_MODEL), lambda i, k: (i, 0))
    vec = pl.BlockSpec((1, D_MODEL), lambda i, k: (0, 0))
    return pl.pallas_call(
        _ffn_down_ln_kernel,
        grid=(m // tm, D_FF_PAD // tk),
        in_specs=[pl.BlockSpec((tm, tk), lambda i, k: (i, k)),
                  pl.BlockSpec((None, tk, D_MODEL), lambda i, k: (layer, k, 0)), row, vec, vec],
        out_specs=[row, row],
        out_shape=[jax.ShapeDtypeStruct((m, D_MODEL), F32), jax.ShapeDtypeStruct((m, D_MODEL), BF16)],
        scratch_shapes=[pltpu.VMEM((tm, D_MODEL), F32)],
        compiler_params=_params("parallel", "arbitrary"),
        name="ffn_down_ln",
    )(h, w, x, g, b)


def _rope128(x, c, s_up, s_dn, shift):
    y = x * c + pltpu.roll(x, 128 - shift, 1) * s_up
    if s_dn is not None:
        y = y + pltpu.roll(x, shift, 1) * s_dn
    return y


def _rope_split_kernel(aq_ref, ak_ref, av_ref, bq_ref, bk_ref, cq_ref, ckv_ref, cw_ref,
                       ca_ref, sau_ref, sad_ref, cb_ref, sb_ref, cc_ref, scu_ref, scd_ref,
                       qa_ref, nd_ref, kva_ref, qb_ref, kb_ref, qc_ref, nn_ref, sw_ref, nw_ref, kcm_ref, vcm_ref):
    ca, sau, sad = ca_ref[...], sau_ref[...], sad_ref[...]
    cb, sb = cb_ref[...], sb_ref[...]
    cc, scu, scd = cc_ref[...], scu_ref[...], scd_ref[...]
    rot_a = DK_A // 8
    rot_c = DK_C // 8
    tm = aq_ref.shape[0]
    for h in range(4):
        sl = slice(h * 128, (h + 1) * 128)
        qa_ref[:, sl] = _rope128(aq_ref[:, sl], ca, sau, sad, rot_a).astype(BF16)
        ka = _rope128(ak_ref[:, sl], ca, sau, sad, rot_a)
        av = av_ref[:, sl]
        nd_ref[pl.ds(h, tm, stride=2 * H_A), :] = ka
        nd_ref[pl.ds(H_A + h, tm, stride=2 * H_A), :] = av
        kva_ref[:, sl] = ka.astype(BF16)
        kva_ref[:, 512 + h * 128:512 + (h + 1) * 128] = av.astype(BF16)
        qb_ref[:, sl] = _rope128(bq_ref[:, sl], cb, sb, None, DK_B // 2).astype(BF16)
        kb_ref[:, sl] = _rope128(bk_ref[:, sl], cb, sb, None, DK_B // 2) * (DK_B ** -0.5)
        qc_ref[:, sl] = _rope128(cq_ref[:, sl], cc, scu, scd, rot_c).astype(BF16)
    k_cmp = _rope128(ckv_ref[:, 0:128], cc, scu, scd, rot_c)
    v_cmp = ckv_ref[:, 128:256]
    k_sel = _rope128(ckv_ref[:, 256:384], cc, scu, scd, rot_c)
    v_sel = ckv_ref[:, 384:512]
    k_win = _rope128(cw_ref[:, 0:128], cc, scu, scd, rot_c)
    v_win = cw_ref[:, 128:256]
    for comp, val in enumerate((k_cmp, v_cmp, k_sel, v_sel)):
        nn_ref[pl.ds(comp, tm, stride=4), :] = val
    sw_ref[:, 0:128] = k_sel.astype(BF16)
    sw_ref[:, 128:256] = v_sel.astype(BF16)
    sw_ref[:, 256:384] = k_win.astype(BF16)
    sw_ref[:, 384:512] = v_win.astype(BF16)
    nw_ref[:, 0:128] = k_win
    nw_ref[:, 128:256] = v_win
    kcm_ref[...] = k_cmp
    vcm_ref[...] = v_cmp


def _rope_split(u, tables, tm):
    m = u.shape[0]
    n_tab = tables[0].shape[0] // tm
    ub = lambda width, blk: pl.BlockSpec((tm, width), lambda i: (i, blk))
    tab = pl.BlockSpec((tm, 128), lambda i: (i % n_tab, 0))
    out = lambda width: pl.BlockSpec((tm, width), lambda i: (i, 0))
    shp = lambda width, dt: jax.ShapeDtypeStruct((m, width), dt)
    return pl.pallas_call(
        _rope_split_kernel,
        grid=(m // tm,),
        in_specs=[ub(512, COL_AQ // 512), ub(512, COL_AK // 512), ub(512, COL_AV // 512),
                  ub(512, COL_BQ // 512), ub(512, COL_BK // 512), ub(512, COL_CQ // 512),
                  ub(512, COL_CKV // 512), ub(256, COL_CWIN // 256)] + [tab] * 8,
        out_specs=[out(512), pl.BlockSpec((tm * 2 * H_A, DV_A), lambda i: (i, 0)), out(1024), out(512), out(512),
                   out(512), pl.BlockSpec((tm * 4, DK_C), lambda i: (i, 0)), out(512), out(256), out(128), out(128)],
        out_shape=[shp(512, BF16), jax.ShapeDtypeStruct((m * 2 * H_A, DV_A), F32), shp(1024, BF16), shp(512, BF16),
                   shp(512, F32), shp(512, BF16), jax.ShapeDtypeStruct((m * 4, DK_C), F32), shp(512, BF16),
                   shp(256, F32), shp(128, F32), shp(128, F32)],
        compiler_params=_params("parallel"),
        name="rope_split",
    )(u, u, u, u, u, u, u, u, *tables)


def _rope_tables(pos):
    posf = pos.astype(F32)[:, None]
    lane = jnp.arange(128)

    def cs(half, theta):
        inv_freq = jnp.exp(-math.log(theta) * jnp.arange(half, dtype=F32) / half)
        ang = posf * inv_freq[None, :]
        return jnp.cos(ang), jnp.sin(ang)

    def partial_tables(head_dim, half, theta):
        cos, sin = cs(half, theta)
        d = lane % head_dim
        lo = d < half
        hi = (d >= half) & (d < 2 * half)
        idx = jnp.where(lo, d, jnp.where(hi, d - half, 0))
        c = jnp.where((lo | hi)[None, :], cos[:, idx], 1.0)
        s_up = jnp.where(lo[None, :], -sin[:, idx], 0.0)
        s_dn = jnp.where(hi[None, :], sin[:, idx], 0.0)
        return c, s_up, s_dn

    ca, sau, sad = partial_tables(DK_A, DK_A // 8, ROPE_THETA)
    cc, scu, scd = partial_tables(DK_C, DK_C // 8, ROPE_THETA)
    cosb, sinb = cs(DK_B // 2, RET_THETA)
    cb = jnp.concatenate([cosb, cosb], axis=1)
    sb = jnp.concatenate([-sinb, sinb], axis=1)
    return (ca, sau, sad, cb, sb, cc, scu, scd)


def _rms_head(o, gain, post_scale):
    return o * lax.rsqrt(jnp.mean(o * o, axis=-1, keepdims=True) + EPS) * gain * post_scale


def _diff_attn_prompt_kernel(lam_ref, q_ref, k_ref, v_ref, gain_ref, o_ref, *, tq, post_scale):
    qi = pl.program_id(2)
    t = k_ref.shape[0]
    n_head = q_ref.shape[1] // 128
    lane = lax.broadcasted_iota(jnp.int32, (tq, 128), 1)
    scale = DK_A ** -0.5
    diag = lax.broadcasted_iota(jnp.int32, (tq, tq), 1) <= lax.broadcasted_iota(jnp.int32, (tq, tq), 0)

    def tile_terms(qh, hs, lo):
        s_d = jnp.where(diag, _dot_nt(qh, k_ref[lo:lo + tq, hs]), NEG)
        mx = jnp.max(s_d, axis=-1, keepdims=True)
        if lo:
            s_lo = _dot_nt(qh, k_ref[0:lo, hs])
            mx = jnp.maximum(mx, jnp.max(s_lo, axis=-1, keepdims=True))
        e_d = jnp.exp(s_d - mx)
        l = jnp.sum(e_d, axis=-1, keepdims=True)
        o = _dot(e_d.astype(BF16), v_ref[lo:lo + tq, hs])
        if lo:
            e_lo = jnp.exp(s_lo - mx)
            l = l + jnp.sum(e_lo, axis=-1, keepdims=True)
            o = o + _dot(e_lo.astype(BF16), v_ref[0:lo, hs])
        return o, l

    for qv in range(t // tq):
        @pl.when(qi == qv)
        def _(lo=qv * tq):
            for hh in range(n_head):
                hs = slice(hh * 128, (hh + 1) * 128)
                q = q_ref[:, hs]
                zero = jnp.zeros_like(q)
                o1, l1 = tile_terms(jnp.where(lane < DK_A, q, zero) * scale, hs, lo)
                o2, l2 = tile_terms(jnp.where(lane >= DK_A, q, zero) * scale, hs, lo)
                o = o1 * (1.0 / l1) - (lam_ref[0] / l2) * o2
                o_ref[:, hs] = _rms_head(o, gain_ref[...], post_scale).astype(o_ref.dtype)


def _key_buckets(t, tq):
    n_bucket = 8
    return n_bucket if t % (n_bucket * 128) == 0 and (t // n_bucket) % tq == 0 else 1


def _diff_attn_prompt(lam, qa, kva, gain, b, t, tq, post_scale):
    m = qa.shape[0]
    nq = t // tq
    return pl.pallas_call(
        functools.partial(_diff_attn_prompt_kernel, tq=tq, post_scale=post_scale),
        grid=(b, H_A // 2, nq),
        in_specs=[pl.BlockSpec(memory_space=pltpu.SMEM),
                  pl.BlockSpec((tq, 256), lambda bi, h, qi: (bi * nq + qi, h)),
                  pl.BlockSpec((t, 256), lambda bi, h, qi: (bi, h)),
                  pl.BlockSpec((t, 256), lambda bi, h, qi: (bi, H_A // 2 + h)),
                  pl.BlockSpec((1, 128), lambda bi, h, qi: (0, 0))],
        out_specs=pl.BlockSpec((tq, 256), lambda bi, h, qi: (bi * nq + qi, h)),
        out_shape=jax.ShapeDtypeStruct((m, H_A * DV_A), BF16),
        compiler_params=_params("parallel", "parallel", "parallel"),
        name="diff_attn_prompt",
    )(lam, qa, kva, kva, gain)


def _diff_attn_paged_kernel(pt_ref, lam_ref, q_ref, *rest, n_group, n_new, post_scale):
    page_refs = rest[:n_group]
    kvn_ref, gain_ref, o_ref, m_ref, l_ref, acc_ref = rest[n_group:]
    j = pl.program_id(1)
    scale = DK_A ** -0.5
    q = q_ref[...]
    hk = H_A * 2 * DK_A

    @pl.when(j == 0)
    def _():
        m_ref[...] = jnp.full_like(m_ref, NEG)
        l_ref[...] = jnp.zeros_like(l_ref)
        acc_ref[...] = jnp.zeros_like(acc_ref)

    def update(s, v):
        m_old = m_ref[...]
        m_new = jnp.maximum(m_old, jnp.max(s, axis=-1, keepdims=True))
        a = jnp.exp(m_old - m_new)
        p = jnp.exp(s - m_new[:, 0:1])
        l_ref[...] = a * l_ref[...] + jnp.sum(p, axis=-1, keepdims=True)
        acc_ref[...] = a[:, 0:1] * acc_ref[...] + _dot(p.astype(BF16), v)
        m_ref[...] = m_new

    def heads(r, first):
        return jnp.concatenate([r[pl.ds(first + h, PAGE, stride=2 * H_A), :].astype(BF16) for h in range(H_A)],
                               axis=1)

    k = jnp.concatenate([heads(r, 0) for r in page_refs], axis=0)
    v = jnp.concatenate([heads(r, H_A) for r in page_refs], axis=0)
    update(_dot_nt(q, k) * scale, v)

    @pl.when(j == pl.num_programs(1) - 1)
    def _():
        pad = jnp.zeros((PAGE - n_new, 2 * hk), F32)
        kvn = jnp.concatenate([kvn_ref[...], pad], axis=0)
        s = _dot_nt(q, kvn[:, 0:hk].astype(BF16)) * scale
        tq = lax.broadcasted_iota(jnp.int32, s.shape, 0) % n_new
        tk = lax.broadcasted_iota(jnp.int32, s.shape, 1)
        update(jnp.where(tk <= tq, s, NEG), kvn[:, hk:2 * hk].astype(BF16))
        o = acc_ref[...] / l_ref[:, 0:1]
        lam = lam_ref[0]
        for h in range(H_A):
            r0 = h * 2 * n_new
            sl = slice(h * DV_A, (h + 1) * DV_A)
            oh = o[r0:r0 + n_new, sl] - lam * o[r0 + n_new:r0 + 2 * n_new, sl]
            o_ref[:, sl] = _rms_head(oh, gain_ref[...], post_scale)


def _diff_attn_paged(page_table, lam, q_exp, cache, layer, kv_new, gain, n_group, post_scale):
    b, n_pages = page_table.shape
    n_new = kv_new.shape[1]
    rows = q_exp.shape[1]
    width = kv_new.shape[2]
    pt = page_table.reshape(-1)

    def page_spec(g):
        return pl.BlockSpec((None, None, PAGE * 2 * H_A, DV_A),
                            lambda bi, j, pt_ref: (layer, pt_ref[bi * n_pages + j * n_group + g], 0, 0))

    grid_spec = pltpu.PrefetchScalarGridSpec(
        num_scalar_prefetch=1,
        grid=(b, n_pages // n_group),
        in_specs=[pl.BlockSpec(memory_space=pltpu.SMEM),
                  pl.BlockSpec((None, rows, q_exp.shape[2]), lambda bi, j, pt_ref: (bi, 0, 0))]
                 + [page_spec(g) for g in range(n_group)]
                 + [pl.BlockSpec((None, n_new, width), lambda bi, j, pt_ref: (bi, 0, 0)),
                    pl.BlockSpec((1, DV_A), lambda bi, j, pt_ref: (0, 0))],
        out_specs=pl.BlockSpec((None, n_new, H_A * DV_A), lambda bi, j, pt_ref: (bi, 0, 0)),
        scratch_shapes=[pltpu.VMEM((rows, 128), F32), pltpu.VMEM((rows, 128), F32),
                        pltpu.VMEM((rows, H_A * DV_A), F32)],
    )
    return pl.pallas_call(
        functools.partial(_diff_attn_paged_kernel, n_group=n_group, n_new=n_new, post_scale=post_scale),
        grid_spec=grid_spec,
        out_shape=jax.ShapeDtypeStruct((b, n_new, H_A * DV_A), F32),
        compiler_params=_params("parallel", "arbitrary"),
        name="diff_attn_paged",
    )(pt, lam, q_exp, *([cache] * n_group), kv_new, gain)


def _retention_kernel(q_ref, k_ref, v0_ref, v1_ref, g0_ref, g1_ref, s0_ref, dm_ref, qd_ref, kd_ref, cd_ref,
                      gain_ref, o_ref, sf_ref, s_ref):
    c = pl.program_id(1)

    @pl.when(c == 0)
    def _():
        s_ref[...] = s0_ref[...]

    for h in range(H_B):
        ks = slice(h * DK_B, (h + 1) * DK_B)
        half = slice((h % 2) * DV_B, (h % 2 + 1) * DV_B)
        v_ref, g_ref = (v0_ref, g0_ref) if h < 2 else (v1_ref, g1_ref)
        q = q_ref[:, ks]
        k = k_ref[:, ks]
        v = v_ref[:, half].astype(BF16)
        s = s_ref[h]
        att = _dot_nt(q, k.astype(BF16)) * dm_ref[h]
        o = _dot(att.astype(BF16), v) + _dot(q, s.astype(BF16)) * qd_ref[h]
        s_new = s * cd_ref[h] + _dot_tn((k * kd_ref[h]).astype(BF16), v)
        s_ref[h] = s_new
        sf_ref[h] = s_new
        mu = jnp.mean(o, axis=-1, keepdims=True)
        d = o - mu
        var = jnp.mean(d * d, axis=-1, keepdims=True)
        y = d * lax.rsqrt(var + EPS) * gain_ref[:, h * DV_B:(h + 1) * DV_B]
        g = g_ref[:, half]
        o_ref[:, h * DV_B:(h + 1) * DV_B] = (y * (g * _sigmoid(g))).astype(o_ref.dtype)


def _retention(q, k, vg, v_blk, g_blk, s0, tables, gain, b, nc, chunk):
    m = q.shape[0]
    dmask, qdec, kdec, cdec = tables
    row = lambda width, blk: pl.BlockSpec((chunk, width), lambda bi, c: (bi * nc + c, blk))
    full = lambda a: pl.BlockSpec(a.shape, lambda bi, c: (0,) * a.ndim)
    state = pl.BlockSpec((None, H_B, DK_B, DV_B), lambda bi, c: (bi, 0, 0, 0))
    return pl.pallas_call(
        _retention_kernel,
        grid=(b, nc),
        in_specs=[row(H_B * DK_B, 0), row(H_B * DK_B, 0),
                  row(2 * DV_B, v_blk), row(2 * DV_B, v_blk + 1), row(2 * DV_B, g_blk), row(2 * DV_B, g_blk + 1),
                  state, full(dmask), full(qdec), full(kdec), full(cdec), full(gain)],
        out_specs=[row(H_B * DV_B, 0), state],
        out_shape=[jax.ShapeDtypeStruct((m, H_B * DV_B), BF16),
                   jax.ShapeDtypeStruct((b, H_B, DK_B, DV_B), F32)],
        scratch_shapes=[pltpu.VMEM((H_B, DK_B, DV_B), F32)],
        compiler_params=_params("parallel", "arbitrary"),
        name="retention",
    )(q, k, vg, vg, vg, vg, s0, dmask, qdec, kdec, cdec, gain)


def _retention_tables(chunk, rows):
    log_g = jnp.log(1.0 - jnp.exp2(-5.0 - jnp.arange(H_B, dtype=F32)))
    idx = jnp.arange(chunk, dtype=F32)
    rel = idx[:, None] - idx[None, :]
    dmask = jnp.where(rel >= 0, jnp.exp(log_g[:, None, None] * jnp.maximum(rel, 0.0)), 0.0)
    q_dec = jnp.exp(log_g[:, None] * (idx[None, :] + 1.0))
    k_dec = jnp.exp(log_g[:, None] * (chunk - 1.0 - idx[None, :]))
    c_dec = jnp.exp(log_g * chunk)
    padn = rows - chunk
    dmask = jnp.pad(dmask, ((0, 0), (0, padn), (0, padn)))
    q_dec = jnp.pad(q_dec, ((0, 0), (0, padn)))
    k_dec = jnp.pad(k_dec, ((0, 0), (0, padn)))
    qd = jnp.broadcast_to(q_dec[:, :, None], (H_B, rows, DV_B))
    kd = jnp.broadcast_to(k_dec[:, :, None], (H_B, rows, DK_B))
    cd = jnp.broadcast_to(c_dec[:, None, None], (H_B, 1, DV_B))
    return dmask, qd, kd, cd


def _compress_kernel(x_ref, pe_ref, w_ref, o_ref):
    o_ref[...] = _dot((x_ref[...] + pe_ref[...]).astype(BF16), w_ref[...])


def _compress(x, pe, w, layer):
    nb = x.shape[1]
    kdim = CMP_BLOCK * DK_C
    return pl.pallas_call(
        _compress_kernel,
        grid=(2,),
        in_specs=[pl.BlockSpec((None, nb, kdim), lambda i: (i, 0, 0)),
                  pl.BlockSpec((None, 1, kdim), lambda i: (i, 0, 0)),
                  pl.BlockSpec((None, None, kdim, DK_C), lambda i: (layer, i, 0, 0))],
        out_specs=pl.BlockSpec((None, nb, DK_C), lambda i: (i, 0, 0)),
        out_shape=jax.ShapeDtypeStruct((2, nb, DK_C), F32),
        compiler_params=_params("parallel"),
        name="nsa_compress_prompt",
    )(x, pe, w)


def _cmp_attention(qs, qpos, kce, kco, vce, vco):
    scale = DK_C ** -0.5
    nh = kce.shape[0]
    n = lax.broadcasted_iota(jnp.int32, (1, nh), 1)
    me = (2 * CMP_BLOCK * n + (CMP_BLOCK - 1)) <= qpos
    mo = (2 * CMP_BLOCK * n + (2 * CMP_BLOCK - 1)) <= qpos
    se = jnp.where(me, _dot_nt(qs, kce) * scale, NEG)
    so = jnp.where(mo, _dot_nt(qs, kco) * scale, NEG)
    mx = jnp.maximum(jnp.max(se, axis=-1, keepdims=True), jnp.max(so, axis=-1, keepdims=True))
    ee = jnp.where(me, jnp.exp(se - mx), 0.0)
    eo = jnp.where(mo, jnp.exp(so - mx), 0.0)
    den = jnp.maximum(jnp.sum(ee, axis=-1, keepdims=True) + jnp.sum(eo, axis=-1, keepdims=True), TINY)
    pe = ee / den
    po = eo / den
    o = _dot(pe.astype(BF16), vce) + _dot(po.astype(BF16), vco)
    return o, pe, po


def _top_blocks(score, n_pick):
    blk = lax.broadcasted_iota(jnp.int32, score.shape, 1).astype(F32)
    big = 1e9
    work = score
    sel = jnp.zeros(score.shape, F32)
    idxs, vals = [], []
    for _ in range(n_pick):
        mval = jnp.max(work, axis=-1, keepdims=True)
        idx = jnp.min(jnp.where(work == mval, blk, big), axis=-1, keepdims=True)
        pick = blk == idx
        sel = jnp.where(pick, jnp.where(mval >= 0.0, 1.0, 0.0), sel)
        work = jnp.where(pick, -2.0, work)
        idxs.append(idx.astype(jnp.int32))
        vals.append(mval)
    return sel, idxs, vals


def _block_scores(imp, qpos):
    blk = lax.broadcasted_iota(jnp.int32, imp.shape, 1)
    cur = lax.shift_right_arithmetic(qpos, int(math.log2(SEL_BLOCK)))
    forced = jnp.where(blk == 0, 1, 0) + jnp.where(blk == cur, 1, 0) + jnp.where(blk == cur - 1, 1, 0)
    valid = blk * SEL_BLOCK <= qpos
    return jnp.where(valid, jnp.where(forced > 0, FORCED_SCORE, imp), -1.0)


def _nsa_prompt_kernel(q_ref, kce_ref, kco_ref, vce_ref, vco_ref, sw_ref, g_ref, o_ref, osel_ref, *,
                       tq, n_bucket, n_sel):
    qi = pl.program_id(1)
    scale = DK_C ** -0.5
    q = q_ref[...]
    qs = jnp.concatenate([q[:, h * DK_C:(h + 1) * DK_C] for h in range(H_C)], axis=0)
    qpos1 = qi * tq + lax.broadcasted_iota(jnp.int32, (tq, 1), 0)
    qpos = jnp.concatenate([qpos1] * H_C, axis=0)
    qrow1 = qi * tq + lax.broadcasted_iota(jnp.int32, (1, tq), 1)
    qrow = jnp.concatenate([qrow1] * H_C, axis=1)
    t = sw_ref.shape[0]
    ncp = kce_ref.shape[0]

    n = lax.broadcasted_iota(jnp.int32, (ncp, 1), 0)
    me = (2 * CMP_BLOCK * n + (CMP_BLOCK - 1)) <= qrow
    mo = (2 * CMP_BLOCK * n + (2 * CMP_BLOCK - 1)) <= qrow
    se = jnp.where(me, _dot_nt(kce_ref[...], qs) * scale, NEG)
    so = jnp.where(mo, _dot_nt(kco_ref[...], qs) * scale, NEG)
    mx = jnp.maximum(jnp.max(se, axis=0, keepdims=True), jnp.max(so, axis=0, keepdims=True))
    ee = jnp.where(me, jnp.exp(se - mx), 0.0)
    eo = jnp.where(mo, jnp.exp(so - mx), 0.0)
    den = jnp.maximum(jnp.sum(ee, axis=0, keepdims=True) + jnp.sum(eo, axis=0, keepdims=True), TINY)
    pe = ee / den
    po = eo / den
    o_cmp = _dot(pe.T.astype(BF16), vce_ref[...]) + _dot(po.T.astype(BF16), vco_ref[...])
    spe = pe[:, 0:tq] + pe[:, tq:2 * tq] + pe[:, 2 * tq:3 * tq] + pe[:, 3 * tq:4 * tq]
    spo = po[:, 0:tq] + po[:, tq:2 * tq] + po[:, 2 * tq:3 * tq] + po[:, 3 * tq:4 * tq]

    imp = (spe + spo)[0:n_sel]
    blk = lax.broadcasted_iota(jnp.int32, (n_sel, tq), 0)
    cur = lax.shift_right_arithmetic(qrow1, int(math.log2(SEL_BLOCK)))
    forced = jnp.where(blk == 0, 1, 0) + jnp.where(blk == cur, 1, 0) + jnp.where(blk == cur - 1, 1, 0)
    score = jnp.where(blk * SEL_BLOCK <= qrow1, jnp.where(forced > 0, FORCED_SCORE, imp), -1.0)
    blkf = blk.astype(F32)
    sel_t = jnp.zeros((n_sel, tq), F32)
    for _ in range(min(TOPK, n_sel)):
        mval = jnp.max(score, axis=0, keepdims=True)
        idx = jnp.min(jnp.where(score == mval, blkf, 1e9), axis=0, keepdims=True)
        pick = blkf == idx
        sel_t = jnp.where(pick, jnp.where(mval >= 0.0, 1.0, 0.0), sel_t)
        score = jnp.where(pick, -2.0, score)
    selb = jnp.concatenate([sel_t, jnp.zeros((ncp - n_sel, tq), F32)], axis=0).T.astype(BF16)

    kb = t // n_bucket
    bucket = lax.div((qi + 1) * tq - 1, kb)
    for nb in range(n_bucket):
        @pl.when(bucket == nb)
        def _(nkeys=(nb + 1) * kb):
            key_blk = lax.shift_right_arithmetic(lax.broadcasted_iota(jnp.int32, (ncp, nkeys), 1),
                                                 int(math.log2(SEL_BLOCK)))
            expand = jnp.where(key_blk == lax.broadcasted_iota(jnp.int32, (ncp, nkeys), 0), 1.0, 0.0).astype(BF16)
            kpos = lax.broadcasted_iota(jnp.int32, (1, nkeys), 1)
            selk1 = jnp.where(kpos <= qpos1, _dot(selb, expand), 0.0)
            selk = jnp.concatenate([selk1] * H_C, axis=0) > 0.5
            e, l = _visible_softmax_terms(_dot_nt(qs, sw_ref[0:nkeys, 0:128]) * scale, selk)
            osel_ref[...] = _dot(e.astype(BF16), sw_ref[0:nkeys, 128:256]) * (1.0 / l)
    o_sel = osel_ref[...]

    wlen = min(t, WINDOW + tq)
    start = pl.multiple_of(jnp.clip(qi * tq - WINDOW, 0, t - wlen), 128) if wlen < t else 0
    wpos = start + lax.broadcasted_iota(jnp.int32, (1, wlen), 1)
    d = qpos - wpos
    wmask = jnp.abs(2 * d - WINDOW) <= WINDOW
    e, l = _visible_softmax_terms(_dot_nt(qs, sw_ref[pl.ds(start, wlen), 256:384]) * scale, wmask)
    o_win = _dot(e.astype(BF16), sw_ref[pl.ds(start, wlen), 384:512]) * (1.0 / l)

    g = _sigmoid(g_ref[...])
    for h in range(H_C):
        r = slice(h * tq, (h + 1) * tq)
        oc = (g[:, h:h + 1] * o_cmp[r] + g[:, H_C + h:H_C + h + 1] * o_sel[r]
              + g[:, 2 * H_C + h:2 * H_C + h + 1] * o_win[r])
        o_ref[:, h * DK_C:(h + 1) * DK_C] = oc.astype(o_ref.dtype)


def _nsa_prompt(qc, kce, kco, vce, vco, sw, u, b, t, tq):
    m = qc.shape[0]
    nq = t // tq
    nh = kce.shape[1]
    n_sel = t // SEL_BLOCK
    assert nh == 128 and n_sel <= nh and t % SEL_BLOCK == 0 and 2 * CMP_BLOCK * n_sel >= t
    cm = pl.BlockSpec((None, nh, DK_C), lambda bi, qi: (bi, 0, 0))
    return pl.pallas_call(
        functools.partial(_nsa_prompt_kernel, tq=tq, n_bucket=_key_buckets(t, tq), n_sel=n_sel),
        grid=(b, nq),
        scratch_shapes=[pltpu.VMEM((H_C * tq, DK_C), F32)],
        in_specs=[pl.BlockSpec((tq, H_C * DK_C), lambda bi, qi: (bi * nq + qi, 0)), cm, cm, cm, cm,
                  pl.BlockSpec((t, 512), lambda bi, qi: (bi, 0)),
                  pl.BlockSpec((tq, 128), lambda bi, qi: (bi * nq + qi, COL_CG // 128))],
        out_specs=pl.BlockSpec((tq, H_C * DK_C), lambda bi, qi: (bi * nq + qi, 0)),
        out_shape=jax.ShapeDtypeStruct((m, H_C * DK_C), BF16),
        compiler_params=_params("parallel", "parallel"),
        name="nsa_prompt",
    )(qc, kce, kco, vce, vco, sw, u)


def _compress_paged_kernel(pt_ref, *rest, n_group):
    page_refs = rest[:n_group]
    pe_ref, w_ref, o_ref, xk_ref, xv_ref = rest[n_group:]
    j = pl.program_id(1)
    per_page = PAGE // CMP_BLOCK
    for g, r in enumerate(page_refs):
        for n in range(per_page):
            blk = (j * n_group + g) * per_page + n
            dst = pl.ds(pl.multiple_of(blk * CMP_PITCH, 8), CMP_BLOCK)
            xk_ref[dst, :] = r[pl.ds(n * CMP_BLOCK * 4, CMP_BLOCK, stride=4), :]
            xv_ref[dst, :] = r[pl.ds(n * CMP_BLOCK * 4 + 1, CMP_BLOCK, stride=4), :]

    @pl.when(j == pl.num_programs(1) - 1)
    def _():
        nb = xk_ref.shape[0] // CMP_PITCH
        acc_k = jnp.zeros((nb, DK_C), F32)
        acc_v = jnp.zeros((nb, DK_C), F32)
        for i in range(0, CMP_BLOCK, 2):
            def pair(x_ref, c):
                return jnp.concatenate(
                    [x_ref[pl.ds(i + d, nb, stride=CMP_PITCH), :] + pe_ref[c, i + d:i + d + 1, :] for d in (0, 1)],
                    axis=1).astype(BF16)
            wsl = slice(i * DK_C, (i + 2) * DK_C)
            acc_k = acc_k + _dot(pair(xk_ref, 0), w_ref[0, wsl, :])
            acc_v = acc_v + _dot(pair(xv_ref, 1), w_ref[1, wsl, :])
        o_ref[0] = acc_k
        o_ref[1] = acc_v


def _compress_paged(page_table, cache, layer, pe, w, n_group):
    b, n_pages = page_table.shape
    pt = page_table.reshape(-1)
    nb = n_pages * PAGE // CMP_BLOCK

    def page_spec(g):
        return pl.BlockSpec((None, None, PAGE * 4, DK_C),
                            lambda bi, j, pt_ref: (layer, pt_ref[bi * n_pages + j * n_group + g], 0, 0))

    grid_spec = pltpu.PrefetchScalarGridSpec(
        num_scalar_prefetch=1,
        grid=(b, n_pages // n_group),
        in_specs=[page_spec(g) for g in range(n_group)]
                 + [pl.BlockSpec((2, CMP_BLOCK, DK_C), lambda bi, j, pt_ref: (0, 0, 0)),
                    pl.BlockSpec((None, 2, CMP_BLOCK * DK_C, DK_C), lambda bi, j, pt_ref: (layer, 0, 0, 0))],
        out_specs=pl.BlockSpec((None, 2, nb, DK_C), lambda bi, j, pt_ref: (bi, 0, 0, 0)),
        scratch_shapes=[pltpu.VMEM((nb * CMP_PITCH, DK_C), F32), pltpu.VMEM((nb * CMP_PITCH, DK_C), F32)],
    )
    return pl.pallas_call(
        functools.partial(_compress_paged_kernel, n_group=n_group),
        grid_spec=grid_spec,
        out_shape=jax.ShapeDtypeStruct((b, 2, nb, DK_C), F32),
        compiler_params=_params("parallel", "arbitrary"),
        name="nsa_compress_paged",
    )(pt, *([cache] * n_group), pe, w)


def _nsa_sample_select_kernel(q_ref, kce_ref, kco_ref, vce_ref, vco_ref, o_ref, idx_ref, *, n_new, past_len, n_pick):
    qs = q_ref[...]
    rows = qs.shape[0]
    qpos = past_len + lax.broadcasted_iota(jnp.int32, (rows, 1), 0) % n_new
    o_cmp, pe, po = _cmp_attention(qs, qpos, kce_ref[...], kco_ref[...], vce_ref[...], vco_ref[...])
    o_ref[...] = o_cmp
    spe = pe[0:n_new]
    spo = po[0:n_new]
    for h in range(1, H_C):
        spe = spe + pe[h * n_new:(h + 1) * n_new]
        spo = spo + po[h * n_new:(h + 1) * n_new]
    score = _block_scores(spe + spo, qpos[0:n_new])
    _, idxs, _ = _top_blocks(score, n_pick)
    lane = lax.broadcasted_iota(jnp.int32, (n_new, 128), 1)
    out = jnp.zeros((n_new, 128), jnp.int32)
    for r, idx in enumerate(idxs):
        out = jnp.where(lane == r, idx, out)
    idx_ref[...] = out


def _nsa_sample_select(qs, kce, kco, vce, vco, n_new, past_len, n_pick):
    b, rows, _ = qs.shape
    nh = kce.shape[1]
    cm = pl.BlockSpec((None, nh, DK_C), lambda bi: (bi, 0, 0))
    return pl.pallas_call(
        functools.partial(_nsa_sample_select_kernel, n_new=n_new, past_len=past_len, n_pick=n_pick),
        grid=(b,),
        in_specs=[pl.BlockSpec((None, rows, DK_C), lambda bi: (bi, 0, 0)), cm, cm, cm, cm],
        out_specs=[pl.BlockSpec((None, rows, DK_C), lambda bi: (bi, 0, 0)),
                   pl.BlockSpec((None, n_new, 128), lambda bi: (bi, 0, 0))],
        out_shape=[jax.ShapeDtypeStruct((b, rows, DK_C), F32), jax.ShapeDtypeStruct((b, n_new, 128), jnp.int32)],
        compiler_params=_params("parallel"),
        name="nsa_sample_select",
    )(qs, kce, kco, vce, vco)


def _nsa_sample_attend_kernel(pt_ref, ix_ref, q_ref, *rest, n_pick, n_new, past_len):
    blk_refs = rest[:n_pick]
    new_ref, wst_ref, wnew_ref, ocmp_ref, g_ref, o_ref = rest[n_pick:]
    bi = pl.program_id(0)
    qi = pl.program_id(1)
    scale = DK_C ** -0.5
    q = q_ref[...].astype(BF16)
    qpos = past_len + qi
    lane_blk = lax.broadcasted_iota(jnp.int32, (1, SEL_BLOCK), 1)

    ks = [r[pl.ds(2, SEL_BLOCK, stride=4), :].astype(BF16) for r in blk_refs]
    vs = [r[pl.ds(3, SEL_BLOCK, stride=4), :].astype(BF16) for r in blk_refs]
    kpos = [ix_ref[(bi * n_new + qi) * TOPK + r] * SEL_BLOCK + lane_blk for r in range(n_pick)]
    new = jnp.concatenate([new_ref[...], jnp.zeros((SEL_BLOCK - n_new, 2 * DK_C), F32)], axis=0)
    ks.append(new[:, 0:DK_C].astype(BF16))
    vs.append(new[:, DK_C:2 * DK_C].astype(BF16))
    kpos.append(past_len + lane_blk)
    s = _dot_nt(q, jnp.concatenate(ks, axis=0)) * scale
    mask = jnp.concatenate(kpos, axis=1) <= qpos
    o_sel = _dot(_masked_softmax(s, mask).astype(BF16), jnp.concatenate(vs, axis=0))

    nbuf = wst_ref.shape[0]
    wnew = jnp.concatenate([wnew_ref[...], jnp.zeros((PAGE - n_new, 2 * DK_C), F32)], axis=0)
    kw = jnp.concatenate([wst_ref[:, 0:DK_C].astype(BF16), wnew[:, 0:DK_C].astype(BF16)], axis=0)
    vw = jnp.concatenate([wst_ref[:, DK_C:2 * DK_C].astype(BF16), wnew[:, DK_C:2 * DK_C].astype(BF16)], axis=0)
    wpos = past_len - nbuf + lax.broadcasted_iota(jnp.int32, (1, nbuf + PAGE), 1)
    d = qpos - wpos
    wmask = jnp.where(wpos >= 0, jnp.abs(2 * d - WINDOW), 4 * WINDOW) <= WINDOW
    s = _dot_nt(q, kw) * scale
    o_win = _dot(_masked_softmax(s, wmask).astype(BF16), vw)

    g = _sigmoid(g_ref[...])
    o_ref[...] = g[0] * ocmp_ref[...] + g[1] * o_sel + g[2] * o_win


def _nsa_sample_attend(page_table, idx, q8, cache, layer, nsa_new, win_state, win_new, ocmp8, gexp,
                       n_pick, past_len):
    b, n_pages = page_table.shape
    n_new = q8.shape[1]
    pt = page_table.reshape(-1)
    nbuf = win_state.shape[1]
    per_page = PAGE // SEL_BLOCK

    def blk_spec(r):
        def imap(bi, qi, pt_ref, ix_ref):
            blk = ix_ref[(bi * n_new + qi) * TOPK + r]
            return (layer, pt_ref[bi * n_pages + blk // per_page], blk % per_page, 0)
        return pl.BlockSpec((None, None, SEL_BLOCK * 4, DK_C), imap)

    per_q = lambda: pl.BlockSpec((None, None, 8, DK_C), lambda bi, qi, p, x: (bi, qi, 0, 0))
    grid_spec = pltpu.PrefetchScalarGridSpec(
        num_scalar_prefetch=2,
        grid=(b, n_new),
        in_specs=[per_q()] + [blk_spec(r) for r in range(n_pick)]
                 + [pl.BlockSpec((None, n_new, 2 * DK_C), lambda bi, qi, p, x: (bi, 0, 1)),
                    pl.BlockSpec((None, nbuf, 2 * DK_C), lambda bi, qi, p, x: (bi, 0, 0)),
                    pl.BlockSpec((None, n_new, 2 * DK_C), lambda bi, qi, p, x: (bi, 0, 0)),
                    per_q(),
                    pl.BlockSpec((None, None, 3, 8, DK_C), lambda bi, qi, p, x: (bi, qi, 0, 0, 0))],
        out_specs=per_q(),
    )
    return pl.pallas_call(
        functools.partial(_nsa_sample_attend_kernel, n_pick=n_pick, n_new=n_new, past_len=past_len),
        grid_spec=grid_spec,
        out_shape=jax.ShapeDtypeStruct((b, n_new, 8, DK_C), F32),
        compiler_params=_params("parallel", "parallel"),
        name="nsa_sample_attend",
    )(pt, idx, q8, *([cache] * n_pick), nsa_new, win_state, win_new, ocmp8, gexp)


def _tile(m, pref):
    if m <= pref:
        return m
    t = pref - pref % 128
    while m % t:
        t -= 128
    assert t > 0, (m, pref)
    return t


def _lambda_scalar(lp, layer):
    lam_init = 0.8 - 0.6 * math.exp(-0.3 * layer)
    lam = jnp.exp(jnp.sum(lp[0] * lp[1])) - jnp.exp(jnp.sum(lp[2] * lp[3])) + lam_init
    return lam.reshape(1).astype(F32), 1.0 - lam_init


def _dense_tail(x, xb, oa, ob, oc, p, w, layer, seq_len, conv_prev):
    m = x.shape[0]
    short = seq_len < 128
    mix = _merge(xb, w['w_mg'], oa, ob, oc, w['wa'], w['wb'], w['wc'], layer, _tile(m, 512), 512)
    x1, x1b = _proj_ln(mix, w['w_out'], layer, x, w['g0'], w['b0'], _tile(m, 512))
    if short:
        p1, p2 = conv_prev
        h, a = _ffn_up_short(x1b, w['w_fg'], w['w_fu'], layer, w['cw'], w['cb'], p1, p2, seq_len, 512)
        conv = a.reshape(m // seq_len, seq_len, D_FF_PAD)[:, seq_len - (CONV_W - 1):, :D_FF]
    else:
        h, st = _ffn_up_seq(x1b, w['w_fg'], w['w_fu'], layer, w['cw'], w['cb'], seq_len, _tile(seq_len, 2048), 512)
        conv = st[:, 8 - (CONV_W - 1):, :D_FF]
    x2, x2b = _ffn_down_ln(h, w['w_fd'], layer, x1, w['g1'], w['b1'], _tile(m, 512), D_FF_PAD // 4)
    x3, x3b = _ple_ln(x2b, w['w_pg'], p, w['w_pp'], layer, x2, w['g2'], w['b2'], _tile(m, 512))
    return x3, x3b, conv


def _prompt_layer(x, xb, p, w, layer, b, t, rope_tabs, ret_tabs):
    m = b * t
    u = _matmul(xb, w['w_in'], layer, F32, _tile(m, 1024), 1024, "in_proj")
    (qa, new_diff, kva, qb, kb, qc, new_nsa, sw, new_win, kcm, vcm) = _rope_split(u, rope_tabs, _tile(t, 256))

    lam, post = _lambda_scalar(w['diff_lambda'], layer)
    oa = _diff_attn_prompt(lam, qa, kva, w['diff_gain'], b, t, _tile(t, 256), post)

    chunk = RET_CHUNK
    s0 = jnp.zeros((b, H_B, DK_B, DV_B), F32)
    ob, ret = _retention(qb, kb, u, COL_BV // 512, COL_BG // 512, s0, ret_tabs, w['ret_gain'], b, t // chunk, chunk)

    nb = m // CMP_BLOCK
    xc = jnp.stack([kcm.reshape(nb, CMP_BLOCK * DK_C), vcm.reshape(nb, CMP_BLOCK * DK_C)])
    kvc = _compress(xc, w['cmp_pe'].reshape(2, 1, CMP_BLOCK * DK_C), w['cmp_w'], layer)
    kvc = kvc.reshape(2, b, t // CMP_BLOCK, DK_C).astype(BF16)
    half = lambda a: jnp.pad(a, ((0, 0), (0, 128 - a.shape[1]), (0, 0)))
    oc = _nsa_prompt(qc, half(kvc[0, :, 0::2]), half(kvc[0, :, 1::2]), half(kvc[1, :, 0::2]),
                     half(kvc[1, :, 1::2]), sw, u, b, t, _tile(t, 128))

    x3, x3b, conv = _dense_tail(x, xb, oa, ob, oc, p, w, layer, t, None)
    nwin = min(WINDOW, t)
    states = (new_diff.reshape(b, t, 2, H_A, DV_A), new_nsa.reshape(b, t, 4, DK_C),
              new_win.reshape(b, t, 2, DK_C)[:, t - nwin:], ret, conv)
    return x3, x3b, states


def _sample_layer(x, xb, p, w, layer, b, t, past_len, rope_tabs, ret_tabs, page_table,
                  cache_diff, cache_nsa, win_state, ret_state, conv_state):
    m = b * t
    u = _matmul(xb, w['w_in'], layer, F32, m, 1024, "in_proj")
    (qa, new_diff, kva, qb, kb, qc, new_nsa, sw, new_win, kcm, vcm) = _rope_split(u, rope_tabs, m)

    lam, post = _lambda_scalar(w['diff_lambda'], layer)
    q5 = qa.reshape(b, t, H_A, 2, DK_A).transpose(0, 2, 3, 1, 4)
    eye = jnp.eye(H_A * 2, dtype=BF16).reshape(H_A, 2, 1, H_A, 2, 1)
    q_exp = (q5[:, :, :, :, None, None, :] * eye[None]).reshape(b, H_A * 2 * t, H_A * 2 * DK_A)
    n_phys = cache_diff.shape[1]
    oa = _diff_attn_paged(page_table, lam, q_exp, cache_diff.reshape(-1, n_phys, PAGE * 2 * H_A, DV_A),
                          layer, new_diff.reshape(b, t, 2 * H_A * DV_A), w['diff_gain'], 16, post)
    oa = oa.reshape(m, H_A * DV_A).astype(BF16)

    rows = RET_CHUNK
    padr = lambda a: jnp.pad(a.reshape(b, t, a.shape[-1]), ((0, 0), (0, rows - t), (0, 0))).reshape(b * rows, -1)
    ob, ret = _retention(padr(qb), padr(kb), padr(u[:, COL_BV:COL_CQ]), 0, 2, ret_state, ret_tabs,
                         w['ret_gain'], b, 1, rows)
    ob = ob.reshape(b, rows, H_B * DV_B)[:, :t].reshape(m, H_B * DV_B)

    n_pages = page_table.shape[1]
    cache_rows = cache_nsa.reshape(-1, n_phys, PAGE * 4, DK_C)
    kvc = _compress_paged(page_table, cache_rows, layer, w['cmp_pe'], w['cmp_w'], 16).astype(BF16)
    qs = qc.reshape(b, t, H_C, DK_C).transpose(0, 2, 1, 3).reshape(b, H_C * t, DK_C)
    n_pick = TOPK - 1
    assert past_len % SEL_BLOCK == 0 and t <= SEL_BLOCK and past_len // SEL_BLOCK >= n_pick
    ocmp, idx = _nsa_sample_select(qs, kvc[:, 0, 0::2], kvc[:, 0, 1::2], kvc[:, 1, 0::2], kvc[:, 1, 1::2],
                                   t, past_len, n_pick)
    pad8 = lambda a: jnp.pad(a, ((0, 0),) * (a.ndim - 2) + ((0, 8 - H_C), (0, 0)))
    q8 = pad8(qc.astype(F32).reshape(b, t, H_C, DK_C))
    ocmp8 = pad8(ocmp.reshape(b, H_C, t, DK_C).transpose(0, 2, 1, 3))
    cg = u[:, COL_CG:COL_CG + 3 * H_C].reshape(b, t, 3, H_C)
    gexp = jnp.broadcast_to(pad8(cg[..., None]), (b, t, 3, 8, DK_C))
    oc = _nsa_sample_attend(page_table, idx[:, :, :TOPK].reshape(-1), q8, cache_rows, layer,
                            new_nsa.reshape(b, t, 4 * DK_C), win_state.reshape(b, -1, 2 * DK_C),
                            new_win.reshape(b, t, 2 * DK_C), ocmp8, gexp, n_pick, past_len)
    oc = oc[:, :, :H_C].reshape(m, H_C * DK_C).astype(BF16)

    cs = jnp.pad(conv_state, ((0, 0), (0, 0), (0, D_FF_PAD - D_FF)))
    zeros = jnp.zeros((b, t - 1, D_FF_PAD), F32)
    p1 = jnp.concatenate([cs[:, 1:2], zeros], axis=1).reshape(m, D_FF_PAD)
    p2 = jnp.concatenate([cs, zeros[:, 1:]], axis=1).reshape(m, D_FF_PAD)

    x3, x3b, conv = _dense_tail(x, xb, oa, ob, oc, p, w, layer, t, (p1, p2))
    nbuf = win_state.shape[1]
    win = jnp.concatenate([win_state, new_win.reshape(b, t, 2, DK_C)], axis=1)[:, t:]
    assert win.shape[1] == nbuf
    states = (new_diff.reshape(b, t, 2, H_A, DV_A), new_nsa.reshape(b, t, 4, DK_C), win, ret, conv)
    return x3, x3b, states


def _layer_weights(i, ln_gain, ln_bias, w_in, diff_lambda, diff_norm_gain, ret_norm_gain, nsa_cmp_pos, nsa_cmp_w,
                   w_branch_a, w_branch_b, w_branch_c, w_merge_gate, w_out, w_ffn_gate, w_ffn_up, ffn_conv_w,
                   ffn_conv_b, w_ffn_down, w_ple_gate, w_ple_proj):
    bf = lambda a: a.astype(BF16)
    padc = lambda a, n: jnp.pad(a, ((0, 0),) * (a.ndim - 1) + ((0, n - a.shape[-1]),))
    return {
        'w_in': bf(padc(w_in, N_IN_PAD)),
        'diff_lambda': diff_lambda[i].astype(F32),
        'diff_gain': diff_norm_gain[i].reshape(1, DV_A),
        'ret_gain': ret_norm_gain[i].reshape(1, H_B * DV_B),
        'cmp_pe': nsa_cmp_pos[i],
        'cmp_w': bf(nsa_cmp_w),
        'wa': bf(w_branch_a), 'wb': bf(w_branch_b), 'wc': bf(w_branch_c),
        'w_mg': bf(w_merge_gate), 'w_out': bf(w_out),
        'w_fg': bf(padc(w_ffn_gate, D_FF_PAD)), 'w_fu': bf(padc(w_ffn_up, D_FF_PAD)),
        'cw': padc(ffn_conv_w[i], D_FF_PAD), 'cb': padc(ffn_conv_b[i].reshape(1, D_FF), D_FF_PAD),
        'w_fd': bf(jnp.pad(w_ffn_down, ((0, 0), (0, D_FF_PAD - D_FF), (0, 0)))),
        'w_pg': bf(w_ple_gate), 'w_pp': bf(w_ple_proj),
        'g0': ln_gain[i, 0:1], 'g1': ln_gain[i, 1:2], 'g2': ln_gain[i, 2:3],
        'b0': ln_bias[i, 0:1], 'b1': ln_bias[i, 1:2], 'b2': ln_bias[i, 2:3],
    }


def kernel(x_prompt, x_sample, cache_diff_kv, cache_nsa_kv, state_nsa_win, state_ret, state_conv, page_table,
           p_prompt, p_sample, ln_gain, ln_bias, w_in, diff_lambda, diff_norm_gain, ret_norm_gain, nsa_cmp_pos,
           nsa_cmp_w, w_branch_a, w_branch_b, w_branch_c, w_merge_gate, w_out, w_ffn_gate, w_ffn_up, ffn_conv_w,
           ffn_conv_b, w_ffn_down, w_ple_gate, w_ple_proj):
    bp, tp, _ = x_prompt.shape
    bs, ts, _ = x_sample.shape
    n_layers = w_in.shape[0]
    past_len = page_table.shape[1] * PAGE

    rope_p = _rope_tables(jnp.arange(tp, dtype=jnp.int32))
    rope_s = tuple(jnp.tile(tb, (bs, 1)) for tb in _rope_tables(past_len + jnp.arange(ts, dtype=jnp.int32)))
    ret_p = _retention_tables(RET_CHUNK, RET_CHUNK)
    ret_s = _retention_tables(ts, RET_CHUNK)

    xp = x_prompt.reshape(bp * tp, D_MODEL)
    xs = x_sample.reshape(bs * ts, D_MODEL)
    xpb, xsb = xp.astype(BF16), xs.astype(BF16)
    st_p, st_s = [], []
    for i in range(n_layers):
        w = _layer_weights(i, ln_gain, ln_bias, w_in, diff_lambda, diff_norm_gain, ret_norm_gain, nsa_cmp_pos,
                           nsa_cmp_w, w_branch_a, w_branch_b, w_branch_c, w_merge_gate, w_out, w_ffn_gate,
                           w_ffn_up, ffn_conv_w, ffn_conv_b, w_ffn_down, w_ple_gate, w_ple_proj)
        xp, xpb, sp = _prompt_layer(xp, xpb, p_prompt[i].reshape(bp * tp, PLE_DIM), w, i, bp, tp, rope_p, ret_p)
        xs, xsb, ss = _sample_layer(xs, xsb, p_sample[i].reshape(bs * ts, PLE_DIM), w, i, bs, ts, past_len,
                                    rope_s, ret_s, page_table, cache_diff_kv, cache_nsa_kv, state_nsa_win[i],
                                    state_ret[i], state_conv[i])
        st_p.append(sp)
        st_s.append(ss)
    diff_p, nsa_p, win_p, ret_p_out, conv_p = [jnp.stack(s) for s in zip(*st_p)]
    diff_s, nsa_s, win_s, ret_s_out, conv_s = [jnp.stack(s) for s in zip(*st_s)]
    return (xp.reshape(bp, tp, D_MODEL), xs.reshape(bs, ts, D_MODEL), diff_p, diff_s, nsa_p, nsa_s,
            win_p, win_s, ret_p_out, ret_s_out, conv_p, conv_s)
```

```python
import functools
import math

import jax
import jax.numpy as jnp
from jax import lax
from jax.experimental import pallas as pl
from jax.experimental.pallas import tpu as pltpu

F32 = jnp.float32
BF16 = jnp.bfloat16

D_MODEL = 2048
H_A, DK_A, DV_A = 4, 64, 128
H_B, DK_B, DV_B = 4, 128, 256
RET_CHUNK = 128
RET_THETA = 10000.0
H_C, DK_C = 4, 128
CMP_BLOCK, SEL_BLOCK, TOPK, WINDOW = 32, 64, 16, 512
FORCED_SCORE = H_C + 1.0
ROPE_THETA = 500000.0
D_FF = 5504
CONV_W = 3
PLE_DIM = 256
N_LAYERS = 4
ALPHA = (2 * N_LAYERS) ** 0.25
EPS = 1e-5
PAGE = 128

N_IN = 5900
N_IN_PAD = 6144
COL_AQ, COL_AK, COL_AV = 0, 512, 1024
COL_BQ, COL_BK, COL_BV, COL_BG = 1536, 2048, 2560, 3584
COL_CQ, COL_CKV, COL_CWIN, COL_CG = 4608, 5120, 5632, 5888
D_FF_PAD = 5632
CMP_PITCH = 40

VMEM_LIMIT = 56 * 1024 * 1024

NEG = float(jnp.finfo(jnp.float32).min)
TINY = float(jnp.finfo(jnp.float32).tiny)


def _params(*sem):
    return pltpu.CompilerParams(dimension_semantics=sem, vmem_limit_bytes=VMEM_LIMIT)


def _dot(a, b):
    return jnp.dot(a, b, preferred_element_type=F32)


def _dot_nt(a, b):
    return lax.dot_general(a, b, (((1,), (1,)), ((), ())), preferred_element_type=F32)


def _dot_tn(a, b):
    return lax.dot_general(a, b, (((0,), (0,)), ((), ())), preferred_element_type=F32)


def _masked_softmax(s, mask):
    s = jnp.where(mask, s, NEG)
    e = jnp.where(mask, jnp.exp(s - jnp.max(s, axis=-1, keepdims=True)), 0.0)
    return e / jnp.maximum(jnp.sum(e, axis=-1, keepdims=True), TINY)


def _visible_softmax_terms(s, mask):
    s = jnp.where(mask, s, NEG)
    e = jnp.exp(s - jnp.max(s, axis=-1, keepdims=True))
    return e, jnp.sum(e, axis=-1, keepdims=True)


def _layer_norm(y, g, b):
    mu = jnp.mean(y, axis=-1, keepdims=True)
    d = y - mu
    var = jnp.mean(d * d, axis=-1, keepdims=True)
    return d * lax.rsqrt(var + EPS) * g + b


def _sigmoid(x):
    return 1.0 / (1.0 + jnp.exp(-x))


def _cast_pad_cols_kernel(x_ref, o_ref):
    n = x_ref.shape[1]
    o_ref[:, 0:n] = x_ref[...].astype(BF16)
    o_ref[:, n:] = jnp.zeros((o_ref.shape[0], o_ref.shape[1] - n), BF16)


def _cast_pad_cols(w, n_pad, tr):
    l, k, n = w.shape
    assert n % 128 == 0 and k % tr == 0
    return pl.pallas_call(
        _cast_pad_cols_kernel,
        grid=(l, k // tr),
        in_specs=[pl.BlockSpec((None, tr, n), lambda i, r: (i, r, 0))],
        out_specs=pl.BlockSpec((None, tr, n_pad), lambda i, r: (i, r, 0)),
        out_shape=jax.ShapeDtypeStruct((l, k, n_pad), BF16),
        compiler_params=_params("parallel", "parallel"),
        name="cast_pad_cols",
    )(w)


def _cast_pad_rows_kernel(x_ref, o_ref, *, n_in_blocks):
    r = pl.program_id(1)
    x = x_ref[...].astype(BF16)
    o_ref[...] = jnp.where(r < n_in_blocks, x, jnp.zeros_like(x))


def _cast_pad_rows(w, k_pad, tr):
    l, k, n = w.shape
    assert k % tr == 0 and tr % 16 == 0
    n_in = k // tr
    n_out = -(-k_pad // tr)
    return pl.pallas_call(
        functools.partial(_cast_pad_rows_kernel, n_in_blocks=n_in),
        grid=(l, n_out),
        in_specs=[pl.BlockSpec((None, tr, n), lambda i, r: (i, jnp.minimum(r, n_in - 1), 0))],
        out_specs=pl.BlockSpec((None, tr, n), lambda i, r: (i, r, 0)),
        out_shape=jax.ShapeDtypeStruct((l, k_pad, n), BF16),
        compiler_params=_params("parallel", "parallel"),
        name="cast_pad_rows",
    )(w)


def _mm_kernel(x_ref, w_ref, o_ref):
    o_ref[...] = _dot(x_ref[...], w_ref[...]).astype(o_ref.dtype)


def _matmul(x, w, layer, out_dtype, tm, tn, name):
    m, k = x.shape
    n = w.shape[2]
    return pl.pallas_call(
        _mm_kernel,
        grid=(n // tn, m // tm),
        in_specs=[pl.BlockSpec((tm, k), lambda j, i: (i, 0)),
                  pl.BlockSpec((None, k, tn), lambda j, i: (layer, 0, j))],
        out_specs=pl.BlockSpec((tm, tn), lambda j, i: (i, j)),
        out_shape=jax.ShapeDtypeStruct((m, n), out_dtype),
        compiler_params=_params("parallel", "parallel"),
        name=name,
    )(x, w)


def _merge_kernel(x_ref, w0_ref, w1_ref, w2_ref, oa_ref, ob_ref, oc_ref, wa_ref, wb_ref, wc_ref, o_ref):
    x = x_ref[...]
    m = _sigmoid(_dot(x, w0_ref[...])) * _dot(oa_ref[...], wa_ref[...])
    m = m + _sigmoid(_dot(x, w1_ref[...])) * _dot(ob_ref[...], wb_ref[...])
    m = m + _sigmoid(_dot(x, w2_ref[...])) * _dot(oc_ref[...], wc_ref[...])
    o_ref[...] = m.astype(o_ref.dtype)


def _merge(xb, w_mg, oa, ob, oc, wa, wb, wc, layer, tm, tn):
    m = xb.shape[0]
    nj = D_MODEL // tn
    row = lambda width: pl.BlockSpec((tm, width), lambda j, i: (i, 0))
    col = lambda k, off: pl.BlockSpec((None, k, tn), lambda j, i: (layer, 0, j + off))
    return pl.pallas_call(
        _merge_kernel,
        grid=(nj, m // tm),
        in_specs=[row(D_MODEL), col(D_MODEL, 0), col(D_MODEL, nj), col(D_MODEL, 2 * nj),
                  row(H_A * DV_A), row(H_B * DV_B), row(H_C * DK_C),
                  col(H_A * DV_A, 0), col(H_B * DV_B, 0), col(H_C * DK_C, 0)],
        out_specs=pl.BlockSpec((tm, tn), lambda j, i: (i, j)),
        out_shape=jax.ShapeDtypeStruct((m, D_MODEL), BF16),
        compiler_params=_params("parallel", "parallel"),
        name="merge_gate",
    )(xb, w_mg, w_mg, w_mg, oa, ob, oc, wa, wb, wc)


def _proj_ln_kernel(m_ref, w_ref, x_ref, g_ref, b_ref, o_ref, ob_ref):
    y = ALPHA * x_ref[...] + _dot(m_ref[...], w_ref[...])
    out = _layer_norm(y, g_ref[...], b_ref[...])
    o_ref[...] = out
    ob_ref[...] = out.astype(BF16)


def _resident_weight(k, n, layer):
    return pl.BlockSpec((None, k, n), lambda i: (layer, 0, 0), pipeline_mode=pl.Buffered(1))


def _proj_ln(mb, w, layer, x, g, b, tm):
    m = x.shape[0]
    k = mb.shape[1]
    row = lambda width: pl.BlockSpec((tm, width), lambda i: (i, 0))
    full = lambda r, c: pl.BlockSpec((r, c), lambda i: (0, 0))
    return pl.pallas_call(
        _proj_ln_kernel,
        grid=(m // tm,),
        in_specs=[row(k), _resident_weight(k, D_MODEL, layer), row(D_MODEL), full(1, D_MODEL), full(1, D_MODEL)],
        out_specs=[row(D_MODEL), row(D_MODEL)],
        out_shape=[jax.ShapeDtypeStruct((m, D_MODEL), F32), jax.ShapeDtypeStruct((m, D_MODEL), BF16)],
        compiler_params=_params("parallel"),
        name="out_proj_ln",
    )(mb, w, x, g, b)


def _ple_ln_kernel(xb_ref, wg_ref, p_ref, wp_ref, x_ref, g_ref, b_ref, o_ref, ob_ref):
    pe = _sigmoid(_dot(xb_ref[...], wg_ref[...])) * _dot(p_ref[...].astype(BF16), wp_ref[...])
    out = _layer_norm(ALPHA * x_ref[...] + pe, g_ref[...], b_ref[...])
    o_ref[...] = out
    ob_ref[...] = out.astype(BF16)


def _ple_ln(xb, wg, p, wp, layer, x, g, b, tm):
    m = x.shape[0]
    row = lambda width: pl.BlockSpec((tm, width), lambda i: (i, 0))
    full = lambda r, c: pl.BlockSpec((r, c), lambda i: (0, 0))
    return pl.pallas_call(
        _ple_ln_kernel,
        grid=(m // tm,),
        in_specs=[row(D_MODEL), _resident_weight(D_MODEL, D_MODEL, layer), row(PLE_DIM),
                  _resident_weight(PLE_DIM, D_MODEL, layer), row(D_MODEL), full(1, D_MODEL), full(1, D_MODEL)],
        out_specs=[row(D_MODEL), row(D_MODEL)],
        out_shape=[jax.ShapeDtypeStruct((m, D_MODEL), F32), jax.ShapeDtypeStruct((m, D_MODEL), BF16)],
        compiler_params=_params("parallel"),
        name="ple_ln",
    )(xb, wg, p, wp, x, g, b)


def _conv_silu(a, a1, a2, u, cw_ref, cb_ref):
    ac = cb_ref[...] + ((a2 * cw_ref[0:1, :] + a1 * cw_ref[1:2, :]) + a * cw_ref[2:3, :])
    return (ac * _sigmoid(ac)) * u


def _ffn_up_seq_kernel(x_ref, wg_ref, wu_ref, cw_ref, cb_ref, h_ref, st_ref, carry_ref, *, tiles_per_seq):
    i = pl.program_id(1)
    x = x_ref[...]
    a = _dot(x, wg_ref[...])
    u = _dot(x, wu_ref[...])
    tm = a.shape[0]

    @pl.when(i % tiles_per_seq == 0)
    def _():
        carry_ref[...] = jnp.zeros_like(carry_ref)

    prev = carry_ref[...]
    row = lax.broadcasted_iota(jnp.int32, a.shape, 0)
    a1 = jnp.where(row == 0, prev[7:8, :], pltpu.roll(a, 1, 0))
    a2 = jnp.where(row == 0, prev[6:7, :], jnp.where(row == 1, prev[7:8, :], pltpu.roll(a, 2, 0)))
    h_ref[...] = _conv_silu(a, a1, a2, u, cw_ref, cb_ref).astype(h_ref.dtype)
    tail = a[tm - 8:tm, :]
    carry_ref[...] = tail
    st_ref[...] = tail


def _ffn_up_seq(xb, wg, wu, layer, cw, cb, seq_len, tm, tn):
    m = xb.shape[0]
    tps = seq_len // tm
    return pl.pallas_call(
        functools.partial(_ffn_up_seq_kernel, tiles_per_seq=tps),
        grid=(D_FF_PAD // tn, m // tm),
        in_specs=[pl.BlockSpec((tm, D_MODEL), lambda j, i: (i, 0)),
                  pl.BlockSpec((None, D_MODEL, tn), lambda j, i: (layer, 0, j)),
                  pl.BlockSpec((None, D_MODEL, tn), lambda j, i: (layer, 0, j)),
                  pl.BlockSpec((CONV_W, tn), lambda j, i: (0, j)),
                  pl.BlockSpec((1, tn), lambda j, i: (0, j))],
        out_specs=[pl.BlockSpec((tm, tn), lambda j, i: (i, j)),
                   pl.BlockSpec((None, 8, tn), lambda j, i: (i // tps, 0, j))],
        out_shape=[jax.ShapeDtypeStruct((m, D_FF_PAD), BF16),
                   jax.ShapeDtypeStruct((m // seq_len, 8, D_FF_PAD), F32)],
        scratch_shapes=[pltpu.VMEM((8, tn), F32)],
        compiler_params=_params("arbitrary", "arbitrary"),
        name="ffn_up_conv_prompt",
    )(xb, wg, wu, cw, cb)


def _ffn_up_short_kernel(x_ref, wg_ref, wu_ref, cw_ref, cb_ref, p1_ref, p2_ref, h_ref, a_ref, *, seq_len):
    x = x_ref[...]
    a = _dot(x, wg_ref[...])
    u = _dot(x, wu_ref[...])
    t = lax.broadcasted_iota(jnp.int32, a.shape, 0) % seq_len
    a1 = jnp.where(t == 0, p1_ref[...], pltpu.roll(a, 1, 0))
    a2 = jnp.where(t < 2, p2_ref[...], pltpu.roll(a, 2, 0))
    h_ref[...] = _conv_silu(a, a1, a2, u, cw_ref, cb_ref).astype(h_ref.dtype)
    a_ref[...] = a


def _ffn_up_short(xb, wg, wu, layer, cw, cb, p1, p2, seq_len, tn):
    m = xb.shape[0]
    colb = lambda r: pl.BlockSpec((r, tn), lambda j: (0, j))
    wcol = pl.BlockSpec((None, D_MODEL, tn), lambda j: (layer, 0, j))
    return pl.pallas_call(
        functools.partial(_ffn_up_short_kernel, seq_len=seq_len),
        grid=(D_FF_PAD // tn,),
        in_specs=[pl.BlockSpec((m, D_MODEL), lambda j: (0, 0)), wcol, wcol,
                  colb(CONV_W), colb(1), colb(m), colb(m)],
        out_specs=[colb(m), colb(m)],
        out_shape=[jax.ShapeDtypeStruct((m, D_FF_PAD), BF16), jax.ShapeDtypeStruct((m, D_FF_PAD), F32)],
        compiler_params=_params("parallel"),
        name="ffn_up_conv_sample",
    )(xb, wg, wu, cw, cb, p1, p2)


def _ffn_down_ln_kernel(h_ref, w_ref, x_ref, g_ref, b_ref, o_ref, ob_ref, acc_ref):
    k = pl.program_id(1)

    @pl.when(k == 0)
    def _():
        acc_ref[...] = jnp.zeros_like(acc_ref)

    acc_ref[...] += _dot(h_ref[...], w_ref[...])

    @pl.when(k == pl.num_programs(1) - 1)
    def _():
        out = _layer_norm(ALPHA * x_ref[...] + acc_ref[...], g_ref[...], b_ref[...])
        o_ref[...] = out
        ob_ref[...] = out.astype(BF16)


def _ffn_down_ln(h, w, layer, x, g, b, tm, tk):
    m = x.shape[0]
    row = pl.BlockSpec((tm, D_MODEL), lambda i, k: (i, 0))
    vec = pl.BlockSpec((1, D_MODEL), lambda i, k: (0, 0))
    return pl.pallas_call(
        _ffn_down_ln_kernel,
        grid=(m // tm, D_FF_PAD // tk),
        in_specs=[pl.BlockSpec((tm, tk), lambda i, k: (i, k)),
                  pl.BlockSpec((None, tk, D_MODEL), lambda i, k: (layer, k, 0)), row, vec, vec],
        out_specs=[row, row],
        out_shape=[jax.ShapeDtypeStruct((m, D_MODEL), F32), jax.ShapeDtypeStruct((m, D_MODEL), BF16)],
        scratch_shapes=[pltpu.VMEM((tm, D_MODEL), F32)],
        compiler_params=_params("parallel", "arbitrary"),
        name="ffn_down_ln",
    )(h, w, x, g, b)


def _rope128(x, c, s_up, s_dn, shift):
    y = x * c + pltpu.roll(x, 128 - shift, 1) * s_up
    if s_dn is not None:
        y = y + pltpu.roll(x, shift, 1) * s_dn
    return y


def _rope_split_kernel(aq_ref, ak_ref, av_ref, bq_ref, bk_ref, cq_ref, ckv_ref, cw_ref,
                       ca_ref, sau_ref, sad_ref, cb_ref, sb_ref, cc_ref, scu_ref, scd_ref,
                       qa_ref, nd_ref, kva_ref, qb_ref, kb_ref, qc_ref, nn_ref, sw_ref, nw_ref, kcm_ref, vcm_ref):
    ca, sau, sad = ca_ref[...], sau_ref[...], sad_ref[...]
    cb, sb = cb_ref[...], sb_ref[...]
    cc, scu, scd = cc_ref[...], scu_ref[...], scd_ref[...]
    rot_a = DK_A // 8
    rot_c = DK_C // 8
    tm = aq_ref.shape[0]
    for h in range(4):
        sl = slice(h * 128, (h + 1) * 128)
        qa_ref[:, sl] = _rope128(aq_ref[:, sl], ca, sau, sad, rot_a).astype(BF16)
        ka = _rope128(ak_ref[:, sl], ca, sau, sad, rot_a)
        av = av_ref[:, sl]
        nd_ref[pl.ds(h, tm, stride=2 * H_A), :] = ka
        nd_ref[pl.ds(H_A + h, tm, stride=2 * H_A), :] = av
        kva_ref[:, sl] = ka.astype(BF16)
        kva_ref[:, 512 + h * 128:512 + (h + 1) * 128] = av.astype(BF16)
        qb_ref[:, sl] = _rope128(bq_ref[:, sl], cb, sb, None, DK_B // 2).astype(BF16)
        kb_ref[:, sl] = _rope128(bk_ref[:, sl], cb, sb, None, DK_B // 2) * (DK_B ** -0.5)
        qc_ref[:, sl] = _rope128(cq_ref[:, sl], cc, scu, scd, rot_c).astype(BF16)
    k_cmp = _rope128(ckv_ref[:, 0:128], cc, scu, scd, rot_c)
    v_cmp = ckv_ref[:, 128:256]
    k_sel = _rope128(ckv_ref[:, 256:384], cc, scu, scd, rot_c)
    v_sel = ckv_ref[:, 384:512]
    k_win = _rope128(cw_ref[:, 0:128], cc, scu, scd, rot_c)
    v_win = cw_ref[:, 128:256]
    for comp, val in enumerate((k_cmp, v_cmp, k_sel, v_sel)):
        nn_ref[pl.ds(comp, tm, stride=4), :] = val
    sw_ref[:, 0:128] = k_sel.astype(BF16)
    sw_ref[:, 128:256] = v_sel.astype(BF16)
    sw_ref[:, 256:384] = k_win.astype(BF16)
    sw_ref[:, 384:512] = v_win.astype(BF16)
    nw_ref[:, 0:128] = k_win
    nw_ref[:, 128:256] = v_win
    kcm_ref[...] = k_cmp
    vcm_ref[...] = v_cmp


def _rope_split(u, tables, tm):
    m = u.shape[0]
    n_tab = tables[0].shape[0] // tm
    ub = lambda width, blk: pl.BlockSpec((tm, width), lambda i: (i, blk))
    tab = pl.BlockSpec((tm, 128), lambda i: (i % n_tab, 0))
    out = lambda width: pl.BlockSpec((tm, width), lambda i: (i, 0))
    shp = lambda width, dt: jax.ShapeDtypeStruct((m, width), dt)
    return pl.pallas_call(
        _rope_split_kernel,
        grid=(m // tm,),
        in_specs=[ub(512, COL_AQ // 512), ub(512, COL_AK // 512), ub(512, COL_AV // 512),
                  ub(512, COL_BQ // 512), ub(512, COL_BK // 512), ub(512, COL_CQ // 512),
                  ub(512, COL_CKV // 512), ub(256, COL_CWIN // 256)] + [tab] * 8,
        out_specs=[out(512), pl.BlockSpec((tm * 2 * H_A, DV_A), lambda i: (i, 0)), out(1024), out(512), out(512),
                   out(512), pl.BlockSpec((tm * 4, DK_C), lambda i: (i, 0)), out(512), out(256), out(128), out(128)],
        out_shape=[shp(512, BF16), jax.ShapeDtypeStruct((m * 2 * H_A, DV_A), F32), shp(1024, BF16), shp(512, BF16),
                   shp(512, F32), shp(512, BF16), jax.ShapeDtypeStruct((m * 4, DK_C), F32), shp(512, BF16),
                   shp(256, F32), shp(128, F32), shp(128, F32)],
        compiler_params=_params("parallel"),
        name="rope_split",
    )(u, u, u, u, u, u, u, u, *tables)


def _rope_tables(pos):
    posf = pos.astype(F32)[:, None]
    lane = jnp.arange(128)

    def cs(half, theta):
        inv_freq = jnp.exp(-math.log(theta) * jnp.arange(half, dtype=F32) / half)
        ang = posf * inv_freq[None, :]
        return jnp.cos(ang), jnp.sin(ang)

    def partial_tables(head_dim, half, theta):
        cos, sin = cs(half, theta)
        d = lane % head_dim
        lo = d < half
        hi = (d >= half) & (d < 2 * half)
        idx = jnp.where(lo, d, jnp.where(hi, d - half, 0))
        c = jnp.where((lo | hi)[None, :], cos[:, idx], 1.0)
        s_up = jnp.where(lo[None, :], -sin[:, idx], 0.0)
        s_dn = jnp.where(hi[None, :], sin[:, idx], 0.0)
        return c, s_up, s_dn

    ca, sau, sad = partial_tables(DK_A, DK_A // 8, ROPE_THETA)
    cc, scu, scd = partial_tables(DK_C, DK_C // 8, ROPE_THETA)
    cosb, sinb = cs(DK_B // 2, RET_THETA)
    cb = jnp.concatenate([cosb, cosb], axis=1)
    sb = jnp.concatenate([-sinb, sinb], axis=1)
    return (ca, sau, sad, cb, sb, cc, scu, scd)


def _rms_head(o, gain, post_scale):
    return o * lax.rsqrt(jnp.mean(o * o, axis=-1, keepdims=True) + EPS) * gain * post_scale


def _diff_attn_prompt_kernel(lam_ref, q_ref, k_ref, v_ref, gain_ref, o_ref, *, tq, post_scale):
    qi = pl.program_id(2)
    t = k_ref.shape[0]
    n_head = q_ref.shape[1] // 128
    lane = lax.broadcasted_iota(jnp.int32, (tq, 128), 1)
    scale = DK_A ** -0.5
    diag = lax.broadcasted_iota(jnp.int32, (tq, tq), 1) <= lax.broadcasted_iota(jnp.int32, (tq, tq), 0)

    def tile_terms(qh, hs, lo):
        s_d = jnp.where(diag, _dot_nt(qh, k_ref[lo:lo + tq, hs]), NEG)
        mx = jnp.max(s_d, axis=-1, keepdims=True)
        if lo:
            s_lo = _dot_nt(qh, k_ref[0:lo, hs])
            mx = jnp.maximum(mx, jnp.max(s_lo, axis=-1, keepdims=True))
        e_d = jnp.exp(s_d - mx)
        l = jnp.sum(e_d, axis=-1, keepdims=True)
        o = _dot(e_d.astype(BF16), v_ref[lo:lo + tq, hs])
        if lo:
            e_lo = jnp.exp(s_lo - mx)
            l = l + jnp.sum(e_lo, axis=-1, keepdims=True)
            o = o + _dot(e_lo.astype(BF16), v_ref[0:lo, hs])
        return o, l

    for qv in range(t // tq):
        @pl.when(qi == qv)
        def _(lo=qv * tq):
            for hh in range(n_head):
                hs = slice(hh * 128, (hh + 1) * 128)
                q = q_ref[:, hs]
                zero = jnp.zeros_like(q)
                o1, l1 = tile_terms(jnp.where(lane < DK_A, q, zero) * scale, hs, lo)
                o2, l2 = tile_terms(jnp.where(lane >= DK_A, q, zero) * scale, hs, lo)
                o = o1 * (1.0 / l1) - (lam_ref[0] / l2) * o2
                o_ref[:, hs] = _rms_head(o, gain_ref[...], post_scale).astype(o_ref.dtype)


def _key_buckets(t, tq):
    n_bucket = 8
    return n_bucket if t % (n_bucket * 128) == 0 and (t // n_bucket) % tq == 0 else 1


def _diff_attn_prompt(lam, qa, kva, gain, b, t, tq, post_scale):
    m = qa.shape[0]
    nq = t // tq
    return pl.pallas_call(
        functools.partial(_diff_attn_prompt_kernel, tq=tq, post_scale=post_scale),
        grid=(b, H_A // 2, nq),
        in_specs=[pl.BlockSpec(memory_space=pltpu.SMEM),
                  pl.BlockSpec((tq, 256), lambda bi, h, qi: (bi * nq + qi, h)),
                  pl.BlockSpec((t, 256), lambda bi, h, qi: (bi, h)),
                  pl.BlockSpec((t, 256), lambda bi, h, qi: (bi, H_A // 2 + h)),
                  pl.BlockSpec((1, 128), lambda bi, h, qi: (0, 0))],
        out_specs=pl.BlockSpec((tq, 256), lambda bi, h, qi: (bi * nq + qi, h)),
        out_shape=jax.ShapeDtypeStruct((m, H_A * DV_A), BF16),
        compiler_params=_params("parallel", "parallel", "parallel"),
        name="diff_attn_prompt",
    )(lam, qa, kva, kva, gain)


def _diff_attn_paged_kernel(pt_ref, lam_ref, q_ref, *rest, n_group, n_new, post_scale):
    page_refs = rest[:n_group]
    kvn_ref, gain_ref, o_ref, m_ref, l_ref, acc_ref = rest[n_group:]
    j = pl.program_id(1)
    scale = DK_A ** -0.5
    q = q_ref[...]
    hk = H_A * 2 * DK_A

    @pl.when(j == 0)
    def _():
        m_ref[...] = jnp.full_like(m_ref, NEG)
        l_ref[...] = jnp.zeros_like(l_ref)
        acc_ref[...] = jnp.zeros_like(acc_ref)

    def update(s, v):
        m_old = m_ref[...]
        m_new = jnp.maximum(m_old, jnp.max(s, axis=-1, keepdims=True))
        a = jnp.exp(m_old - m_new)
        p = jnp.exp(s - m_new[:, 0:1])
        l_ref[...] = a * l_ref[...] + jnp.sum(p, axis=-1, keepdims=True)
        acc_ref[...] = a[:, 0:1] * acc_ref[...] + _dot(p.astype(BF16), v)
        m_ref[...] = m_new

    def heads(r, first):
        return jnp.concatenate([r[pl.ds(first + h, PAGE, stride=2 * H_A), :].astype(BF16) for h in range(H_A)],
                               axis=1)

    k = jnp.concatenate([heads(r, 0) for r in page_refs], axis=0)
    v = jnp.concatenate([heads(r, H_A) for r in page_refs], axis=0)
    update(_dot_nt(q, k) * scale, v)

    @pl.when(j == pl.num_programs(1) - 1)
    def _():
        pad = jnp.zeros((PAGE - n_new, 2 * hk), F32)
        kvn = jnp.concatenate([kvn_ref[...], pad], axis=0)
        s = _dot_nt(q, kvn[:, 0:hk].astype(BF16)) * scale
        tq = lax.broadcasted_iota(jnp.int32, s.shape, 0) % n_new
        tk = lax.broadcasted_iota(jnp.int32, s.shape, 1)
        update(jnp.where(tk <= tq, s, NEG), kvn[:, hk:2 * hk].astype(BF16))
        o = acc_ref[...] / l_ref[:, 0:1]
        lam = lam_ref[0]
        for h in range(H_A):
            r0 = h * 2 * n_new
            sl = slice(h * DV_A, (h + 1) * DV_A)
            oh = o[r0:r0 + n_new, sl] - lam * o[r0 + n_new:r0 + 2 * n_new, sl]
            o_ref[:, sl] = _rms_head(oh, gain_ref[...], post_scale)


def _diff_attn_paged(page_table, lam, q_exp, cache, layer, kv_new, gain, n_group, post_scale):
    b, n_pages = page_table.shape
    n_new = kv_new.shape[1]
    rows = q_exp.shape[1]
    width = kv_new.shape[2]
    pt = page_table.reshape(-1)

    def page_spec(g):
        return pl.BlockSpec((None, None, PAGE * 2 * H_A, DV_A),
                            lambda bi, j, pt_ref: (layer, pt_ref[bi * n_pages + j * n_group + g], 0, 0))

    grid_spec = pltpu.PrefetchScalarGridSpec(
        num_scalar_prefetch=1,
        grid=(b, n_pages // n_group),
        in_specs=[pl.BlockSpec(memory_space=pltpu.SMEM),
                  pl.BlockSpec((None, rows, q_exp.shape[2]), lambda bi, j, pt_ref: (bi, 0, 0))]
                 + [page_spec(g) for g in range(n_group)]
                 + [pl.BlockSpec((None, n_new, width), lambda bi, j, pt_ref: (bi, 0, 0)),
                    pl.BlockSpec((1, DV_A), lambda bi, j, pt_ref: (0, 0))],
        out_specs=pl.BlockSpec((None, n_new, H_A * DV_A), lambda bi, j, pt_ref: (bi, 0, 0)),
        scratch_shapes=[pltpu.VMEM((rows, 128), F32), pltpu.VMEM((rows, 128), F32),
                        pltpu.VMEM((rows, H_A * DV_A), F32)],
    )
    return pl.pallas_call(
        functools.partial(_diff_attn_paged_kernel, n_group=n_group, n_new=n_new, post_scale=post_scale),
        grid_spec=grid_spec,
        out_shape=jax.ShapeDtypeStruct((b, n_new, H_A * DV_A), F32),
        compiler_params=_params("parallel", "arbitrary"),
        name="diff_attn_paged",
    )(pt, lam, q_exp, *([cache] * n_group), kv_new, gain)


def _retention_kernel(q_ref, k_ref, v0_ref, v1_ref, g0_ref, g1_ref, s0_ref, dm_ref, qd_ref, kd_ref, cd_ref,
                      gain_ref, o_ref, sf_ref, s_ref):
    c = pl.program_id(1)

    @pl.when(c == 0)
    def _():
        s_ref[...] = s0_ref[...]

    for h in range(H_B):
        ks = slice(h * DK_B, (h + 1) * DK_B)
        half = slice((h % 2) * DV_B, (h % 2 + 1) * DV_B)
        v_ref, g_ref = (v0_ref, g0_ref) if h < 2 else (v1_ref, g1_ref)
        q = q_ref[:, ks]
        k = k_ref[:, ks]
        v = v_ref[:, half].astype(BF16)
        s = s_ref[h]
        att = _dot_nt(q, k.astype(BF16)) * dm_ref[h]
        o = _dot(att.astype(BF16), v) + _dot(q, s.astype(BF16)) * qd_ref[h]
        s_new = s * cd_ref[h] + _dot_tn((k * kd_ref[h]).astype(BF16), v)
        s_ref[h] = s_new
        sf_ref[h] = s_new
        mu = jnp.mean(o, axis=-1, keepdims=True)
        d = o - mu
        var = jnp.mean(d * d, axis=-1, keepdims=True)
        y = d * lax.rsqrt(var + EPS) * gain_ref[:, h * DV_B:(h + 1) * DV_B]
        g = g_ref[:, half]
        o_ref[:, h * DV_B:(h + 1) * DV_B] = (y * (g * _sigmoid(g))).astype(o_ref.dtype)


def _retention(q, k, vg, v_blk, g_blk, s0, tables, gain, b, nc, chunk):
    m = q.shape[0]
    dmask, qdec, kdec, cdec = tables
    row = lambda width, blk: pl.BlockSpec((chunk, width), lambda bi, c: (bi * nc + c, blk))
    full = lambda a: pl.BlockSpec(a.shape, lambda bi, c: (0,) * a.ndim)
    state = pl.BlockSpec((None, H_B, DK_B, DV_B), lambda bi, c: (bi, 0, 0, 0))
    return pl.pallas_call(
        _retention_kernel,
        grid=(b, nc),
        in_specs=[row(H_B * DK_B, 0), row(H_B * DK_B, 0),
                  row(2 * DV_B, v_blk), row(2 * DV_B, v_blk + 1), row(2 * DV_B, g_blk), row(2 * DV_B, g_blk + 1),
                  state, full(dmask), full(qdec), full(kdec), full(cdec), full(gain)],
        out_specs=[row(H_B * DV_B, 0), state],
        out_shape=[jax.ShapeDtypeStruct((m, H_B * DV_B), BF16),
                   jax.ShapeDtypeStruct((b, H_B, DK_B, DV_B), F32)],
        scratch_shapes=[pltpu.VMEM((H_B, DK_B, DV_B), F32)],
        compiler_params=_params("parallel", "arbitrary"),
        name="retention",
    )(q, k, vg, vg, vg, vg, s0, dmask, qdec, kdec, cdec, gain)


def _retention_tables(chunk, rows):
    log_g = jnp.log(1.0 - jnp.exp2(-5.0 - jnp.arange(H_B, dtype=F32)))
    idx = jnp.arange(chunk, dtype=F32)
    rel = idx[:, None] - idx[None, :]
    dmask = jnp.where(rel >= 0, jnp.exp(log_g[:, None, None] * jnp.maximum(rel, 0.0)), 0.0)
    q_dec = jnp.exp(log_g[:, None] * (idx[None, :] + 1.0))
    k_dec = jnp.exp(log_g[:, None] * (chunk - 1.0 - idx[None, :]))
    c_dec = jnp.exp(log_g * chunk)
    padn = rows - chunk
    dmask = jnp.pad(dmask, ((0, 0), (0, padn), (0, padn)))
    q_dec = jnp.pad(q_dec, ((0, 0), (0, padn)))
    k_dec = jnp.pad(k_dec, ((0, 0), (0, padn)))
    qd = jnp.broadcast_to(q_dec[:, :, None], (H_B, rows, DV_B))
    kd = jnp.broadcast_to(k_dec[:, :, None], (H_B, rows, DK_B))
    cd = jnp.broadcast_to(c_dec[:, None, None], (H_B, 1, DV_B))
    return dmask, qd, kd, cd


def _compress_kernel(x_ref, pe_ref, w_ref, o_ref):
    o_ref[...] = _dot((x_ref[...] + pe_ref[...]).astype(BF16), w_ref[...])


def _compress(x, pe, w, layer):
    nb = x.shape[1]
    kdim = CMP_BLOCK * DK_C
    return pl.pallas_call(
        _compress_kernel,
        grid=(2,),
        in_specs=[pl.BlockSpec((None, nb, kdim), lambda i: (i, 0, 0)),
                  pl.BlockSpec((None, 1, kdim), lambda i: (i, 0, 0)),
                  pl.BlockSpec((None, None, kdim, DK_C), lambda i: (layer, i, 0, 0))],
        out_specs=pl.BlockSpec((None, nb, DK_C), lambda i: (i, 0, 0)),
        out_shape=jax.ShapeDtypeStruct((2, nb, DK_C), F32),
        compiler_params=_params("parallel"),
        name="nsa_compress_prompt",
    )(x, pe, w)


def _cmp_attention(qs, qpos, kce, kco, vce, vco):
    scale = DK_C ** -0.5
    nh = kce.shape[0]
    n = lax.broadcasted_iota(jnp.int32, (1, nh), 1)
    me = (2 * CMP_BLOCK * n + (CMP_BLOCK - 1)) <= qpos
    mo = (2 * CMP_BLOCK * n + (2 * CMP_BLOCK - 1)) <= qpos
    se = jnp.where(me, _dot_nt(qs, kce) * scale, NEG)
    so = jnp.where(mo, _dot_nt(qs, kco) * scale, NEG)
    mx = jnp.maximum(jnp.max(se, axis=-1, keepdims=True), jnp.max(so, axis=-1, keepdims=True))
    ee = jnp.where(me, jnp.exp(se - mx), 0.0)
    eo = jnp.where(mo, jnp.exp(so - mx), 0.0)
    den = jnp.maximum(jnp.sum(ee, axis=-1, keepdims=True) + jnp.sum(eo, axis=-1, keepdims=True), TINY)
    pe = ee / den
    po = eo / den
    o = _dot(pe.astype(BF16), vce) + _dot(po.astype(BF16), vco)
    return o, pe, po


def _top_blocks(score, n_pick):
    blk = lax.broadcasted_iota(jnp.int32, score.shape, 1).astype(F32)
    big = 1e9
    work = score
    sel = jnp.zeros(score.shape, F32)
    idxs, vals = [], []
    for _ in range(n_pick):
        mval = jnp.max(work, axis=-1, keepdims=True)
        idx = jnp.min(jnp.where(work == mval, blk, big), axis=-1, keepdims=True)
        pick = blk == idx
        sel = jnp.where(pick, jnp.where(mval >= 0.0, 1.0, 0.0), sel)
        work = jnp.where(pick, -2.0, work)
        idxs.append(idx.astype(jnp.int32))
        vals.append(mval)
    return sel, idxs, vals


def _block_scores(imp, qpos):
    blk = lax.broadcasted_iota(jnp.int32, imp.shape, 1)
    cur = lax.shift_right_arithmetic(qpos, int(math.log2(SEL_BLOCK)))
    forced = jnp.where(blk == 0, 1, 0) + jnp.where(blk == cur, 1, 0) + jnp.where(blk == cur - 1, 1, 0)
    valid = blk * SEL_BLOCK <= qpos
    return jnp.where(valid, jnp.where(forced > 0, FORCED_SCORE, imp), -1.0)


def _nsa_prompt_kernel(q_ref, kce_ref, kco_ref, vce_ref, vco_ref, sw_ref, g_ref, o_ref, osel_ref, *,
                       tq, n_bucket, n_sel):
    qi = pl.program_id(1)
    scale = DK_C ** -0.5
    q = q_ref[...]
    qs = jnp.concatenate([q[:, h * DK_C:(h + 1) * DK_C] for h in range(H_C)], axis=0)
    qpos1 = qi * tq + lax.broadcasted_iota(jnp.int32, (tq, 1), 0)
    qpos = jnp.concatenate([qpos1] * H_C, axis=0)
    qrow1 = qi * tq + lax.broadcasted_iota(jnp.int32, (1, tq), 1)
    qrow = jnp.concatenate([qrow1] * H_C, axis=1)
    t = sw_ref.shape[0]
    ncp = kce_ref.shape[0]

    n = lax.broadcasted_iota(jnp.int32, (ncp, 1), 0)
    me = (2 * CMP_BLOCK * n + (CMP_BLOCK - 1)) <= qrow
    mo = (2 * CMP_BLOCK * n + (2 * CMP_BLOCK - 1)) <= qrow
    se = jnp.where(me, _dot_nt(kce_ref[...], qs) * scale, NEG)
    so = jnp.where(mo, _dot_nt(kco_ref[...], qs) * scale, NEG)
    mx = jnp.maximum(jnp.max(se, axis=0, keepdims=True), jnp.max(so, axis=0, keepdims=True))
    ee = jnp.where(me, jnp.exp(se - mx), 0.0)
    eo = jnp.where(mo, jnp.exp(so - mx), 0.0)
    den = jnp.maximum(jnp.sum(ee, axis=0, keepdims=True) + jnp.sum(eo, axis=0, keepdims=True), TINY)
    pe = ee / den
    po = eo / den
    o_cmp = _dot(pe.T.astype(BF16), vce_ref[...]) + _dot(po.T.astype(BF16), vco_ref[...])
    spe = pe[:, 0:tq] + pe[:, tq:2 * tq] + pe[:, 2 * tq:3 * tq] + pe[:, 3 * tq:4 * tq]
    spo = po[:, 0:tq] + po[:, tq:2 * tq] + po[:, 2 * tq:3 * tq] + po[:, 3 * tq:4 * tq]

    imp = (spe + spo)[0:n_sel]
    blk = lax.broadcasted_iota(jnp.int32, (n_sel, tq), 0)
    cur = lax.shift_right_arithmetic(qrow1, int(math.log2(SEL_BLOCK)))
    forced = jnp.where(blk == 0, 1, 0) + jnp.where(blk == cur, 1, 0) + jnp.where(blk == cur - 1, 1, 0)
    score = jnp.where(blk * SEL_BLOCK <= qrow1, jnp.where(forced > 0, FORCED_SCORE, imp), -1.0)
    blkf = blk.astype(F32)
    sel_t = jnp.zeros((n_sel, tq), F32)
    for _ in range(min(TOPK, n_sel)):
        mval = jnp.max(score, axis=0, keepdims=True)
        idx = jnp.min(jnp.where(score == mval, blkf, 1e9), axis=0, keepdims=True)
        pick = blkf == idx
        sel_t = jnp.where(pick, jnp.where(mval >= 0.0, 1.0, 0.0), sel_t)
        score = jnp.where(pick, -2.0, score)
    selb = jnp.concatenate([sel_t, jnp.zeros((ncp - n_sel, tq), F32)], axis=0).T.astype(BF16)

    kb = t // n_bucket
    bucket = lax.div((qi + 1) * tq - 1, kb)
    for nb in range(n_bucket):
        @pl.when(bucket == nb)
        def _(nkeys=(nb + 1) * kb):
            key_blk = lax.shift_right_arithmetic(lax.broadcasted_iota(jnp.int32, (ncp, nkeys), 1),
                                                 int(math.log2(SEL_BLOCK)))
            expand = jnp.where(key_blk == lax.broadcasted_iota(jnp.int32, (ncp, nkeys), 0), 1.0, 0.0).astype(BF16)
            kpos = lax.broadcasted_iota(jnp.int32, (1, nkeys), 1)
            selk1 = jnp.where(kpos <= qpos1, _dot(selb, expand), 0.0)
            selk = jnp.concatenate([selk1] * H_C, axis=0) > 0.5
            e, l = _visible_softmax_terms(_dot_nt(qs, sw_ref[0:nkeys, 0:128]) * scale, selk)
            osel_ref[...] = _dot(e.astype(BF16), sw_ref[0:nkeys, 128:256]) * (1.0 / l)
    o_sel = osel_ref[...]

    wlen = min(t, WINDOW + tq)
    start = pl.multiple_of(jnp.clip(qi * tq - WINDOW, 0, t - wlen), 128) if wlen < t else 0
    wpos = start + lax.broadcasted_iota(jnp.int32, (1, wlen), 1)
    d = qpos - wpos
    wmask = jnp.abs(2 * d - WINDOW) <= WINDOW
    e, l = _visible_softmax_terms(_dot_nt(qs, sw_ref[pl.ds(start, wlen), 256:384]) * scale, wmask)
    o_win = _dot(e.astype(BF16), sw_ref[pl.ds(start, wlen), 384:512]) * (1.0 / l)

    g = _sigmoid(g_ref[...])
    for h in range(H_C):
        r = slice(h * tq, (h + 1) * tq)
        oc = (g[:, h:h + 1] * o_cmp[r] + g[:, H_C + h:H_C + h + 1] * o_sel[r]
              + g[:, 2 * H_C + h:2 * H_C + h + 1] * o_win[r])
        o_ref[:, h * DK_C:(h + 1) * DK_C] = oc.astype(o_ref.dtype)


def _nsa_prompt(qc, kce, kco, vce, vco, sw, u, b, t, tq):
    m = qc.shape[0]
    nq = t // tq
    nh = kce.shape[1]
    n_sel = t // SEL_BLOCK
    assert nh == 128 and n_sel <= nh and t % SEL_BLOCK == 0 and 2 * CMP_BLOCK * n_sel >= t
    cm = pl.BlockSpec((None, nh, DK_C), lambda bi, qi: (bi, 0, 0))
    return pl.pallas_call(
        functools.partial(_nsa_prompt_kernel, tq=tq, n_bucket=_key_buckets(t, tq), n_sel=n_sel),
        grid=(b, nq),
        scratch_shapes=[pltpu.VMEM((H_C * tq, DK_C), F32)],
        in_specs=[pl.BlockSpec((tq, H_C * DK_C), lambda bi, qi: (bi * nq + qi, 0)), cm, cm, cm, cm,
                  pl.BlockSpec((t, 512), lambda bi, qi: (bi, 0)),
                  pl.BlockSpec((tq, 128), lambda bi, qi: (bi * nq + qi, COL_CG // 128))],
        out_specs=pl.BlockSpec((tq, H_C * DK_C), lambda bi, qi: (bi * nq + qi, 0)),
        out_shape=jax.ShapeDtypeStruct((m, H_C * DK_C), BF16),
        compiler_params=_params("parallel", "parallel"),
        name="nsa_prompt",
    )(qc, kce, kco, vce, vco, sw, u)


def _compress_paged_kernel(pt_ref, *rest, n_group):
    page_refs = rest[:n_group]
    pe_ref, w_ref, o_ref, xk_ref, xv_ref = rest[n_group:]
    j = pl.program_id(1)
    per_page = PAGE // CMP_BLOCK
    for g, r in enumerate(page_refs):
        for n in range(per_page):
            blk = (j * n_group + g) * per_page + n
            dst = pl.ds(pl.multiple_of(blk * CMP_PITCH, 8), CMP_BLOCK)
            xk_ref[dst, :] = r[pl.ds(n * CMP_BLOCK * 4, CMP_BLOCK, stride=4), :]
            xv_ref[dst, :] = r[pl.ds(n * CMP_BLOCK * 4 + 1, CMP_BLOCK, stride=4), :]

    @pl.when(j == pl.num_programs(1) - 1)
    def _():
        nb = xk_ref.shape[0] // CMP_PITCH
        acc_k = jnp.zeros((nb, DK_C), F32)
        acc_v = jnp.zeros((nb, DK_C), F32)
        for i in range(0, CMP_BLOCK, 2):
            def pair(x_ref, c):
                return jnp.concatenate(
                    [x_ref[pl.ds(i + d, nb, stride=CMP_PITCH), :] + pe_ref[c, i + d:i + d + 1, :] for d in (0, 1)],
                    axis=1).astype(BF16)
            wsl = slice(i * DK_C, (i + 2) * DK_C)
            acc_k = acc_k + _dot(pair(xk_ref, 0), w_ref[0, wsl, :])
            acc_v = acc_v + _dot(pair(xv_ref, 1), w_ref[1, wsl, :])
        o_ref[0] = acc_k
        o_ref[1] = acc_v


def _compress_paged(page_table, cache, layer, pe, w, n_group):
    b, n_pages = page_table.shape
    pt = page_table.reshape(-1)
    nb = n_pages * PAGE // CMP_BLOCK

    def page_spec(g):
        return pl.BlockSpec((None, None, PAGE * 4, DK_C),
                            lambda bi, j, pt_ref: (layer, pt_ref[bi * n_pages + j * n_group + g], 0, 0))

    grid_spec = pltpu.PrefetchScalarGridSpec(
        num_scalar_prefetch=1,
        grid=(b, n_pages // n_group),
        in_specs=[page_spec(g) for g in range(n_group)]
                 + [pl.BlockSpec((2, CMP_BLOCK, DK_C), lambda bi, j, pt_ref: (0, 0, 0)),
                    pl.BlockSpec((None, 2, CMP_BLOCK * DK_C, DK_C), lambda bi, j, pt_ref: (layer, 0, 0, 0))],
        out_specs=pl.BlockSpec((None, 2, nb, DK_C), lambda bi, j, pt_ref: (bi, 0, 0, 0)),
        scratch_shapes=[pltpu.VMEM((nb * CMP_PITCH, DK_C), F32), pltpu.VMEM((nb * CMP_PITCH, DK_C), F32)],
    )
    return pl.pallas_call(
        functools.partial(_compress_paged_kernel, n_group=n_group),
        grid_spec=grid_spec,
        out_shape=jax.ShapeDtypeStruct((b, 2, nb, DK_C), F32),
        compiler_params=_params("parallel", "arbitrary"),
        name="nsa_compress_paged",
    )(pt, *([cache] * n_group), pe, w)


def _nsa_sample_select_kernel(q_ref, kce_ref, kco_ref, vce_ref, vco_ref, o_ref, idx_ref, *, n_new, past_len, n_pick):
    qs = q_ref[...]
    rows = qs.shape[0]
    qpos = past_len + lax.broadcasted_iota(jnp.int32, (rows, 1), 0) % n_new
    o_cmp, pe, po = _cmp_attention(qs, qpos, kce_ref[...], kco_ref[...], vce_ref[...], vco_ref[...])
    o_ref[...] = o_cmp
    spe = pe[0:n_new]
    spo = po[0:n_new]
    for h in range(1, H_C):
        spe = spe + pe[h * n_new:(h + 1) * n_new]
        spo = spo + po[h * n_new:(h + 1) * n_new]
    score = _block_scores(spe + spo, qpos[0:n_new])
    _, idxs, _ = _top_blocks(score, n_pick)
    lane = lax.broadcasted_iota(jnp.int32, (n_new, 128), 1)
    out = jnp.zeros((n_new, 128), jnp.int32)
    for r, idx in enumerate(idxs):
        out = jnp.where(lane == r, idx, out)
    idx_ref[...] = out


def _nsa_sample_select(qs, kce, kco, vce, vco, n_new, past_len, n_pick):
    b, rows, _ = qs.shape
    nh = kce.shape[1]
    cm = pl.BlockSpec((None, nh, DK_C), lambda bi: (bi, 0, 0))
    return pl.pallas_call(
        functools.partial(_nsa_sample_select_kernel, n_new=n_new, past_len=past_len, n_pick=n_pick),
        grid=(b,),
        in_specs=[pl.BlockSpec((None, rows, DK_C), lambda bi: (bi, 0, 0)), cm, cm, cm, cm],
        out_specs=[pl.BlockSpec((None, rows, DK_C), lambda bi: (bi, 0, 0)),
                   pl.BlockSpec((None, n_new, 128), lambda bi: (bi, 0, 0))],
        out_shape=[jax.ShapeDtypeStruct((b, rows, DK_C), F32), jax.ShapeDtypeStruct((b, n_new, 128), jnp.int32)],
        compiler_params=_params("parallel"),
        name="nsa_sample_select",
    )(qs, kce, kco, vce, vco)


def _nsa_sample_attend_kernel(pt_ref, ix_ref, q_ref, *rest, n_pick, n_new, past_len):
    blk_refs = rest[:n_pick]
    new_ref, wst_ref, wnew_ref, ocmp_ref, g_ref, o_ref = rest[n_pick:]
    bi = pl.program_id(0)
    qi = pl.program_id(1)
    scale = DK_C ** -0.5
    q = q_ref[...].astype(BF16)
    qpos = past_len + qi
    lane_blk = lax.broadcasted_iota(jnp.int32, (1, SEL_BLOCK), 1)

    ks = [r[pl.ds(2, SEL_BLOCK, stride=4), :].astype(BF16) for r in blk_refs]
    vs = [r[pl.ds(3, SEL_BLOCK, stride=4), :].astype(BF16) for r in blk_refs]
    kpos = [ix_ref[(bi * n_new + qi) * TOPK + r] * SEL_BLOCK + lane_blk for r in range(n_pick)]
    new = jnp.concatenate([new_ref[...], jnp.zeros((SEL_BLOCK - n_new, 2 * DK_C), F32)], axis=0)
    ks.append(new[:, 0:DK_C].astype(BF16))
    vs.append(new[:, DK_C:2 * DK_C].astype(BF16))
    kpos.append(past_len + lane_blk)
    s = _dot_nt(q, jnp.concatenate(ks, axis=0)) * scale
    mask = jnp.concatenate(kpos, axis=1) <= qpos
    o_sel = _dot(_masked_softmax(s, mask).astype(BF16), jnp.concatenate(vs, axis=0))

    nbuf = wst_ref.shape[0]
    wnew = jnp.concatenate([wnew_ref[...], jnp.zeros((PAGE - n_new, 2 * DK_C), F32)], axis=0)
    kw = jnp.concatenate([wst_ref[:, 0:DK_C].astype(BF16), wnew[:, 0:DK_C].astype(BF16)], axis=0)
    vw = jnp.concatenate([wst_ref[:, DK_C:2 * DK_C].astype(BF16), wnew[:, DK_C:2 * DK_C].astype(BF16)], axis=0)
    wpos = past_len - nbuf + lax.broadcasted_iota(jnp.int32, (1, nbuf + PAGE), 1)
    d = qpos - wpos
    wmask = jnp.where(wpos >= 0, jnp.abs(2 * d - WINDOW), 4 * WINDOW) <= WINDOW
    s = _dot_nt(q, kw) * scale
    o_win = _dot(_masked_softmax(s, wmask).astype(BF16), vw)

    g = _sigmoid(g_ref[...])
    o_ref[...] = g[0] * ocmp_ref[...] + g[1] * o_sel + g[2] * o_win


def _nsa_sample_attend(page_table, idx, q8, cache, layer, nsa_new, win_state, win_new, ocmp8, gexp,
                       n_pick, past_len):
    b, n_pages = page_table.shape
    n_new = q8.shape[1]
    pt = page_table.reshape(-1)
    nbuf = win_state.shape[1]
    per_page_log2 = int(math.log2(PAGE // SEL_BLOCK))

    def blk_spec(r):
        def imap(bi, qi, pt_ref, ix_ref):
            blk = ix_ref[(bi * n_new + qi) * TOPK + r]
            page = pt_ref[bi * n_pages + lax.shift_right_logical(blk, per_page_log2)]
            return (layer, page, jnp.bitwise_and(blk, (1 << per_page_log2) - 1), 0)
        return pl.BlockSpec((None, None, SEL_BLOCK * 4, DK_C), imap)

    per_q = lambda: pl.BlockSpec((None, None, 8, DK_C), lambda bi, qi, p, x: (bi, qi, 0, 0))
    grid_spec = pltpu.PrefetchScalarGridSpec(
        num_scalar_prefetch=2,
        grid=(b, n_new),
        in_specs=[per_q()] + [blk_spec(r) for r in range(n_pick)]
                 + [pl.BlockSpec((None, n_new, 2 * DK_C), lambda bi, qi, p, x: (bi, 0, 1)),
                    pl.BlockSpec((None, nbuf, 2 * DK_C), lambda bi, qi, p, x: (bi, 0, 0)),
                    pl.BlockSpec((None, n_new, 2 * DK_C), lambda bi, qi, p, x: (bi, 0, 0)),
                    per_q(),
                    pl.BlockSpec((None, None, 3, 8, DK_C), lambda bi, qi, p, x: (bi, qi, 0, 0, 0))],
        out_specs=per_q(),
    )
    return pl.pallas_call(
        functools.partial(_nsa_sample_attend_kernel, n_pick=n_pick, n_new=n_new, past_len=past_len),
        grid_spec=grid_spec,
        out_shape=jax.ShapeDtypeStruct((b, n_new, 8, DK_C), F32),
        compiler_params=_params("parallel", "parallel"),
        name="nsa_sample_attend",
    )(pt, idx, q8, *([cache] * n_pick), nsa_new, win_state, win_new, ocmp8, gexp)


def _tile(m, pref):
    if m <= pref:
        return m
    t = pref - pref % 128
    while m % t:
        t -= 128
    assert t > 0, (m, pref)
    return t


def _lambda_scalar(lp, layer):
    lam_init = 0.8 - 0.6 * math.exp(-0.3 * layer)
    lam = jnp.exp(jnp.sum(lp[0] * lp[1])) - jnp.exp(jnp.sum(lp[2] * lp[3])) + lam_init
    return lam.reshape(1).astype(F32), 1.0 - lam_init


def _dense_tail(x, xb, oa, ob, oc, p, w, layer, seq_len, conv_prev):
    m = x.shape[0]
    short = seq_len < 128
    mix = _merge(xb, w['w_mg'], oa, ob, oc, w['wa'], w['wb'], w['wc'], layer, _tile(m, 512), 512)
    x1, x1b = _proj_ln(mix, w['w_out'], layer, x, w['g0'], w['b0'], _tile(m, 512))
    if short:
        p1, p2 = conv_prev
        h, a = _ffn_up_short(x1b, w['w_fg'], w['w_fu'], layer, w['cw'], w['cb'], p1, p2, seq_len, 512)
        conv = a.reshape(m // seq_len, seq_len, D_FF_PAD)[:, seq_len - (CONV_W - 1):, :D_FF]
    else:
        h, st = _ffn_up_seq(x1b, w['w_fg'], w['w_fu'], layer, w['cw'], w['cb'], seq_len, _tile(seq_len, 2048), 512)
        conv = st[:, 8 - (CONV_W - 1):, :D_FF]
    x2, x2b = _ffn_down_ln(h, w['w_fd'], layer, x1, w['g1'], w['b1'], _tile(m, 512), D_FF_PAD // 4)
    x3, x3b = _ple_ln(x2b, w['w_pg'], p, w['w_pp'], layer, x2, w['g2'], w['b2'], _tile(m, 512))
    return x3, x3b, conv


def _prompt_layer(x, xb, p, w, layer, b, t, rope_tabs, ret_tabs):
    m = b * t
    u = _matmul(xb, w['w_in'], layer, F32, _tile(m, 1024), 1024, "in_proj")
    (qa, new_diff, kva, qb, kb, qc, new_nsa, sw, new_win, kcm, vcm) = _rope_split(u, rope_tabs, _tile(t, 256))

    lam, post = _lambda_scalar(w['diff_lambda'], layer)
    oa = _diff_attn_prompt(lam, qa, kva, w['diff_gain'], b, t, _tile(t, 256), post)

    chunk = RET_CHUNK
    s0 = jnp.zeros((b, H_B, DK_B, DV_B), F32)
    ob, ret = _retention(qb, kb, u, COL_BV // 512, COL_BG // 512, s0, ret_tabs, w['ret_gain'], b, t // chunk, chunk)

    nb = m // CMP_BLOCK
    xc = jnp.stack([kcm.reshape(nb, CMP_BLOCK * DK_C), vcm.reshape(nb, CMP_BLOCK * DK_C)])
    kvc = _compress(xc, w['cmp_pe'].reshape(2, 1, CMP_BLOCK * DK_C), w['cmp_w'], layer)
    kvc = kvc.reshape(2, b, t // CMP_BLOCK, DK_C).astype(BF16)
    half = lambda a: jnp.pad(a, ((0, 0), (0, 128 - a.shape[1]), (0, 0)))
    oc = _nsa_prompt(qc, half(kvc[0, :, 0::2]), half(kvc[0, :, 1::2]), half(kvc[1, :, 0::2]),
                     half(kvc[1, :, 1::2]), sw, u, b, t, _tile(t, 128))

    x3, x3b, conv = _dense_tail(x, xb, oa, ob, oc, p, w, layer, t, None)
    nwin = min(WINDOW, t)
    states = (new_diff.reshape(b, t, 2, H_A, DV_A), new_nsa.reshape(b, t, 4, DK_C),
              new_win.reshape(b, t, 2, DK_C)[:, t - nwin:], ret, conv)
    return x3, x3b, states


def _sample_layer(x, xb, p, w, layer, b, t, past_len, rope_tabs, ret_tabs, page_table,
                  cache_diff, cache_nsa, win_state, ret_state, conv_state):
    m = b * t
    u = _matmul(xb, w['w_in'], layer, F32, m, 1024, "in_proj")
    (qa, new_diff, kva, qb, kb, qc, new_nsa, sw, new_win, kcm, vcm) = _rope_split(u, rope_tabs, m)

    lam, post = _lambda_scalar(w['diff_lambda'], layer)
    q5 = qa.reshape(b, t, H_A, 2, DK_A).transpose(0, 2, 3, 1, 4)
    eye = jnp.eye(H_A * 2, dtype=BF16).reshape(H_A, 2, 1, H_A, 2, 1)
    q_exp = (q5[:, :, :, :, None, None, :] * eye[None]).reshape(b, H_A * 2 * t, H_A * 2 * DK_A)
    n_phys = cache_diff.shape[1]
    oa = _diff_attn_paged(page_table, lam, q_exp, cache_diff.reshape(-1, n_phys, PAGE * 2 * H_A, DV_A),
                          layer, new_diff.reshape(b, t, 2 * H_A * DV_A), w['diff_gain'], 16, post)
    oa = oa.reshape(m, H_A * DV_A).astype(BF16)

    rows = RET_CHUNK
    padr = lambda a: jnp.pad(a.reshape(b, t, a.shape[-1]), ((0, 0), (0, rows - t), (0, 0))).reshape(b * rows, -1)
    ob, ret = _retention(padr(qb), padr(kb), padr(u[:, COL_BV:COL_CQ]), 0, 2, ret_state, ret_tabs,
                         w['ret_gain'], b, 1, rows)
    ob = ob.reshape(b, rows, H_B * DV_B)[:, :t].reshape(m, H_B * DV_B)

    n_pages = page_table.shape[1]
    cache_rows = cache_nsa.reshape(-1, n_phys, PAGE * 4, DK_C)
    kvc = _compress_paged(page_table, cache_rows, layer, w['cmp_pe'], w['cmp_w'], 16).astype(BF16)
    qs = qc.reshape(b, t, H_C, DK_C).transpose(0, 2, 1, 3).reshape(b, H_C * t, DK_C)
    n_pick = TOPK - 1
    assert past_len % SEL_BLOCK == 0 and t <= SEL_BLOCK and past_len // SEL_BLOCK >= n_pick
    ocmp, idx = _nsa_sample_select(qs, kvc[:, 0, 0::2], kvc[:, 0, 1::2], kvc[:, 1, 0::2], kvc[:, 1, 1::2],
                                   t, past_len, n_pick)
    pad8 = lambda a: jnp.pad(a, ((0, 0),) * (a.ndim - 2) + ((0, 8 - H_C), (0, 0)))
    q8 = pad8(qc.astype(F32).reshape(b, t, H_C, DK_C))
    ocmp8 = pad8(ocmp.reshape(b, H_C, t, DK_C).transpose(0, 2, 1, 3))
    cg = u[:, COL_CG:COL_CG + 3 * H_C].reshape(b, t, 3, H_C)
    gexp = jnp.broadcast_to(pad8(cg[..., None]), (b, t, 3, 8, DK_C))
    oc = _nsa_sample_attend(page_table, idx[:, :, :TOPK].reshape(-1), q8, cache_rows, layer,
                            new_nsa.reshape(b, t, 4 * DK_C), win_state.reshape(b, -1, 2 * DK_C),
                            new_win.reshape(b, t, 2 * DK_C), ocmp8, gexp, n_pick, past_len)
    oc = oc[:, :, :H_C].reshape(m, H_C * DK_C).astype(BF16)

    cs = jnp.pad(conv_state, ((0, 0), (0, 0), (0, D_FF_PAD - D_FF)))
    zeros = jnp.zeros((b, t - 1, D_FF_PAD), F32)
    p1 = jnp.concatenate([cs[:, 1:2], zeros], axis=1).reshape(m, D_FF_PAD)
    p2 = jnp.concatenate([cs, zeros[:, 1:]], axis=1).reshape(m, D_FF_PAD)

    x3, x3b, conv = _dense_tail(x, xb, oa, ob, oc, p, w, layer, t, (p1, p2))
    nbuf = win_state.shape[1]
    win = jnp.concatenate([win_state, new_win.reshape(b, t, 2, DK_C)], axis=1)[:, t:]
    assert win.shape[1] == nbuf
    states = (new_diff.reshape(b, t, 2, H_A, DV_A), new_nsa.reshape(b, t, 4, DK_C), win, ret, conv)
    return x3, x3b, states


def _pad_last(a, n):
    return jnp.pad(a, ((0, 0),) * (a.ndim - 1) + ((0, n - a.shape[-1]),))


def _stacked_weights(w_in, nsa_cmp_w, w_branch_a, w_branch_b, w_branch_c, w_merge_gate, w_out, w_ffn_gate,
                     w_ffn_up, w_ffn_down, w_ple_gate, w_ple_proj):
    bf = lambda a: a.astype(BF16)
    return {
        'w_in': bf(_pad_last(w_in, N_IN_PAD)),
        'cmp_w': bf(nsa_cmp_w),
        'wa': bf(w_branch_a), 'wb': bf(w_branch_b), 'wc': bf(w_branch_c),
        'w_mg': bf(w_merge_gate), 'w_out': bf(w_out),
        'w_fg': _cast_pad_cols(w_ffn_gate, D_FF_PAD, 256), 'w_fu': _cast_pad_cols(w_ffn_up, D_FF_PAD, 256),
        'w_fd': _cast_pad_rows(w_ffn_down, D_FF_PAD, 688),
        'w_pg': bf(w_ple_gate), 'w_pp': bf(w_ple_proj),
    }


def _layer_vectors(i, ln_gain, ln_bias, diff_lambda, diff_norm_gain, ret_norm_gain, nsa_cmp_pos, ffn_conv_w,
                   ffn_conv_b):
    return {
        'diff_lambda': diff_lambda[i].astype(F32),
        'diff_gain': diff_norm_gain[i].reshape(1, DV_A),
        'ret_gain': ret_norm_gain[i].reshape(1, H_B * DV_B),
        'cmp_pe': nsa_cmp_pos[i],
        'cw': _pad_last(ffn_conv_w[i], D_FF_PAD), 'cb': _pad_last(ffn_conv_b[i].reshape(1, D_FF), D_FF_PAD),
        'g0': ln_gain[i, 0:1], 'g1': ln_gain[i, 1:2], 'g2': ln_gain[i, 2:3],
        'b0': ln_bias[i, 0:1], 'b1': ln_bias[i, 1:2], 'b2': ln_bias[i, 2:3],
    }


def kernel(x_prompt, x_sample, cache_diff_kv, cache_nsa_kv, state_nsa_win, state_ret, state_conv, page_table,
           p_prompt, p_sample, ln_gain, ln_bias, w_in, diff_lambda, diff_norm_gain, ret_norm_gain, nsa_cmp_pos,
           nsa_cmp_w, w_branch_a, w_branch_b, w_branch_c, w_merge_gate, w_out, w_ffn_gate, w_ffn_up, ffn_conv_w,
           ffn_conv_b, w_ffn_down, w_ple_gate, w_ple_proj):
    bp, tp, _ = x_prompt.shape
    bs, ts, _ = x_sample.shape
    n_layers = w_in.shape[0]
    past_len = page_table.shape[1] * PAGE

    rope_p = _rope_tables(jnp.arange(tp, dtype=jnp.int32))
    rope_s = tuple(jnp.tile(tb, (bs, 1)) for tb in _rope_tables(past_len + jnp.arange(ts, dtype=jnp.int32)))
    ret_p = _retention_tables(RET_CHUNK, RET_CHUNK)
    ret_s = _retention_tables(ts, RET_CHUNK)

    xp = x_prompt.reshape(bp * tp, D_MODEL)
    xs = x_sample.reshape(bs * ts, D_MODEL)
    xpb, xsb = xp.astype(BF16), xs.astype(BF16)
    st_p, st_s = [], []
    stacked = _stacked_weights(w_in, nsa_cmp_w, w_branch_a, w_branch_b, w_branch_c, w_merge_gate, w_out,
                               w_ffn_gate, w_ffn_up, w_ffn_down, w_ple_gate, w_ple_proj)
    for i in range(n_layers):
        w = dict(stacked, **_layer_vectors(i, ln_gain, ln_bias, diff_lambda, diff_norm_gain, ret_norm_gain,
                                           nsa_cmp_pos, ffn_conv_w, ffn_conv_b))
        xp, xpb, sp = _prompt_layer(xp, xpb, p_prompt[i].reshape(bp * tp, PLE_DIM), w, i, bp, tp, rope_p, ret_p)
        xs, xsb, ss = _sample_layer(xs, xsb, p_sample[i].reshape(bs * ts, PLE_DIM), w, i, bs, ts, past_len,
                                    rope_s, ret_s, page_table, cache_diff_kv, cache_nsa_kv, state_nsa_win[i],
                                    state_ret[i], state_conv[i])
        st_p.append(sp)
        st_s.append(ss)
    diff_p, nsa_p, win_p, ret_p_out, conv_p = [jnp.stack(s) for s in zip(*st_p)]
    diff_s, nsa_s, win_s, ret_s_out, conv_s = [jnp.stack(s) for s in zip(*st_s)]
    return (xp.reshape(bp, tp, D_MODEL), xs.reshape(bs, ts, D_MODEL), diff_p, diff_s, nsa_p, nsa_s,
            win_p, win_s, ret_p_out, ret_s_out, conv_p, conv_s)
```

```python
import functools
import math

import jax
import jax.numpy as jnp
from jax import lax
from jax.experimental import pallas as pl
from jax.experimental.pallas import tpu as pltpu

F32 = jnp.float32
BF16 = jnp.bfloat16

D_MODEL = 2048
H_A, DK_A, DV_A = 4, 64, 128
H_B, DK_B, DV_B = 4, 128, 256
RET_CHUNK = 128
RET_THETA = 10000.0
H_C, DK_C = 4, 128
CMP_BLOCK, SEL_BLOCK, TOPK, WINDOW = 32, 64, 16, 512
FORCED_SCORE = H_C + 1.0
ROPE_THETA = 500000.0
D_FF = 5504
CONV_W = 3
PLE_DIM = 256
N_LAYERS = 4
ALPHA = (2 * N_LAYERS) ** 0.25
EPS = 1e-5
PAGE = 128

N_IN = 5900
N_IN_PAD = 6144
COL_AQ, COL_AK, COL_AV = 0, 512, 1024
COL_BQ, COL_BK, COL_BV, COL_BG = 1536, 2048, 2560, 3584
COL_CQ, COL_CKV, COL_CWIN, COL_CG = 4608, 5120, 5632, 5888
D_FF_PAD = 5632
CMP_PITCH = 40

VMEM_LIMIT = 56 * 1024 * 1024

NEG = float(jnp.finfo(jnp.float32).min)
TINY = float(jnp.finfo(jnp.float32).tiny)


def _params(*sem):
    return pltpu.CompilerParams(dimension_semantics=sem, vmem_limit_bytes=VMEM_LIMIT)


def _dot(a, b):
    return jnp.dot(a, b, preferred_element_type=F32)


def _dot_nt(a, b):
    return lax.dot_general(a, b, (((1,), (1,)), ((), ())), preferred_element_type=F32)


def _dot_tn(a, b):
    return lax.dot_general(a, b, (((0,), (0,)), ((), ())), preferred_element_type=F32)


def _masked_softmax(s, mask):
    s = jnp.where(mask, s, NEG)
    e = jnp.where(mask, jnp.exp(s - jnp.max(s, axis=-1, keepdims=True)), 0.0)
    return e / jnp.maximum(jnp.sum(e, axis=-1, keepdims=True), TINY)


def _visible_softmax_terms(s, mask):
    s = jnp.where(mask, s, NEG)
    e = jnp.exp(s - jnp.max(s, axis=-1, keepdims=True))
    return e, jnp.sum(e, axis=-1, keepdims=True)


def _layer_norm(y, g, b):
    mu = jnp.mean(y, axis=-1, keepdims=True)
    d = y - mu
    var = jnp.mean(d * d, axis=-1, keepdims=True)
    return d * lax.rsqrt(var + EPS) * g + b


def _sigmoid(x):
    return 1.0 / (1.0 + jnp.exp(-x))


def _cast_pad_cols_kernel(x_ref, o_ref):
    n = x_ref.shape[1]
    o_ref[:, 0:n] = x_ref[...].astype(BF16)
    o_ref[:, n:] = jnp.zeros((o_ref.shape[0], o_ref.shape[1] - n), BF16)


def _cast_pad_cols(w, n_pad, tr):
    l, k, n = w.shape
    assert n % 128 == 0 and k % tr == 0
    return pl.pallas_call(
        _cast_pad_cols_kernel,
        grid=(l, k // tr),
        in_specs=[pl.BlockSpec((None, tr, n), lambda i, r: (i, r, 0))],
        out_specs=pl.BlockSpec((None, tr, n_pad), lambda i, r: (i, r, 0)),
        out_shape=jax.ShapeDtypeStruct((l, k, n_pad), BF16),
        compiler_params=_params("parallel", "parallel"),
        name="cast_pad_cols",
    )(w)


def _cast_pad_rows_kernel(x_ref, o_ref, *, n_in_blocks):
    r = pl.program_id(1)
    x = x_ref[...].astype(BF16)
    o_ref[...] = jnp.where(r < n_in_blocks, x, jnp.zeros_like(x))


def _cast_pad_rows(w, k_pad, tr):
    l, k, n = w.shape
    assert k % tr == 0 and tr % 16 == 0
    n_in = k // tr
    n_out = -(-k_pad // tr)
    return pl.pallas_call(
        functools.partial(_cast_pad_rows_kernel, n_in_blocks=n_in),
        grid=(l, n_out),
        in_specs=[pl.BlockSpec((None, tr, n), lambda i, r: (i, jnp.minimum(r, n_in - 1), 0))],
        out_specs=pl.BlockSpec((None, tr, n), lambda i, r: (i, r, 0)),
        out_shape=jax.ShapeDtypeStruct((l, k_pad, n), BF16),
        compiler_params=_params("parallel", "parallel"),
        name="cast_pad_rows",
    )(w)


def _mm_kernel(x_ref, w_ref, o_ref):
    o_ref[...] = _dot(x_ref[...], w_ref[...]).astype(o_ref.dtype)


def _matmul(x, w, layer, out_dtype, tm, tn, name):
    m, k = x.shape
    n = w.shape[2]
    return pl.pallas_call(
        _mm_kernel,
        grid=(n // tn, m // tm),
        in_specs=[pl.BlockSpec((tm, k), lambda j, i: (i, 0)),
                  pl.BlockSpec((None, k, tn), lambda j, i: (layer, 0, j))],
        out_specs=pl.BlockSpec((tm, tn), lambda j, i: (i, j)),
        out_shape=jax.ShapeDtypeStruct((m, n), out_dtype),
        compiler_params=_params("parallel", "parallel"),
        name=name,
    )(x, w)


def _merge_kernel(x_ref, w0_ref, w1_ref, w2_ref, oa_ref, ob_ref, oc_ref, wa_ref, wb_ref, wc_ref, o_ref):
    x = x_ref[...]
    m = _sigmoid(_dot(x, w0_ref[...])) * _dot(oa_ref[...], wa_ref[...])
    m = m + _sigmoid(_dot(x, w1_ref[...])) * _dot(ob_ref[...], wb_ref[...])
    m = m + _sigmoid(_dot(x, w2_ref[...])) * _dot(oc_ref[...], wc_ref[...])
    o_ref[...] = m.astype(o_ref.dtype)


def _merge(xb, w_mg, oa, ob, oc, wa, wb, wc, layer, tm, tn):
    m = xb.shape[0]
    nj = D_MODEL // tn
    row = lambda width: pl.BlockSpec((tm, width), lambda j, i: (i, 0))
    col = lambda k, off: pl.BlockSpec((None, k, tn), lambda j, i: (layer, 0, j + off))
    return pl.pallas_call(
        _merge_kernel,
        grid=(nj, m // tm),
        in_specs=[row(D_MODEL), col(D_MODEL, 0), col(D_MODEL, nj), col(D_MODEL, 2 * nj),
                  row(H_A * DV_A), row(H_B * DV_B), row(H_C * DK_C),
                  col(H_A * DV_A, 0), col(H_B * DV_B, 0), col(H_C * DK_C, 0)],
        out_specs=pl.BlockSpec((tm, tn), lambda j, i: (i, j)),
        out_shape=jax.ShapeDtypeStruct((m, D_MODEL), BF16),
        compiler_params=_params("parallel", "parallel"),
        name="merge_gate",
    )(xb, w_mg, w_mg, w_mg, oa, ob, oc, wa, wb, wc)


def _proj_ln_kernel(m_ref, w_ref, x_ref, g_ref, b_ref, o_ref, ob_ref):
    y = ALPHA * x_ref[...] + _dot(m_ref[...], w_ref[...])
    out = _layer_norm(y, g_ref[...], b_ref[...])
    o_ref[...] = out
    ob_ref[...] = out.astype(BF16)


def _resident_weight(k, n, layer):
    return pl.BlockSpec((None, k, n), lambda i: (layer, 0, 0), pipeline_mode=pl.Buffered(1))


def _proj_ln(mb, w, layer, x, g, b, tm):
    m = x.shape[0]
    k = mb.shape[1]
    row = lambda width: pl.BlockSpec((tm, width), lambda i: (i, 0))
    full = lambda r, c: pl.BlockSpec((r, c), lambda i: (0, 0))
    return pl.pallas_call(
        _proj_ln_kernel,
        grid=(m // tm,),
        in_specs=[row(k), _resident_weight(k, D_MODEL, layer), row(D_MODEL), full(1, D_MODEL), full(1, D_MODEL)],
        out_specs=[row(D_MODEL), row(D_MODEL)],
        out_shape=[jax.ShapeDtypeStruct((m, D_MODEL), F32), jax.ShapeDtypeStruct((m, D_MODEL), BF16)],
        compiler_params=_params("parallel"),
        name="out_proj_ln",
    )(mb, w, x, g, b)


def _ple_ln_kernel(xb_ref, wg_ref, p_ref, wp_ref, x_ref, g_ref, b_ref, o_ref, ob_ref):
    pe = _sigmoid(_dot(xb_ref[...], wg_ref[...])) * _dot(p_ref[...].astype(BF16), wp_ref[...])
    out = _layer_norm(ALPHA * x_ref[...] + pe, g_ref[...], b_ref[...])
    o_ref[...] = out
    ob_ref[...] = out.astype(BF16)


def _ple_ln(xb, wg, p, wp, layer, x, g, b, tm):
    m = x.shape[0]
    row = lambda width: pl.BlockSpec((tm, width), lambda i: (i, 0))
    full = lambda r, c: pl.BlockSpec((r, c), lambda i: (0, 0))
    return pl.pallas_call(
        _ple_ln_kernel,
        grid=(m // tm,),
        in_specs=[row(D_MODEL), _resident_weight(D_MODEL, D_MODEL, layer), row(PLE_DIM),
                  _resident_weight(PLE_DIM, D_MODEL, layer), row(D_MODEL), full(1, D_MODEL), full(1, D_MODEL)],
        out_specs=[row(D_MODEL), row(D_MODEL)],
        out_shape=[jax.ShapeDtypeStruct((m, D_MODEL), F32), jax.ShapeDtypeStruct((m, D_MODEL), BF16)],
        compiler_params=_params("parallel"),
        name="ple_ln",
    )(xb, wg, p, wp, x, g, b)


def _conv_silu(a, a1, a2, u, cw_ref, cb_ref):
    ac = cb_ref[...] + ((a2 * cw_ref[0:1, :] + a1 * cw_ref[1:2, :]) + a * cw_ref[2:3, :])
    return (ac * _sigmoid(ac)) * u


def _ffn_up_seq_kernel(x_ref, wg_ref, wu_ref, cw_ref, cb_ref, h_ref, st_ref, carry_ref, *, tiles_per_seq):
    i = pl.program_id(1)
    x = x_ref[...]
    a = _dot(x, wg_ref[...])
    u = _dot(x, wu_ref[...])
    tm = a.shape[0]

    @pl.when(i % tiles_per_seq == 0)
    def _():
        carry_ref[...] = jnp.zeros_like(carry_ref)

    prev = carry_ref[...]
    row = lax.broadcasted_iota(jnp.int32, a.shape, 0)
    a1 = jnp.where(row == 0, prev[7:8, :], pltpu.roll(a, 1, 0))
    a2 = jnp.where(row == 0, prev[6:7, :], jnp.where(row == 1, prev[7:8, :], pltpu.roll(a, 2, 0)))
    h_ref[...] = _conv_silu(a, a1, a2, u, cw_ref, cb_ref).astype(h_ref.dtype)
    tail = a[tm - 8:tm, :]
    carry_ref[...] = tail
    st_ref[...] = tail


def _ffn_up_seq(xb, wg, wu, layer, cw, cb, seq_len, tm, tn):
    m = xb.shape[0]
    tps = seq_len // tm
    return pl.pallas_call(
        functools.partial(_ffn_up_seq_kernel, tiles_per_seq=tps),
        grid=(D_FF_PAD // tn, m // tm),
        in_specs=[pl.BlockSpec((tm, D_MODEL), lambda j, i: (i, 0)),
                  pl.BlockSpec((None, D_MODEL, tn), lambda j, i: (layer, 0, j)),
                  pl.BlockSpec((None, D_MODEL, tn), lambda j, i: (layer, 0, j)),
                  pl.BlockSpec((CONV_W, tn), lambda j, i: (0, j)),
                  pl.BlockSpec((1, tn), lambda j, i: (0, j))],
        out_specs=[pl.BlockSpec((tm, tn), lambda j, i: (i, j)),
                   pl.BlockSpec((None, 8, tn), lambda j, i: (i // tps, 0, j))],
        out_shape=[jax.ShapeDtypeStruct((m, D_FF_PAD), BF16),
                   jax.ShapeDtypeStruct((m // seq_len, 8, D_FF_PAD), F32)],
        scratch_shapes=[pltpu.VMEM((8, tn), F32)],
        compiler_params=_params("arbitrary", "arbitrary"),
        name="ffn_up_conv_prompt",
    )(xb, wg, wu, cw, cb)


def _ffn_up_short_kernel(x_ref, wg_ref, wu_ref, cw_ref, cb_ref, p1_ref, p2_ref, h_ref, a_ref, *, seq_len):
    x = x_ref[...]
    a = _dot(x, wg_ref[...])
    u = _dot(x, wu_ref[...])
    t = lax.broadcasted_iota(jnp.int32, a.shape, 0) % seq_len
    a1 = jnp.where(t == 0, p1_ref[...], pltpu.roll(a, 1, 0))
    a2 = jnp.where(t < 2, p2_ref[...], pltpu.roll(a, 2, 0))
    h_ref[...] = _conv_silu(a, a1, a2, u, cw_ref, cb_ref).astype(h_ref.dtype)
    a_ref[...] = a


def _ffn_up_short(xb, wg, wu, layer, cw, cb, p1, p2, seq_len, tn):
    m = xb.shape[0]
    colb = lambda r: pl.BlockSpec((r, tn), lambda j: (0, j))
    wcol = pl.BlockSpec((None, D_MODEL, tn), lambda j: (layer, 0, j))
    return pl.pallas_call(
        functools.partial(_ffn_up_short_kernel, seq_len=seq_len),
        grid=(D_FF_PAD // tn,),
        in_specs=[pl.BlockSpec((m, D_MODEL), lambda j: (0, 0)), wcol, wcol,
                  colb(CONV_W), colb(1), colb(m), colb(m)],
        out_specs=[colb(m), colb(m)],
        out_shape=[jax.ShapeDtypeStruct((m, D_FF_PAD), BF16), jax.ShapeDtypeStruct((m, D_FF_PAD), F32)],
        compiler_params=_params("parallel"),
        name="ffn_up_conv_sample",
    )(xb, wg, wu, cw, cb, p1, p2)


def _ffn_down_ln_kernel(h_ref, w_ref, x_ref, g_ref, b_ref, o_ref, ob_ref, acc_ref):
    k = pl.program_id(1)

    @pl.when(k == 0)
    def _():
        acc_ref[...] = jnp.zeros_like(acc_ref)

    acc_ref[...] += _dot(h_ref[...], w_ref[...])

    @pl.when(k == pl.num_programs(1) - 1)
    def _():
        out = _layer_norm(ALPHA * x_ref[...] + acc_ref[...], g_ref[...], b_ref[...])
        o_ref[...] = out
        ob_ref[...] = out.astype(BF16)


def _ffn_down_ln(h, w, layer, x, g, b, tm, tk):
    m = x.shape[0]
    row = pl.BlockSpec((tm, D_MODEL), lambda i, k: (i, 0))
    vec = pl.BlockSpec((1, D_MODEL), lambda i, k: (0, 0))
    return pl.pallas_call(
        _ffn_down_ln_kernel,
        grid=(m // tm, D_FF_PAD // tk),
        in_specs=[pl.BlockSpec((tm, tk), lambda i, k: (i, k)),
                  pl.BlockSpec((None, tk, D_MODEL), lambda i, k: (layer, k, 0)), row, vec, vec],
        out_specs=[row, row],
        out_shape=[jax.ShapeDtypeStruct((m, D_MODEL), F32), jax.ShapeDtypeStruct((m, D_MODEL), BF16)],
        scratch_shapes=[pltpu.VMEM((tm, D_MODEL), F32)],
        compiler_params=_params("parallel", "arbitrary"),
        name="ffn_down_ln",
    )(h, w, x, g, b)


def _rope128(x, c, s_up, s_dn, shift):
    y = x * c + pltpu.roll(x, 128 - shift, 1) * s_up
    if s_dn is not None:
        y = y + pltpu.roll(x, shift, 1) * s_dn
    return y


def _rope_split_kernel(aq_ref, ak_ref, av_ref, bq_ref, bk_ref, cq_ref, ckv_ref, cw_ref,
                       ca_ref, sau_ref, sad_ref, cb_ref, sb_ref, cc_ref, scu_ref, scd_ref,
                       qa_ref, nd_ref, kva_ref, qb_ref, kb_ref, qc_ref, nn_ref, sw_ref, nw_ref, kcm_ref, vcm_ref):
    ca, sau, sad = ca_ref[...], sau_ref[...], sad_ref[...]
    cb, sb = cb_ref[...], sb_ref[...]
    cc, scu, scd = cc_ref[...], scu_ref[...], scd_ref[...]
    rot_a = DK_A // 8
    rot_c = DK_C // 8
    tm = aq_ref.shape[0]
    for h in range(4):
        sl = slice(h * 128, (h + 1) * 128)
        qa_ref[:, sl] = _rope128(aq_ref[:, sl], ca, sau, sad, rot_a).astype(BF16)
        ka = _rope128(ak_ref[:, sl], ca, sau, sad, rot_a)
        av = av_ref[:, sl]
        nd_ref[pl.ds(h, tm, stride=2 * H_A), :] = ka
        nd_ref[pl.ds(H_A + h, tm, stride=2 * H_A), :] = av
        kva_ref[:, sl] = ka.astype(BF16)
        kva_ref[:, 512 + h * 128:512 + (h + 1) * 128] = av.astype(BF16)
        qb_ref[:, sl] = _rope128(bq_ref[:, sl], cb, sb, None, DK_B // 2).astype(BF16)
        kb_ref[:, sl] = _rope128(bk_ref[:, sl], cb, sb, None, DK_B // 2) * (DK_B ** -0.5)
        qc_ref[:, sl] = _rope128(cq_ref[:, sl], cc, scu, scd, rot_c).astype(BF16)
    k_cmp = _rope128(ckv_ref[:, 0:128], cc, scu, scd, rot_c)
    v_cmp = ckv_ref[:, 128:256]
    k_sel = _rope128(ckv_ref[:, 256:384], cc, scu, scd, rot_c)
    v_sel = ckv_ref[:, 384:512]
    k_win = _rope128(cw_ref[:, 0:128], cc, scu, scd, rot_c)
    v_win = cw_ref[:, 128:256]
    for comp, val in enumerate((k_cmp, v_cmp, k_sel, v_sel)):
        nn_ref[pl.ds(comp, tm, stride=4), :] = val
    sw_ref[:, 0:128] = k_sel.astype(BF16)
    sw_ref[:, 128:256] = v_sel.astype(BF16)
    sw_ref[:, 256:384] = k_win.astype(BF16)
    sw_ref[:, 384:512] = v_win.astype(BF16)
    nw_ref[:, 0:128] = k_win
    nw_ref[:, 128:256] = v_win
    kcm_ref[...] = k_cmp
    vcm_ref[...] = v_cmp


def _rope_split(u, tables, tm):
    m = u.shape[0]
    n_tab = tables[0].shape[0] // tm
    ub = lambda width, blk: pl.BlockSpec((tm, width), lambda i: (i, blk))
    tab = pl.BlockSpec((tm, 128), lambda i: (i % n_tab, 0))
    out = lambda width: pl.BlockSpec((tm, width), lambda i: (i, 0))
    shp = lambda width, dt: jax.ShapeDtypeStruct((m, width), dt)
    return pl.pallas_call(
        _rope_split_kernel,
        grid=(m // tm,),
        in_specs=[ub(512, COL_AQ // 512), ub(512, COL_AK // 512), ub(512, COL_AV // 512),
                  ub(512, COL_BQ // 512), ub(512, COL_BK // 512), ub(512, COL_CQ // 512),
                  ub(512, COL_CKV // 512), ub(256, COL_CWIN // 256)] + [tab] * 8,
        out_specs=[out(512), pl.BlockSpec((tm * 2 * H_A, DV_A), lambda i: (i, 0)), out(1024), out(512), out(512),
                   out(512), pl.BlockSpec((tm * 4, DK_C), lambda i: (i, 0)), out(512), out(256), out(128), out(128)],
        out_shape=[shp(512, BF16), jax.ShapeDtypeStruct((m * 2 * H_A, DV_A), F32), shp(1024, BF16), shp(512, BF16),
                   shp(512, F32), shp(512, BF16), jax.ShapeDtypeStruct((m * 4, DK_C), F32), shp(512, BF16),
                   shp(256, F32), shp(128, F32), shp(128, F32)],
        compiler_params=_params("parallel"),
        name="rope_split",
    )(u, u, u, u, u, u, u, u, *tables)


def _rope_tables(pos):
    posf = pos.astype(F32)[:, None]
    lane = jnp.arange(128)

    def cs(half, theta):
        inv_freq = jnp.exp(-math.log(theta) * jnp.arange(half, dtype=F32) / half)
        ang = posf * inv_freq[None, :]
        return jnp.cos(ang), jnp.sin(ang)

    def partial_tables(head_dim, half, theta):
        cos, sin = cs(half, theta)
        d = lane % head_dim
        lo = d < half
        hi = (d >= half) & (d < 2 * half)
        idx = jnp.where(lo, d, jnp.where(hi, d - half, 0))
        c = jnp.where((lo | hi)[None, :], cos[:, idx], 1.0)
        s_up = jnp.where(lo[None, :], -sin[:, idx], 0.0)
        s_dn = jnp.where(hi[None, :], sin[:, idx], 0.0)
        return c, s_up, s_dn

    ca, sau, sad = partial_tables(DK_A, DK_A // 8, ROPE_THETA)
    cc, scu, scd = partial_tables(DK_C, DK_C // 8, ROPE_THETA)
    cosb, sinb = cs(DK_B // 2, RET_THETA)
    cb = jnp.concatenate([cosb, cosb], axis=1)
    sb = jnp.concatenate([-sinb, sinb], axis=1)
    return (ca, sau, sad, cb, sb, cc, scu, scd)


def _rms_head(o, gain, post_scale):
    return o * lax.rsqrt(jnp.mean(o * o, axis=-1, keepdims=True) + EPS) * gain * post_scale


def _diff_attn_prompt_kernel(lam_ref, q_ref, k_ref, v_ref, gain_ref, o_ref, *, tq, post_scale):
    qi = pl.program_id(2)
    t = k_ref.shape[0]
    n_head = q_ref.shape[1] // 128
    lane = lax.broadcasted_iota(jnp.int32, (tq, 128), 1)
    scale = DK_A ** -0.5
    diag = lax.broadcasted_iota(jnp.int32, (tq, tq), 1) <= lax.broadcasted_iota(jnp.int32, (tq, tq), 0)

    def tile_terms(qh, hs, lo):
        s_d = jnp.where(diag, _dot_nt(qh, k_ref[lo:lo + tq, hs]), NEG)
        mx = jnp.max(s_d, axis=-1, keepdims=True)
        if lo:
            s_lo = _dot_nt(qh, k_ref[0:lo, hs])
            mx = jnp.maximum(mx, jnp.max(s_lo, axis=-1, keepdims=True))
        e_d = jnp.exp(s_d - mx)
        l = jnp.sum(e_d, axis=-1, keepdims=True)
        o = _dot(e_d.astype(BF16), v_ref[lo:lo + tq, hs])
        if lo:
            e_lo = jnp.exp(s_lo - mx)
            l = l + jnp.sum(e_lo, axis=-1, keepdims=True)
            o = o + _dot(e_lo.astype(BF16), v_ref[0:lo, hs])
        return o, l

    for qv in range(t // tq):
        @pl.when(qi == qv)
        def _(lo=qv * tq):
            for hh in range(n_head):
                hs = slice(hh * 128, (hh + 1) * 128)
                q = q_ref[:, hs]
                zero = jnp.zeros_like(q)
                o1, l1 = tile_terms(jnp.where(lane < DK_A, q, zero) * scale, hs, lo)
                o2, l2 = tile_terms(jnp.where(lane >= DK_A, q, zero) * scale, hs, lo)
                o = o1 * (1.0 / l1) - (lam_ref[0] / l2) * o2
                o_ref[:, hs] = _rms_head(o, gain_ref[...], post_scale).astype(o_ref.dtype)


def _key_buckets(t, tq):
    n_bucket = 16
    return n_bucket if t % (n_bucket * 128) == 0 and (t // n_bucket) % tq == 0 else 1


def _diff_attn_prompt(lam, qa, kva, gain, b, t, tq, post_scale):
    m = qa.shape[0]
    nq = t // tq
    return pl.pallas_call(
        functools.partial(_diff_attn_prompt_kernel, tq=tq, post_scale=post_scale),
        grid=(b, H_A // 2, nq),
        in_specs=[pl.BlockSpec(memory_space=pltpu.SMEM),
                  pl.BlockSpec((tq, 256), lambda bi, h, qi: (bi * nq + qi, h)),
                  pl.BlockSpec((t, 256), lambda bi, h, qi: (bi, h)),
                  pl.BlockSpec((t, 256), lambda bi, h, qi: (bi, H_A // 2 + h)),
                  pl.BlockSpec((1, 128), lambda bi, h, qi: (0, 0))],
        out_specs=pl.BlockSpec((tq, 256), lambda bi, h, qi: (bi * nq + qi, h)),
        out_shape=jax.ShapeDtypeStruct((m, H_A * DV_A), BF16),
        compiler_params=_params("parallel", "parallel", "parallel"),
        name="diff_attn_prompt",
    )(lam, qa, kva, kva, gain)


def _diff_attn_paged_kernel(pt_ref, lam_ref, q_ref, *rest, n_group, n_new, post_scale):
    page_refs = rest[:n_group]
    kvn_ref, gain_ref, o_ref, m_ref, l_ref, acc_ref = rest[n_group:]
    j = pl.program_id(1)
    scale = DK_A ** -0.5
    q = q_ref[...]
    hk = H_A * 2 * DK_A

    @pl.when(j == 0)
    def _():
        m_ref[...] = jnp.full_like(m_ref, NEG)
        l_ref[...] = jnp.zeros_like(l_ref)
        acc_ref[...] = jnp.zeros_like(acc_ref)

    def update(s, v):
        m_old = m_ref[...]
        m_new = jnp.maximum(m_old, jnp.max(s, axis=-1, keepdims=True))
        a = jnp.exp(m_old - m_new)
        p = jnp.exp(s - m_new[:, 0:1])
        l_ref[...] = a * l_ref[...] + jnp.sum(p, axis=-1, keepdims=True)
        acc_ref[...] = a[:, 0:1] * acc_ref[...] + _dot(p.astype(BF16), v)
        m_ref[...] = m_new

    def heads(r, first):
        return jnp.concatenate([r[pl.ds(first + h, PAGE, stride=2 * H_A), :].astype(BF16) for h in range(H_A)],
                               axis=1)

    k = jnp.concatenate([heads(r, 0) for r in page_refs], axis=0)
    v = jnp.concatenate([heads(r, H_A) for r in page_refs], axis=0)
    update(_dot_nt(q, k) * scale, v)

    @pl.when(j == pl.num_programs(1) - 1)
    def _():
        pad = jnp.zeros((PAGE - n_new, 2 * hk), F32)
        kvn = jnp.concatenate([kvn_ref[...], pad], axis=0)
        s = _dot_nt(q, kvn[:, 0:hk].astype(BF16)) * scale
        tq = lax.broadcasted_iota(jnp.int32, s.shape, 0) % n_new
        tk = lax.broadcasted_iota(jnp.int32, s.shape, 1)
        update(jnp.where(tk <= tq, s, NEG), kvn[:, hk:2 * hk].astype(BF16))
        o = acc_ref[...] / l_ref[:, 0:1]
        lam = lam_ref[0]
        for h in range(H_A):
            r0 = h * 2 * n_new
            sl = slice(h * DV_A, (h + 1) * DV_A)
            oh = o[r0:r0 + n_new, sl] - lam * o[r0 + n_new:r0 + 2 * n_new, sl]
            o_ref[:, sl] = _rms_head(oh, gain_ref[...], post_scale)


def _diff_attn_paged(page_table, lam, q_exp, cache, layer, kv_new, gain, n_group, post_scale):
    b, n_pages = page_table.shape
    n_new = kv_new.shape[1]
    rows = q_exp.shape[1]
    width = kv_new.shape[2]
    pt = page_table.reshape(-1)

    def page_spec(g):
        return pl.BlockSpec((None, None, PAGE * 2 * H_A, DV_A),
                            lambda bi, j, pt_ref: (layer, pt_ref[bi * n_pages + j * n_group + g], 0, 0))

    grid_spec = pltpu.PrefetchScalarGridSpec(
        num_scalar_prefetch=1,
        grid=(b, n_pages // n_group),
        in_specs=[pl.BlockSpec(memory_space=pltpu.SMEM),
                  pl.BlockSpec((None, rows, q_exp.shape[2]), lambda bi, j, pt_ref: (bi, 0, 0))]
                 + [page_spec(g) for g in range(n_group)]
                 + [pl.BlockSpec((None, n_new, width), lambda bi, j, pt_ref: (bi, 0, 0)),
                    pl.BlockSpec((1, DV_A), lambda bi, j, pt_ref: (0, 0))],
        out_specs=pl.BlockSpec((None, n_new, H_A * DV_A), lambda bi, j, pt_ref: (bi, 0, 0)),
        scratch_shapes=[pltpu.VMEM((rows, 128), F32), pltpu.VMEM((rows, 128), F32),
                        pltpu.VMEM((rows, H_A * DV_A), F32)],
    )
    return pl.pallas_call(
        functools.partial(_diff_attn_paged_kernel, n_group=n_group, n_new=n_new, post_scale=post_scale),
        grid_spec=grid_spec,
        out_shape=jax.ShapeDtypeStruct((b, n_new, H_A * DV_A), F32),
        compiler_params=_params("parallel", "arbitrary"),
        name="diff_attn_paged",
    )(pt, lam, q_exp, *([cache] * n_group), kv_new, gain)


def _retention_kernel(q_ref, k_ref, v0_ref, v1_ref, g0_ref, g1_ref, s0_ref, dm_ref, qd_ref, kd_ref, cd_ref,
                      gain_ref, o_ref, sf_ref, s_ref):
    c = pl.program_id(1)

    @pl.when(c == 0)
    def _():
        s_ref[...] = s0_ref[...]

    for h in range(H_B):
        ks = slice(h * DK_B, (h + 1) * DK_B)
        half = slice((h % 2) * DV_B, (h % 2 + 1) * DV_B)
        v_ref, g_ref = (v0_ref, g0_ref) if h < 2 else (v1_ref, g1_ref)
        q = q_ref[:, ks]
        k = k_ref[:, ks]
        v = v_ref[:, half].astype(BF16)
        s = s_ref[h]
        att = _dot_nt(q, k.astype(BF16)) * dm_ref[h]
        o = _dot(att.astype(BF16), v) + _dot(q, s.astype(BF16)) * qd_ref[h]
        s_new = s * cd_ref[h] + _dot_tn((k * kd_ref[h]).astype(BF16), v)
        s_ref[h] = s_new
        sf_ref[h] = s_new
        mu = jnp.mean(o, axis=-1, keepdims=True)
        d = o - mu
        var = jnp.mean(d * d, axis=-1, keepdims=True)
        y = d * lax.rsqrt(var + EPS) * gain_ref[:, h * DV_B:(h + 1) * DV_B]
        g = g_ref[:, half]
        o_ref[:, h * DV_B:(h + 1) * DV_B] = (y * (g * _sigmoid(g))).astype(o_ref.dtype)


def _retention(q, k, vg, v_blk, g_blk, s0, tables, gain, b, nc, chunk):
    m = q.shape[0]
    dmask, qdec, kdec, cdec = tables
    row = lambda width, blk: pl.BlockSpec((chunk, width), lambda bi, c: (bi * nc + c, blk))
    full = lambda a: pl.BlockSpec(a.shape, lambda bi, c: (0,) * a.ndim)
    state = pl.BlockSpec((None, H_B, DK_B, DV_B), lambda bi, c: (bi, 0, 0, 0))
    return pl.pallas_call(
        _retention_kernel,
        grid=(b, nc),
        in_specs=[row(H_B * DK_B, 0), row(H_B * DK_B, 0),
                  row(2 * DV_B, v_blk), row(2 * DV_B, v_blk + 1), row(2 * DV_B, g_blk), row(2 * DV_B, g_blk + 1),
                  state, full(dmask), full(qdec), full(kdec), full(cdec), full(gain)],
        out_specs=[row(H_B * DV_B, 0), state],
        out_shape=[jax.ShapeDtypeStruct((m, H_B * DV_B), BF16),
                   jax.ShapeDtypeStruct((b, H_B, DK_B, DV_B), F32)],
        scratch_shapes=[pltpu.VMEM((H_B, DK_B, DV_B), F32)],
        compiler_params=_params("parallel", "arbitrary"),
        name="retention",
    )(q, k, vg, vg, vg, vg, s0, dmask, qdec, kdec, cdec, gain)


def _retention_tables(chunk, rows):
    log_g = jnp.log(1.0 - jnp.exp2(-5.0 - jnp.arange(H_B, dtype=F32)))
    idx = jnp.arange(chunk, dtype=F32)
    rel = idx[:, None] - idx[None, :]
    dmask = jnp.where(rel >= 0, jnp.exp(log_g[:, None, None] * jnp.maximum(rel, 0.0)), 0.0)
    q_dec = jnp.exp(log_g[:, None] * (idx[None, :] + 1.0))
    k_dec = jnp.exp(log_g[:, None] * (chunk - 1.0 - idx[None, :]))
    c_dec = jnp.exp(log_g * chunk)
    padn = rows - chunk
    dmask = jnp.pad(dmask, ((0, 0), (0, padn), (0, padn)))
    q_dec = jnp.pad(q_dec, ((0, 0), (0, padn)))
    k_dec = jnp.pad(k_dec, ((0, 0), (0, padn)))
    qd = jnp.broadcast_to(q_dec[:, :, None], (H_B, rows, DV_B))
    kd = jnp.broadcast_to(k_dec[:, :, None], (H_B, rows, DK_B))
    cd = jnp.broadcast_to(c_dec[:, None, None], (H_B, 1, DV_B))
    return dmask, qd, kd, cd


def _compress_kernel(x_ref, pe_ref, w_ref, o_ref):
    o_ref[...] = _dot((x_ref[...] + pe_ref[...]).astype(BF16), w_ref[...])


def _compress(x, pe, w, layer):
    nb = x.shape[1]
    kdim = CMP_BLOCK * DK_C
    return pl.pallas_call(
        _compress_kernel,
        grid=(2,),
        in_specs=[pl.BlockSpec((None, nb, kdim), lambda i: (i, 0, 0)),
                  pl.BlockSpec((None, 1, kdim), lambda i: (i, 0, 0)),
                  pl.BlockSpec((None, None, kdim, DK_C), lambda i: (layer, i, 0, 0))],
        out_specs=pl.BlockSpec((None, nb, DK_C), lambda i: (i, 0, 0)),
        out_shape=jax.ShapeDtypeStruct((2, nb, DK_C), F32),
        compiler_params=_params("parallel"),
        name="nsa_compress_prompt",
    )(x, pe, w)


def _cmp_attention(qs, qpos, kce, kco, vce, vco):
    scale = DK_C ** -0.5
    nh = kce.shape[0]
    n = lax.broadcasted_iota(jnp.int32, (1, nh), 1)
    me = (2 * CMP_BLOCK * n + (CMP_BLOCK - 1)) <= qpos
    mo = (2 * CMP_BLOCK * n + (2 * CMP_BLOCK - 1)) <= qpos
    se = jnp.where(me, _dot_nt(qs, kce) * scale, NEG)
    so = jnp.where(mo, _dot_nt(qs, kco) * scale, NEG)
    mx = jnp.maximum(jnp.max(se, axis=-1, keepdims=True), jnp.max(so, axis=-1, keepdims=True))
    ee = jnp.where(me, jnp.exp(se - mx), 0.0)
    eo = jnp.where(mo, jnp.exp(so - mx), 0.0)
    den = jnp.maximum(jnp.sum(ee, axis=-1, keepdims=True) + jnp.sum(eo, axis=-1, keepdims=True), TINY)
    pe = ee / den
    po = eo / den
    o = _dot(pe.astype(BF16), vce) + _dot(po.astype(BF16), vco)
    return o, pe, po


def _top_blocks(score, n_pick):
    blk = lax.broadcasted_iota(jnp.int32, score.shape, 1).astype(F32)
    big = 1e9
    work = score
    sel = jnp.zeros(score.shape, F32)
    idxs, vals = [], []
    for _ in range(n_pick):
        mval = jnp.max(work, axis=-1, keepdims=True)
        idx = jnp.min(jnp.where(work == mval, blk, big), axis=-1, keepdims=True)
        pick = blk == idx
        sel = jnp.where(pick, jnp.where(mval >= 0.0, 1.0, 0.0), sel)
        work = jnp.where(pick, -2.0, work)
        idxs.append(idx.astype(jnp.int32))
        vals.append(mval)
    return sel, idxs, vals


def _block_scores(imp, qpos):
    blk = lax.broadcasted_iota(jnp.int32, imp.shape, 1)
    cur = lax.shift_right_arithmetic(qpos, int(math.log2(SEL_BLOCK)))
    forced = jnp.where(blk == 0, 1, 0) + jnp.where(blk == cur, 1, 0) + jnp.where(blk == cur - 1, 1, 0)
    valid = blk * SEL_BLOCK <= qpos
    return jnp.where(valid, jnp.where(forced > 0, FORCED_SCORE, imp), -1.0)


def _nsa_prompt_kernel(q_ref, kce_ref, kco_ref, vce_ref, vco_ref, sw_ref, g_ref, o_ref, osel_ref, *,
                       tq, n_bucket, n_sel):
    qi = pl.program_id(1)
    scale = DK_C ** -0.5
    q = q_ref[...]
    qs = jnp.concatenate([q[:, h * DK_C:(h + 1) * DK_C] for h in range(H_C)], axis=0)
    qpos1 = qi * tq + lax.broadcasted_iota(jnp.int32, (tq, 1), 0)
    qpos = jnp.concatenate([qpos1] * H_C, axis=0)
    qrow1 = qi * tq + lax.broadcasted_iota(jnp.int32, (1, tq), 1)
    qrow = jnp.concatenate([qrow1] * H_C, axis=1)
    t = sw_ref.shape[0]
    ncp = kce_ref.shape[0]

    n = lax.broadcasted_iota(jnp.int32, (ncp, 1), 0)
    me = (2 * CMP_BLOCK * n + (CMP_BLOCK - 1)) <= qrow
    mo = (2 * CMP_BLOCK * n + (2 * CMP_BLOCK - 1)) <= qrow
    se = jnp.where(me, _dot_nt(kce_ref[...], qs) * scale, NEG)
    so = jnp.where(mo, _dot_nt(kco_ref[...], qs) * scale, NEG)
    mx = jnp.maximum(jnp.max(se, axis=0, keepdims=True), jnp.max(so, axis=0, keepdims=True))
    ee = jnp.where(me, jnp.exp(se - mx), 0.0)
    eo = jnp.where(mo, jnp.exp(so - mx), 0.0)
    den = jnp.maximum(jnp.sum(ee, axis=0, keepdims=True) + jnp.sum(eo, axis=0, keepdims=True), TINY)
    pe = ee / den
    po = eo / den
    o_cmp = _dot(pe.T.astype(BF16), vce_ref[...]) + _dot(po.T.astype(BF16), vco_ref[...])
    spe = pe[:, 0:tq] + pe[:, tq:2 * tq] + pe[:, 2 * tq:3 * tq] + pe[:, 3 * tq:4 * tq]
    spo = po[:, 0:tq] + po[:, tq:2 * tq] + po[:, 2 * tq:3 * tq] + po[:, 3 * tq:4 * tq]

    imp = (spe + spo)[0:n_sel]
    blk = lax.broadcasted_iota(jnp.int32, (n_sel, tq), 0)
    cur = lax.shift_right_arithmetic(qrow1, int(math.log2(SEL_BLOCK)))
    forced = jnp.where(blk == 0, 1, 0) + jnp.where(blk == cur, 1, 0) + jnp.where(blk == cur - 1, 1, 0)
    score = jnp.where(blk * SEL_BLOCK <= qrow1, jnp.where(forced > 0, FORCED_SCORE, imp), -1.0)
    blkf = blk.astype(F32)
    sel_t = jnp.zeros((n_sel, tq), F32)
    for _ in range(min(TOPK, n_sel)):
        mval = jnp.max(score, axis=0, keepdims=True)
        idx = jnp.min(jnp.where(score == mval, blkf, 1e9), axis=0, keepdims=True)
        pick = blkf == idx
        sel_t = jnp.where(pick, jnp.where(mval >= 0.0, 1.0, 0.0), sel_t)
        score = jnp.where(pick, -2.0, score)
    selb = jnp.concatenate([sel_t, jnp.zeros((ncp - n_sel, tq), F32)], axis=0).T.astype(BF16)

    kb = t // n_bucket
    bucket = lax.div((qi + 1) * tq - 1, kb)
    for nb in range(n_bucket):
        @pl.when(bucket == nb)
        def _(nkeys=(nb + 1) * kb):
            key_blk = lax.shift_right_arithmetic(lax.broadcasted_iota(jnp.int32, (ncp, nkeys), 1),
                                                 int(math.log2(SEL_BLOCK)))
            expand = jnp.where(key_blk == lax.broadcasted_iota(jnp.int32, (ncp, nkeys), 0), 1.0, 0.0).astype(BF16)
            kpos = lax.broadcasted_iota(jnp.int32, (1, nkeys), 1)
            selk1 = jnp.where(kpos <= qpos1, _dot(selb, expand), 0.0)
            selk = jnp.concatenate([selk1] * H_C, axis=0) > 0.5
            e, l = _visible_softmax_terms(_dot_nt(qs, sw_ref[0:nkeys, 0:128]) * scale, selk)
            osel_ref[...] = _dot(e.astype(BF16), sw_ref[0:nkeys, 128:256]) * (1.0 / l)
    o_sel = osel_ref[...]

    wlen = min(t, WINDOW + tq)
    start = pl.multiple_of(jnp.clip(qi * tq - WINDOW, 0, t - wlen), 128) if wlen < t else 0
    wpos = start + lax.broadcasted_iota(jnp.int32, (1, wlen), 1)
    d = qpos - wpos
    wmask = jnp.abs(2 * d - WINDOW) <= WINDOW
    e, l = _visible_softmax_terms(_dot_nt(qs, sw_ref[pl.ds(start, wlen), 256:384]) * scale, wmask)
    o_win = _dot(e.astype(BF16), sw_ref[pl.ds(start, wlen), 384:512]) * (1.0 / l)

    g = _sigmoid(g_ref[...])
    for h in range(H_C):
        r = slice(h * tq, (h + 1) * tq)
        oc = (g[:, h:h + 1] * o_cmp[r] + g[:, H_C + h:H_C + h + 1] * o_sel[r]
              + g[:, 2 * H_C + h:2 * H_C + h + 1] * o_win[r])
        o_ref[:, h * DK_C:(h + 1) * DK_C] = oc.astype(o_ref.dtype)


def _nsa_prompt(qc, kce, kco, vce, vco, sw, u, b, t, tq):
    m = qc.shape[0]
    nq = t // tq
    nh = kce.shape[1]
    n_sel = t // SEL_BLOCK
    assert nh == 128 and n_sel <= nh and t % SEL_BLOCK == 0 and 2 * CMP_BLOCK * n_sel >= t
    cm = pl.BlockSpec((None, nh, DK_C), lambda bi, qi: (bi, 0, 0))
    return pl.pallas_call(
        functools.partial(_nsa_prompt_kernel, tq=tq, n_bucket=_key_buckets(t, tq), n_sel=n_sel),
        grid=(b, nq),
        scratch_shapes=[pltpu.VMEM((H_C * tq, DK_C), F32)],
        in_specs=[pl.BlockSpec((tq, H_C * DK_C), lambda bi, qi: (bi * nq + qi, 0)), cm, cm, cm, cm,
                  pl.BlockSpec((t, 512), lambda bi, qi: (bi, 0)),
                  pl.BlockSpec((tq, 128), lambda bi, qi: (bi * nq + qi, COL_CG // 128))],
        out_specs=pl.BlockSpec((tq, H_C * DK_C), lambda bi, qi: (bi * nq + qi, 0)),
        out_shape=jax.ShapeDtypeStruct((m, H_C * DK_C), BF16),
        compiler_params=_params("parallel", "parallel"),
        name="nsa_prompt",
    )(qc, kce, kco, vce, vco, sw, u)


def _compress_paged_kernel(pt_ref, *rest, n_group):
    page_refs = rest[:n_group]
    pe_ref, w_ref, o_ref, xk_ref, xv_ref = rest[n_group:]
    j = pl.program_id(1)
    per_page = PAGE // CMP_BLOCK
    for g, r in enumerate(page_refs):
        for n in range(per_page):
            blk = (j * n_group + g) * per_page + n
            dst = pl.ds(pl.multiple_of(blk * CMP_PITCH, 8), CMP_BLOCK)
            xk_ref[dst, :] = r[pl.ds(n * CMP_BLOCK * 4, CMP_BLOCK, stride=4), :]
            xv_ref[dst, :] = r[pl.ds(n * CMP_BLOCK * 4 + 1, CMP_BLOCK, stride=4), :]

    @pl.when(j == pl.num_programs(1) - 1)
    def _():
        nb = xk_ref.shape[0] // CMP_PITCH
        acc_k = jnp.zeros((nb, DK_C), F32)
        acc_v = jnp.zeros((nb, DK_C), F32)
        for i in range(0, CMP_BLOCK, 2):
            def pair(x_ref, c):
                return jnp.concatenate(
                    [x_ref[pl.ds(i + d, nb, stride=CMP_PITCH), :] + pe_ref[c, i + d:i + d + 1, :] for d in (0, 1)],
                    axis=1).astype(BF16)
            wsl = slice(i * DK_C, (i + 2) * DK_C)
            acc_k = acc_k + _dot(pair(xk_ref, 0), w_ref[0, wsl, :])
            acc_v = acc_v + _dot(pair(xv_ref, 1), w_ref[1, wsl, :])
        o_ref[0] = acc_k
        o_ref[1] = acc_v


def _compress_paged(page_table, cache, layer, pe, w, n_group):
    b, n_pages = page_table.shape
    pt = page_table.reshape(-1)
    nb = n_pages * PAGE // CMP_BLOCK

    def page_spec(g):
        return pl.BlockSpec((None, None, PAGE * 4, DK_C),
                            lambda bi, j, pt_ref: (layer, pt_ref[bi * n_pages + j * n_group + g], 0, 0))

    grid_spec = pltpu.PrefetchScalarGridSpec(
        num_scalar_prefetch=1,
        grid=(b, n_pages // n_group),
        in_specs=[page_spec(g) for g in range(n_group)]
                 + [pl.BlockSpec((2, CMP_BLOCK, DK_C), lambda bi, j, pt_ref: (0, 0, 0)),
                    pl.BlockSpec((None, 2, CMP_BLOCK * DK_C, DK_C), lambda bi, j, pt_ref: (layer, 0, 0, 0))],
        out_specs=pl.BlockSpec((None, 2, nb, DK_C), lambda bi, j, pt_ref: (bi, 0, 0, 0)),
        scratch_shapes=[pltpu.VMEM((nb * CMP_PITCH, DK_C), F32), pltpu.VMEM((nb * CMP_PITCH, DK_C), F32)],
    )
    return pl.pallas_call(
        functools.partial(_compress_paged_kernel, n_group=n_group),
        grid_spec=grid_spec,
        out_shape=jax.ShapeDtypeStruct((b, 2, nb, DK_C), F32),
        compiler_params=_params("parallel", "arbitrary"),
        name="nsa_compress_paged",
    )(pt, *([cache] * n_group), pe, w)


def _nsa_sample_select_kernel(q_ref, kce_ref, kco_ref, vce_ref, vco_ref, o_ref, idx_ref, *, n_new, past_len, n_pick):
    qs = q_ref[...]
    rows = qs.shape[0]
    qpos = past_len + lax.broadcasted_iota(jnp.int32, (rows, 1), 0) % n_new
    o_cmp, pe, po = _cmp_attention(qs, qpos, kce_ref[...], kco_ref[...], vce_ref[...], vco_ref[...])
    o_ref[...] = o_cmp
    spe = pe[0:n_new]
    spo = po[0:n_new]
    for h in range(1, H_C):
        spe = spe + pe[h * n_new:(h + 1) * n_new]
        spo = spo + po[h * n_new:(h + 1) * n_new]
    score = _block_scores(spe + spo, qpos[0:n_new])
    _, idxs, _ = _top_blocks(score, n_pick)
    lane = lax.broadcasted_iota(jnp.int32, (n_new, 128), 1)
    out = jnp.zeros((n_new, 128), jnp.int32)
    for r, idx in enumerate(idxs):
        out = jnp.where(lane == r, idx, out)
    idx_ref[...] = out


def _nsa_sample_select(qs, kce, kco, vce, vco, n_new, past_len, n_pick):
    b, rows, _ = qs.shape
    nh = kce.shape[1]
    cm = pl.BlockSpec((None, nh, DK_C), lambda bi: (bi, 0, 0))
    return pl.pallas_call(
        functools.partial(_nsa_sample_select_kernel, n_new=n_new, past_len=past_len, n_pick=n_pick),
        grid=(b,),
        in_specs=[pl.BlockSpec((None, rows, DK_C), lambda bi: (bi, 0, 0)), cm, cm, cm, cm],
        out_specs=[pl.BlockSpec((None, rows, DK_C), lambda bi: (bi, 0, 0)),
                   pl.BlockSpec((None, n_new, 128), lambda bi: (bi, 0, 0))],
        out_shape=[jax.ShapeDtypeStruct((b, rows, DK_C), F32), jax.ShapeDtypeStruct((b, n_new, 128), jnp.int32)],
        compiler_params=_params("parallel"),
        name="nsa_sample_select",
    )(qs, kce, kco, vce, vco)


def _nsa_sample_attend_kernel(pt_ref, ix_ref, q_ref, *rest, n_pick, n_new, past_len):
    blk_refs = rest[:n_pick]
    new_ref, wst_ref, wnew_ref, ocmp_ref, g_ref, o_ref = rest[n_pick:]
    bi = pl.program_id(0)
    qi = pl.program_id(1)
    scale = DK_C ** -0.5
    q = q_ref[...].astype(BF16)
    qpos = past_len + qi
    lane_blk = lax.broadcasted_iota(jnp.int32, (1, SEL_BLOCK), 1)

    ks = [r[pl.ds(2, SEL_BLOCK, stride=4), :].astype(BF16) for r in blk_refs]
    vs = [r[pl.ds(3, SEL_BLOCK, stride=4), :].astype(BF16) for r in blk_refs]
    kpos = [ix_ref[(bi * n_new + qi) * TOPK + r] * SEL_BLOCK + lane_blk for r in range(n_pick)]
    new = jnp.concatenate([new_ref[...], jnp.zeros((SEL_BLOCK - n_new, 2 * DK_C), F32)], axis=0)
    ks.append(new[:, 0:DK_C].astype(BF16))
    vs.append(new[:, DK_C:2 * DK_C].astype(BF16))
    kpos.append(past_len + lane_blk)
    s = _dot_nt(q, jnp.concatenate(ks, axis=0)) * scale
    mask = jnp.concatenate(kpos, axis=1) <= qpos
    o_sel = _dot(_masked_softmax(s, mask).astype(BF16), jnp.concatenate(vs, axis=0))

    nbuf = wst_ref.shape[0]
    wnew = jnp.concatenate([wnew_ref[...], jnp.zeros((PAGE - n_new, 2 * DK_C), F32)], axis=0)
    kw = jnp.concatenate([wst_ref[:, 0:DK_C].astype(BF16), wnew[:, 0:DK_C].astype(BF16)], axis=0)
    vw = jnp.concatenate([wst_ref[:, DK_C:2 * DK_C].astype(BF16), wnew[:, DK_C:2 * DK_C].astype(BF16)], axis=0)
    wpos = past_len - nbuf + lax.broadcasted_iota(jnp.int32, (1, nbuf + PAGE), 1)
    d = qpos - wpos
    wmask = jnp.where(wpos >= 0, jnp.abs(2 * d - WINDOW), 4 * WINDOW) <= WINDOW
    s = _dot_nt(q, kw) * scale
    o_win = _dot(_masked_softmax(s, wmask).astype(BF16), vw)

    g = _sigmoid(g_ref[...])
    o_ref[...] = g[0] * ocmp_ref[...] + g[1] * o_sel + g[2] * o_win


def _nsa_sample_attend(page_table, idx, q8, cache, layer, nsa_new, win_state, win_new, ocmp8, gexp,
                       n_pick, past_len):
    b, n_pages = page_table.shape
    n_new = q8.shape[1]
    pt = page_table.reshape(-1)
    nbuf = win_state.shape[1]
    per_page_log2 = int(math.log2(PAGE // SEL_BLOCK))

    def blk_spec(r):
        def imap(bi, qi, pt_ref, ix_ref):
            blk = ix_ref[(bi * n_new + qi) * TOPK + r]
            page = pt_ref[bi * n_pages + lax.shift_right_logical(blk, per_page_log2)]
            return (layer, page, jnp.bitwise_and(blk, (1 << per_page_log2) - 1), 0)
        return pl.BlockSpec((None, None, SEL_BLOCK * 4, DK_C), imap)

    per_q = lambda: pl.BlockSpec((None, None, 8, DK_C), lambda bi, qi, p, x: (bi, qi, 0, 0))
    grid_spec = pltpu.PrefetchScalarGridSpec(
        num_scalar_prefetch=2,
        grid=(b, n_new),
        in_specs=[per_q()] + [blk_spec(r) for r in range(n_pick)]
                 + [pl.BlockSpec((None, n_new, 2 * DK_C), lambda bi, qi, p, x: (bi, 0, 1)),
                    pl.BlockSpec((None, nbuf, 2 * DK_C), lambda bi, qi, p, x: (bi, 0, 0)),
                    pl.BlockSpec((None, n_new, 2 * DK_C), lambda bi, qi, p, x: (bi, 0, 0)),
                    per_q(),
                    pl.BlockSpec((None, None, 3, 8, DK_C), lambda bi, qi, p, x: (bi, qi, 0, 0, 0))],
        out_specs=per_q(),
    )
    return pl.pallas_call(
        functools.partial(_nsa_sample_attend_kernel, n_pick=n_pick, n_new=n_new, past_len=past_len),
        grid_spec=grid_spec,
        out_shape=jax.ShapeDtypeStruct((b, n_new, 8, DK_C), F32),
        compiler_params=_params("parallel", "parallel"),
        name="nsa_sample_attend",
    )(pt, idx, q8, *([cache] * n_pick), nsa_new, win_state, win_new, ocmp8, gexp)


def _tile(m, pref):
    if m <= pref:
        return m
    t = pref - pref % 128
    while m % t:
        t -= 128
    assert t > 0, (m, pref)
    return t


def _lambda_scalar(lp, layer):
    lam_init = 0.8 - 0.6 * math.exp(-0.3 * layer)
    lam = jnp.exp(jnp.sum(lp[0] * lp[1])) - jnp.exp(jnp.sum(lp[2] * lp[3])) + lam_init
    return lam.reshape(1).astype(F32), 1.0 - lam_init


def _dense_tail(x, xb, oa, ob, oc, p, w, layer, seq_len, conv_prev):
    m = x.shape[0]
    short = seq_len < 128
    mix = _merge(xb, w['w_mg'], oa, ob, oc, w['wa'], w['wb'], w['wc'], layer, _tile(m, 512), 512)
    x1, x1b = _proj_ln(mix, w['w_out'], layer, x, w['g0'], w['b0'], _tile(m, 512))
    if short:
        p1, p2 = conv_prev
        h, a = _ffn_up_short(x1b, w['w_fg'], w['w_fu'], layer, w['cw'], w['cb'], p1, p2, seq_len, 512)
        conv = a.reshape(m // seq_len, seq_len, D_FF_PAD)[:, seq_len - (CONV_W - 1):, :D_FF]
    else:
        h, st = _ffn_up_seq(x1b, w['w_fg'], w['w_fu'], layer, w['cw'], w['cb'], seq_len, _tile(seq_len, 2048), 512)
        conv = st[:, 8 - (CONV_W - 1):, :D_FF]
    x2, x2b = _ffn_down_ln(h, w['w_fd'], layer, x1, w['g1'], w['b1'], _tile(m, 512), D_FF_PAD // 4)
    x3, x3b = _ple_ln(x2b, w['w_pg'], p, w['w_pp'], layer, x2, w['g2'], w['b2'], _tile(m, 512))
    return x3, x3b, conv


def _prompt_layer(x, xb, p, w, layer, b, t, rope_tabs, ret_tabs):
    m = b * t
    u = _matmul(xb, w['w_in'], layer, F32, _tile(m, 1024), 1024, "in_proj")
    (qa, new_diff, kva, qb, kb, qc, new_nsa, sw, new_win, kcm, vcm) = _rope_split(u, rope_tabs, _tile(t, 256))

    lam, post = _lambda_scalar(w['diff_lambda'], layer)
    oa = _diff_attn_prompt(lam, qa, kva, w['diff_gain'], b, t, _tile(t, 256), post)

    chunk = RET_CHUNK
    s0 = jnp.zeros((b, H_B, DK_B, DV_B), F32)
    ob, ret = _retention(qb, kb, u, COL_BV // 512, COL_BG // 512, s0, ret_tabs, w['ret_gain'], b, t // chunk, chunk)

    nb = m // CMP_BLOCK
    xc = jnp.stack([kcm.reshape(nb, CMP_BLOCK * DK_C), vcm.reshape(nb, CMP_BLOCK * DK_C)])
    kvc = _compress(xc, w['cmp_pe'].reshape(2, 1, CMP_BLOCK * DK_C), w['cmp_w'], layer)
    kvc = kvc.reshape(2, b, t // CMP_BLOCK, DK_C).astype(BF16)
    half = lambda a: jnp.pad(a, ((0, 0), (0, 128 - a.shape[1]), (0, 0)))
    oc = _nsa_prompt(qc, half(kvc[0, :, 0::2]), half(kvc[0, :, 1::2]), half(kvc[1, :, 0::2]),
                     half(kvc[1, :, 1::2]), sw, u, b, t, _tile(t, 128))

    x3, x3b, conv = _dense_tail(x, xb, oa, ob, oc, p, w, layer, t, None)
    nwin = min(WINDOW, t)
    states = (new_diff.reshape(b, t, 2, H_A, DV_A), new_nsa.reshape(b, t, 4, DK_C),
              new_win.reshape(b, t, 2, DK_C)[:, t - nwin:], ret, conv)
    return x3, x3b, states


def _sample_layer(x, xb, p, w, layer, b, t, past_len, rope_tabs, ret_tabs, page_table,
                  cache_diff, cache_nsa, win_state, ret_state, conv_state):
    m = b * t
    u = _matmul(xb, w['w_in'], layer, F32, m, 1024, "in_proj")
    (qa, new_diff, kva, qb, kb, qc, new_nsa, sw, new_win, kcm, vcm) = _rope_split(u, rope_tabs, m)

    lam, post = _lambda_scalar(w['diff_lambda'], layer)
    q5 = qa.reshape(b, t, H_A, 2, DK_A).transpose(0, 2, 3, 1, 4)
    eye = jnp.eye(H_A * 2, dtype=BF16).reshape(H_A, 2, 1, H_A, 2, 1)
    q_exp = (q5[:, :, :, :, None, None, :] * eye[None]).reshape(b, H_A * 2 * t, H_A * 2 * DK_A)
    n_phys = cache_diff.shape[1]
    oa = _diff_attn_paged(page_table, lam, q_exp, cache_diff.reshape(-1, n_phys, PAGE * 2 * H_A, DV_A),
                          layer, new_diff.reshape(b, t, 2 * H_A * DV_A), w['diff_gain'], 16, post)
    oa = oa.reshape(m, H_A * DV_A).astype(BF16)

    rows = RET_CHUNK
    padr = lambda a: jnp.pad(a.reshape(b, t, a.shape[-1]), ((0, 0), (0, rows - t), (0, 0))).reshape(b * rows, -1)
    ob, ret = _retention(padr(qb), padr(kb), padr(u[:, COL_BV:COL_CQ]), 0, 2, ret_state, ret_tabs,
                         w['ret_gain'], b, 1, rows)
    ob = ob.reshape(b, rows, H_B * DV_B)[:, :t].reshape(m, H_B * DV_B)

    n_pages = page_table.shape[1]
    cache_rows = cache_nsa.reshape(-1, n_phys, PAGE * 4, DK_C)
    kvc = _compress_paged(page_table, cache_rows, layer, w['cmp_pe'], w['cmp_w'], 32).astype(BF16)
    qs = qc.reshape(b, t, H_C, DK_C).transpose(0, 2, 1, 3).reshape(b, H_C * t, DK_C)
    n_pick = TOPK - 1
    assert past_len % SEL_BLOCK == 0 and t <= SEL_BLOCK and past_len // SEL_BLOCK >= n_pick
    ocmp, idx = _nsa_sample_select(qs, kvc[:, 0, 0::2], kvc[:, 0, 1::2], kvc[:, 1, 0::2], kvc[:, 1, 1::2],
                                   t, past_len, n_pick)
    pad8 = lambda a: jnp.pad(a, ((0, 0),) * (a.ndim - 2) + ((0, 8 - H_C), (0, 0)))
    q8 = pad8(qc.astype(F32).reshape(b, t, H_C, DK_C))
    ocmp8 = pad8(ocmp.reshape(b, H_C, t, DK_C).transpose(0, 2, 1, 3))
    cg = u[:, COL_CG:COL_CG + 3 * H_C].reshape(b, t, 3, H_C)
    gexp = jnp.broadcast_to(pad8(cg[..., None]), (b, t, 3, 8, DK_C))
    oc = _nsa_sample_attend(page_table, idx[:, :, :TOPK].reshape(-1), q8, cache_rows, layer,
                            new_nsa.reshape(b, t, 4 * DK_C), win_state.reshape(b, -1, 2 * DK_C),
                            new_win.reshape(b, t, 2 * DK_C), ocmp8, gexp, n_pick, past_len)
    oc = oc[:, :, :H_C].reshape(m, H_C * DK_C).astype(BF16)

    cs = jnp.pad(conv_state, ((0, 0), (0, 0), (0, D_FF_PAD - D_FF)))
    zeros = jnp.zeros((b, t - 1, D_FF_PAD), F32)
    p1 = jnp.concatenate([cs[:, 1:2], zeros], axis=1).reshape(m, D_FF_PAD)
    p2 = jnp.concatenate([cs, zeros[:, 1:]], axis=1).reshape(m, D_FF_PAD)

    x3, x3b, conv = _dense_tail(x, xb, oa, ob, oc, p, w, layer, t, (p1, p2))
    nbuf = win_state.shape[1]
    win = jnp.concatenate([win_state, new_win.reshape(b, t, 2, DK_C)], axis=1)[:, t:]
    assert win.shape[1] == nbuf
    states = (new_diff.reshape(b, t, 2, H_A, DV_A), new_nsa.reshape(b, t, 4, DK_C), win, ret, conv)
    return x3, x3b, states


def _pad_last(a, n):
    return jnp.pad(a, ((0, 0),) * (a.ndim - 1) + ((0, n - a.shape[-1]),))


def _stacked_weights(w_in, nsa_cmp_w, w_branch_a, w_branch_b, w_branch_c, w_merge_gate, w_out, w_ffn_gate,
                     w_ffn_up, w_ffn_down, w_ple_gate, w_ple_proj):
    bf = lambda a: a.astype(BF16)
    return {
        'w_in': bf(_pad_last(w_in, N_IN_PAD)),
        'cmp_w': bf(nsa_cmp_w),
        'wa': bf(w_branch_a), 'wb': bf(w_branch_b), 'wc': bf(w_branch_c),
        'w_mg': bf(w_merge_gate), 'w_out': bf(w_out),
        'w_fg': _cast_pad_cols(w_ffn_gate, D_FF_PAD, 256), 'w_fu': _cast_pad_cols(w_ffn_up, D_FF_PAD, 256),
        'w_fd': _cast_pad_rows(w_ffn_down, D_FF_PAD, 688),
        'w_pg': bf(w_ple_gate), 'w_pp': bf(w_ple_proj),
    }


def _layer_vectors(i, ln_gain, ln_bias, diff_lambda, diff_norm_gain, ret_norm_gain, nsa_cmp_pos, ffn_conv_w,
                   ffn_conv_b):
    return {
        'diff_lambda': diff_lambda[i].astype(F32),
        'diff_gain': diff_norm_gain[i].reshape(1, DV_A),
        'ret_gain': ret_norm_gain[i].reshape(1, H_B * DV_B),
        'cmp_pe': nsa_cmp_pos[i],
        'cw': _pad_last(ffn_conv_w[i], D_FF_PAD), 'cb': _pad_last(ffn_conv_b[i].reshape(1, D_FF), D_FF_PAD),
        'g0': ln_gain[i, 0:1], 'g1': ln_gain[i, 1:2], 'g2': ln_gain[i, 2:3],
        'b0': ln_bias[i, 0:1], 'b1': ln_bias[i, 1:2], 'b2': ln_bias[i, 2:3],
    }


def kernel(x_prompt, x_sample, cache_diff_kv, cache_nsa_kv, state_nsa_win, state_ret, state_conv, page_table,
           p_prompt, p_sample, ln_gain, ln_bias, w_in, diff_lambda, diff_norm_gain, ret_norm_gain, nsa_cmp_pos,
           nsa_cmp_w, w_branch_a, w_branch_b, w_branch_c, w_merge_gate, w_out, w_ffn_gate, w_ffn_up, ffn_conv_w,
           ffn_conv_b, w_ffn_down, w_ple_gate, w_ple_proj):
    bp, tp, _ = x_prompt.shape
    bs, ts, _ = x_sample.shape
    n_layers = w_in.shape[0]
    past_len = page_table.shape[1] * PAGE

    rope_p = _rope_tables(jnp.arange(tp, dtype=jnp.int32))
    rope_s = tuple(jnp.tile(tb, (bs, 1)) for tb in _rope_tables(past_len + jnp.arange(ts, dtype=jnp.int32)))
    ret_p = _retention_tables(RET_CHUNK, RET_CHUNK)
    ret_s = _retention_tables(ts, RET_CHUNK)

    xp = x_prompt.reshape(bp * tp, D_MODEL)
    xs = x_sample.reshape(bs * ts, D_MODEL)
    xpb, xsb = xp.astype(BF16), xs.astype(BF16)
    st_p, st_s = [], []
    stacked = _stacked_weights(w_in, nsa_cmp_w, w_branch_a, w_branch_b, w_branch_c, w_merge_gate, w_out,
                               w_ffn_gate, w_ffn_up, w_ffn_down, w_ple_gate, w_ple_proj)
    for i in range(n_layers):
        w = dict(stacked, **_layer_vectors(i, ln_gain, ln_bias, diff_lambda, diff_norm_gain, ret_norm_gain,
                                           nsa_cmp_pos, ffn_conv_w, ffn_conv_b))
        xp, xpb, sp = _prompt_layer(xp, xpb, p_prompt[i].reshape(bp * tp, PLE_DIM), w, i, bp, tp, rope_p, ret_p)
        xs, xsb, ss = _sample_layer(xs, xsb, p_sample[i].reshape(bs * ts, PLE_DIM), w, i, bs, ts, past_len,
                                    rope_s, ret_s, page_table, cache_diff_kv, cache_nsa_kv, state_nsa_win[i],
                                    state_ret[i], state_conv[i])
        st_p.append(sp)
        st_s.append(ss)
    diff_p, nsa_p, win_p, ret_p_out, conv_p = [jnp.stack(s) for s in zip(*st_p)]
    diff_s, nsa_s, win_s, ret_s_out, conv_s = [jnp.stack(s) for s in zip(*st_s)]
    return (xp.reshape(bp, tp, D_MODEL), xs.reshape(bs, ts, D_MODEL), diff_p, diff_s, nsa_p, nsa_s,
            win_p, win_s, ret_p_out, ret_s_out, conv_p, conv_s)
```
